```python
import math
import jax, jax.numpy as jnp
from jax import lax
import numpy as np

D_MODEL = 1024
BATCH = 2
SEQ = 8192
DEPTH = 1

D_MIX = D_MODEL
D_POOL = D_MIX // 4
POOL_WINDOWS = (2, 4, 8, 16)
N_POOL_GROUPS = len(POOL_WINDOWS)
POOL_GROUP = D_POOL // N_POOL_GROUPS
D_RWKV = D_MIX - D_POOL
HEAD_SIZE = 64
N_RWKV_HEADS = D_RWKV // HEAD_SIZE
D_DECAY_LORA = 64
D_AAA_LORA = 64
D_GATE_LORA = 128
D_RWKV_IN = 3 * D_RWKV + D_DECAY_LORA + D_AAA_LORA + D_GATE_LORA
D_IN = D_POOL + D_RWKV_IN

N_GROUPS = 4
EXPERTS_PER_GROUP = 8
N_EXPERTS = N_GROUPS * EXPERTS_PER_GROUP
TOP_K = 2
D_EXPERT = D_MODEL // 4
ROW_BLOCK = 128

LN_EPS = 1e-5
LNX_EPS = 64e-5
DEEPNORM_ALPHA = (2.0 * DEPTH) ** 0.25
DEEPNORM_BETA = (8.0 * DEPTH) ** -0.25

kernel_name = "hybrid_pool_rwkv7_hiermoe_deepnorm"


def layer_norm(x, g, b):
    xf = x.astype(jnp.float32)
    mu = jnp.mean(xf, axis=-1, keepdims=True)
    var = jnp.mean(jnp.square(xf - mu), axis=-1, keepdims=True)
    return ((xf - mu) * lax.rsqrt(var + LN_EPS) * g.astype(jnp.float32) + b.astype(jnp.float32)).astype(x.dtype)


def multiscale_pool(p, pool_w, pool_scale):
    b, s, _ = p.shape
    pf = p.astype(jnp.float32)
    csum = jnp.cumsum(pf, axis=1)
    count = jnp.arange(1, s + 1, dtype=jnp.float32)[None, :, None]
    outs = []
    for gi, win in enumerate(POOL_WINDOWS):
        c = csum[..., gi * POOL_GROUP:(gi + 1) * POOL_GROUP]
        c_lag = jnp.pad(c, ((0, 0), (win, 0), (0, 0)))[:, :s]
        mean = (c - c_lag) / jnp.minimum(count, float(win))
        outs.append(mean - pf[..., gi * POOL_GROUP:(gi + 1) * POOL_GROUP])
    diff = jnp.stack(outs, axis=2)
    mixed = jnp.einsum('bsgc,gcd->bsgd', diff, pool_w.astype(jnp.float32)).reshape(b, s, D_POOL)
    return (mixed * pool_scale.astype(jnp.float32)).astype(p.dtype)


def rwkv7_step(state, inp):
    r, decay, k, v, kk, a = inp
    sa = jnp.einsum('bhij,bhj->bhi', state, -kk)
    state = (state * decay[:, :, None, :]
             + sa[..., None] * (kk * a)[:, :, None, :]
             + v[..., None] * k[:, :, None, :])
    out = jnp.einsum('bhij,bhj->bhi', state, r)
    return state, out


def rwkv7_mix(z, mu_shift, w0, w_up, a0, a_up, g_up, k_k, k_a, r_k, lnx_g, lnx_b):
    b, s, _ = z.shape
    f32 = jnp.float32
    zf = z.astype(f32)
    prev = jnp.pad(zf, ((0, 0), (1, 0), (0, 0)))[:, :s]
    zm = zf + (prev - zf) * mu_shift.astype(f32)
    o = 0
    r = zm[..., o:o + D_RWKV]; o += D_RWKV
    k = zm[..., o:o + D_RWKV]; o += D_RWKV
    v = zm[..., o:o + D_RWKV]; o += D_RWKV
    wd = zm[..., o:o + D_DECAY_LORA]; o += D_DECAY_LORA
    ad = zm[..., o:o + D_AAA_LORA]; o += D_AAA_LORA
    gd = zm[..., o:o + D_GATE_LORA]
    w_log = -jax.nn.softplus(-(w0.astype(f32) + jnp.tanh(wd) @ w_up.astype(f32))) - 0.5
    decay = jnp.exp(-jnp.exp(w_log))
    a = jax.nn.sigmoid(a0.astype(f32) + ad @ a_up.astype(f32))
    g = jax.nn.sigmoid(gd) @ g_up.astype(f32)
    kk = k * k_k.astype(f32)
    k = k * (1.0 + (a - 1.0) * k_a.astype(f32))
    hs = lambda t: t.reshape(b, s, N_RWKV_HEADS, HEAD_SIZE)
    r, k, v, kk, a, decay = hs(r), hs(k), hs(v), hs(kk), hs(a), hs(decay)
    kk = kk / jnp.maximum(jnp.sqrt(jnp.sum(kk * kk, axis=-1, keepdims=True)), 1e-12)
    tmaj = lambda t: jnp.moveaxis(t, 1, 0)
    state0 = jnp.zeros((b, N_RWKV_HEADS, HEAD_SIZE, HEAD_SIZE), f32)
    _, y = lax.scan(rwkv7_step, state0, (tmaj(r), tmaj(decay), tmaj(k), tmaj(v), tmaj(kk), tmaj(a)))
    y = jnp.moveaxis(y, 0, 1)
    mu = jnp.mean(y, axis=-1, keepdims=True)
    var = jnp.mean(jnp.square(y - mu), axis=-1, keepdims=True)
    y = ((y - mu) * lax.rsqrt(var + LNX_EPS)).reshape(b, s, D_RWKV) * lnx_g.astype(f32) + lnx_b.astype(f32)
    bonus = jnp.sum(r * k * r_k.astype(f32), axis=-1, keepdims=True) * v
    y = (y + bonus.reshape(b, s, D_RWKV)) * g
    return y.astype(z.dtype)


def hierarchical_moe(h, router_group, router_group_b, router_expert, router_expert_b,
                     exp_gate, exp_up, exp_down):
    n, d = h.shape
    hf = h.astype(jnp.float32)
    g_prob = jax.nn.softmax(hf @ router_group.astype(jnp.float32) + router_group_b.astype(jnp.float32), axis=-1)
    g_top_p, g_idx = lax.top_k(g_prob, 1)
    e_logits = (hf @ router_expert.astype(jnp.float32) + router_expert_b.astype(jnp.float32)
                ).reshape(n, N_GROUPS, EXPERTS_PER_GROUP)
    e_sel = jnp.take_along_axis(e_logits, g_idx[:, :, None], axis=1)[:, 0]
    e_top_logit, e_idx = lax.top_k(e_sel, TOP_K)
    weights = g_top_p * jax.nn.softmax(e_top_logit, axis=-1)
    expert_id = g_idx * EXPERTS_PER_GROUP + e_idx

    m = n * TOP_K
    flat_e = expert_id.reshape(m).astype(jnp.int32)
    flat_tok = jnp.repeat(jnp.arange(n, dtype=jnp.int32), TOP_K)
    flat_w = weights.reshape(m)
    order = jnp.argsort(flat_e)
    e_sorted, tok_sorted, w_sorted = flat_e[order], flat_tok[order], flat_w[order]
    counts = jnp.bincount(flat_e, length=N_EXPERTS)
    padded = (counts + ROW_BLOCK - 1) // ROW_BLOCK * ROW_BLOCK
    start = jnp.cumsum(counts) - counts
    pend = jnp.cumsum(padded)
    pstart = pend - padded
    dest = pstart[e_sorted] + (jnp.arange(m, dtype=jnp.int32) - start[e_sorted])
    m_pad = (m + ROW_BLOCK - 1) // ROW_BLOCK * ROW_BLOCK + N_EXPERTS * ROW_BLOCK
    n_blocks = m_pad // ROW_BLOCK
    row_tok = jnp.zeros((m_pad,), jnp.int32).at[dest].set(tok_sorted)
    row_w = jnp.zeros((m_pad,), h.dtype).at[dest].set(w_sorted.astype(h.dtype))
    block_start = jnp.arange(n_blocks, dtype=jnp.int32) * ROW_BLOCK
    block_expert = jnp.minimum(jnp.searchsorted(pend, block_start, side='right'), N_EXPERTS - 1)

    def expert_block(args):
        tok, wgt, e = args
        xb = h[tok]
        hid = jax.nn.silu(xb @ exp_gate[e]) * (xb @ exp_up[e])
        return (hid @ exp_down[e]) * wgt[:, None]

    ys = lax.map(expert_block, (row_tok.reshape(n_blocks, ROW_BLOCK),
                                row_w.reshape(n_blocks, ROW_BLOCK), block_expert))
    return jnp.zeros((n, d), h.dtype).at[row_tok].add(ys.reshape(m_pad, d))


def hybrid_layer(x, w_in, pool_w, pool_scale, mu_shift, w0, w_up, a0, a_up, g_up, k_k, k_a, r_k,
                 lnx_g, lnx_b, w_out, ln1_g, ln1_b, router_group, router_group_b, router_expert,
                 router_expert_b, exp_gate, exp_up, exp_down, ln2_g, ln2_b):
    b, s, d = x.shape
    proj = x @ w_in
    pool_out = multiscale_pool(proj[..., :D_POOL], pool_w, pool_scale)
    rwkv_out = rwkv7_mix(proj[..., D_POOL:], mu_shift, w0, w_up, a0, a_up, g_up,
                         k_k, k_a, r_k, lnx_g, lnx_b)
    mixed = jnp.concatenate([pool_out, rwkv_out], axis=-1) @ w_out
    h = layer_norm(DEEPNORM_ALPHA * x + mixed, ln1_g, ln1_b)
    ffn = hierarchical_moe(h.reshape(b * s, d), router_group, router_group_b, router_expert,
                           router_expert_b, exp_gate, exp_up, exp_down).reshape(b, s, d)
    return layer_norm(DEEPNORM_ALPHA * h + ffn, ln2_g, ln2_b)


def setup_inputs(seed: int = 0) -> dict:
    key = jax.random.key(seed)
    ks = jax.random.split(key, 28)
    nrm = lambda k, shape, scale: jax.random.normal(k, shape, jnp.float32) * scale
    L = DEPTH
    return {
        "x": nrm(ks[0], (BATCH, SEQ, D_MODEL), 1.0),
        "w_in": nrm(ks[1], (L, D_MODEL, D_IN), D_MODEL ** -0.5),
        "pool_w": nrm(ks[2], (L, N_POOL_GROUPS, POOL_GROUP, POOL_GROUP), POOL_GROUP ** -0.5),
        "pool_scale": 1.0 + nrm(ks[3], (L, D_POOL), 0.1),
        "mu_shift": jax.random.uniform(ks[4], (L, D_RWKV_IN), jnp.float32),
        "w0": jnp.linspace(-6.0, -1.0, D_RWKV, dtype=jnp.float32)[None, :] + nrm(ks[5], (L, D_RWKV), 0.1),
        "w_up": nrm(ks[6], (L, D_DECAY_LORA, D_RWKV), 0.1),
        "a0": nrm(ks[7], (L, D_RWKV), 0.1),
        "a_up": nrm(ks[8], (L, D_AAA_LORA, D_RWKV), 0.1),
        "g_up": nrm(ks[9], (L, D_GATE_LORA, D_RWKV), D_GATE_LORA ** -0.5),
        "k_k": 0.85 + nrm(ks[10], (L, D_RWKV), 0.05),
        "k_a": 1.0 + nrm(ks[11], (L, D_RWKV), 0.05),
        "r_k": nrm(ks[12], (L, N_RWKV_HEADS, HEAD_SIZE), 0.1),
        "lnx_g": 1.0 + nrm(ks[13], (L, D_RWKV), 0.1),
        "lnx_b": nrm(ks[14], (L, D_RWKV), 0.01),
        "w_out": nrm(ks[15], (L, D_MIX, D_MODEL), D_MIX ** -0.5 * DEEPNORM_BETA),
        "ln1_g": 1.0 + nrm(ks[16], (L, D_MODEL), 0.05),
        "ln1_b": nrm(ks[17], (L, D_MODEL), 0.01),
        "router_group": nrm(ks[18], (L, D_MODEL, N_GROUPS), D_MODEL ** -0.5),
        "router_group_b": nrm(ks[19], (L, N_GROUPS), 0.01),
        "router_expert": nrm(ks[20], (L, D_MODEL, N_EXPERTS), D_MODEL ** -0.5),
        "router_expert_b": nrm(ks[21], (L, N_EXPERTS), 0.01),
        "exp_gate": nrm(ks[22], (L, N_EXPERTS, D_MODEL, D_EXPERT), D_MODEL ** -0.5),
        "exp_up": nrm(ks[23], (L, N_EXPERTS, D_MODEL, D_EXPERT), D_MODEL ** -0.5),
        "exp_down": nrm(ks[24], (L, N_EXPERTS, D_EXPERT, D_MODEL), D_EXPERT ** -0.5 * DEEPNORM_BETA),
        "ln2_g": 1.0 + nrm(ks[25], (L, D_MODEL), 0.05),
        "ln2_b": nrm(ks[26], (L, D_MODEL), 0.01),
    }


def reference(x, w_in, pool_w, pool_scale, mu_shift, w0, w_up, a0, a_up, g_up, k_k, k_a, r_k,
              lnx_g, lnx_b, w_out, ln1_g, ln1_b, router_group, router_group_b, router_expert,
              router_expert_b, exp_gate, exp_up, exp_down, ln2_g, ln2_b):
    for l in range(DEPTH):
        x = hybrid_layer(x, w_in[l], pool_w[l], pool_scale[l], mu_shift[l], w0[l], w_up[l], a0[l],
                         a_up[l], g_up[l], k_k[l], k_a[l], r_k[l], lnx_g[l], lnx_b[l], w_out[l],
                         ln1_g[l], ln1_b[l], router_group[l], router_group_b[l], router_expert[l],
                         router_expert_b[l], exp_gate[l], exp_up[l], exp_down[l], ln2_g[l], ln2_b[l])
    return x
```

```python
import functools

import jax
import jax.numpy as jnp
from jax import lax
from jax.experimental import pallas as pl
from jax.experimental.pallas import tpu as pltpu

F32 = jnp.float32
BF16 = jnp.bfloat16

D_MODEL = 1024
D_POOL = 256
POOL_WINDOWS = (2, 4, 8, 16)
POOL_GROUP = 64
POOL_HALO = 16
D_RWKV = 768
HEAD = 64
D_DECAY_LORA = 64
D_AAA_LORA = 64
D_GATE_LORA = 128
D_RWKV_IN = 3 * D_RWKV + D_DECAY_LORA + D_AAA_LORA + D_GATE_LORA
D_IN = D_POOL + D_RWKV_IN
N_GROUPS = 4
EXPERTS_PER_GROUP = 8
N_EXPERTS = 32
D_EXPERT = 256
LN_EPS = 1e-5
LNX_EPS = 64e-5
ALPHA = 2.0 ** 0.25

LANES = 128
VMEM_LIMIT = 56 * 1024 * 1024

PAIR = 2 * HEAD
N_PAIRS = D_RWKV // PAIR
CHUNK = 64
TS = 256
BM = 256
EXPERT_LANE0 = 32


def _dot(a, b):
    return jnp.dot(a.astype(BF16), b.astype(BF16), preferred_element_type=F32)


def _dot_nt(a, b):
    return lax.dot_general(a.astype(BF16), b.astype(BF16), (((1,), (1,)), ((), ())),
                           preferred_element_type=F32)


def _dot_tn(a, b):
    return lax.dot_general(a.astype(BF16), b.astype(BF16), (((0,), (0,)), ((), ())),
                           preferred_element_type=F32)


def _dot_split(x, w_bf16):
    hi = x.astype(BF16)
    lo = (x - hi.astype(F32)).astype(BF16)
    return (jnp.dot(hi, w_bf16, preferred_element_type=F32)
            + jnp.dot(lo, w_bf16, preferred_element_type=F32))


def _head_sum(x, ones_bd):
    parts = [_dot_split(x[:, p * PAIR:(p + 1) * PAIR], ones_bd) for p in range(N_PAIRS)]
    return jnp.concatenate(parts, axis=1)


def _sigmoid(x):
    return 1.0 / (1.0 + jnp.exp(-x))


def _prep_kernel(x_ref, win_ref, mu_ref, w0_ref, a0_ref, lora_ref, gup_ref, kk_ref, ka_ref,
                 poolw_ref, pools_ref, ones_ref,
                 pool_o, r_o, w_o, k_o, v_o, a_o, b_o, g_o,
                 proj_ref, halo_ref):
    i = pl.program_id(1)
    ts = x_ref.shape[1]

    @pl.when(i == 0)
    def _():
        halo_ref[...] = jnp.zeros_like(halo_ref)

    proj_ref[...] = jnp.dot(x_ref[0].astype(BF16), win_ref[...], preferred_element_type=F32)

    row = lax.broadcasted_iota(jnp.int32, (ts, 1), 0)

    def shifted(off, width):
        z = proj_ref[:, off:off + width]
        prev = jnp.where(row == 0, halo_ref[POOL_HALO - 1:POOL_HALO, off:off + width],
                         pltpu.roll(z, 1, 0))
        return z + (prev - z) * mu_ref[:, off - D_POOL:off - D_POOL + width]

    p = proj_ref[:, 0:D_POOL]
    ext = jnp.concatenate([halo_ref[:, 0:D_POOL], p], axis=0)
    s2 = ext + pltpu.roll(ext, 1, 0)
    s4 = s2 + pltpu.roll(s2, 2, 0)
    s8 = s4 + pltpu.roll(s4, 4, 0)
    s16 = s8 + pltpu.roll(s8, 8, 0)
    lane = lax.broadcasted_iota(jnp.int32, (ts, D_POOL), 1)
    grp = lane // POOL_GROUP
    wsum = jnp.where(grp == 0, s2[POOL_HALO:], jnp.where(grp == 1, s4[POOL_HALO:],
                     jnp.where(grp == 2, s8[POOL_HALO:], s16[POOL_HALO:])))
    win = jnp.where(grp == 0, 2.0, jnp.where(grp == 1, 4.0, jnp.where(grp == 2, 8.0, 16.0)))
    pos = (i * ts + row + 1).astype(F32)
    diff = wsum / jnp.minimum(pos, win) - p
    pool_o[0] = _dot(diff, poolw_ref[...]) * pools_ref[...]

    o = D_POOL
    r = shifted(o, D_RWKV)
    k = shifted(o + D_RWKV, D_RWKV)
    v = shifted(o + 2 * D_RWKV, D_RWKV)
    lw = shifted(o + 3 * D_RWKV, D_DECAY_LORA + D_AAA_LORA)
    gd = shifted(o + 3 * D_RWKV + D_DECAY_LORA + D_AAA_LORA, D_GATE_LORA)

    lane128 = lax.broadcasted_iota(jnp.int32, (ts, LANES), 1)
    lora_in = jnp.where(lane128 < D_DECAY_LORA, jnp.tanh(lw), lw)
    lora = _dot(lora_in, lora_ref[...])
    wpre = -(w0_ref[...] + lora[:, 0:D_RWKV])
    softplus = jnp.maximum(wpre, 0.0) + jnp.log(1.0 + jnp.exp(-jnp.abs(wpre)))
    w_log = -softplus - 0.5
    w_o[0] = -jnp.exp(w_log)
    eta = _sigmoid(a0_ref[...] + lora[:, D_RWKV:2 * D_RWKV])
    g_o[0] = _dot(_sigmoid(gd), gup_ref[...])
    kk = k * kk_ref[...]
    ss = _head_sum(kk * kk, ones_ref[...])
    kkn = kk / jnp.maximum(jnp.sqrt(ss), 1e-12)
    r_o[0] = r
    v_o[0] = v
    k_o[0] = k * (1.0 + (eta - 1.0) * ka_ref[...])
    a_o[0] = -kkn
    b_o[0] = kkn * eta

    halo_ref[...] = proj_ref[ts - POOL_HALO:ts, :]


def _prep_call(x, win_bf, mu, w0, a0, lora_w, gup_bf, k_k, k_a, poolw_bd, pool_scale, ones_bd):
    b, s, _ = x.shape
    grid = (b, s // TS)
    full = lambda arr: pl.BlockSpec(arr.shape, lambda bi, i: (0,) * arr.ndim)
    tok = lambda width: pl.BlockSpec((1, TS, width), lambda bi, i: (bi, i, 0))
    outs = [jax.ShapeDtypeStruct((b, s, D_POOL), F32)] + [jax.ShapeDtypeStruct((b, s, D_RWKV), F32)] * 7
    params = (win_bf, mu, w0, a0, lora_w, gup_bf, k_k, k_a, poolw_bd, pool_scale, ones_bd)
    return pl.pallas_call(
        _prep_kernel,
        grid=grid,
        in_specs=[tok(D_MODEL)] + [full(a) for a in params],
        out_specs=[tok(D_POOL)] + [tok(D_RWKV)] * 7,
        out_shape=outs,
        scratch_shapes=[pltpu.VMEM((TS, D_IN), F32), pltpu.VMEM((POOL_HALO, D_IN), F32)],
        compiler_params=pltpu.CompilerParams(
            dimension_semantics=("parallel", "arbitrary"), vmem_limit_bytes=VMEM_LIMIT),
        name="prep",
    )(x, *params)


def _rwkv_kernel(r_ref, w_ref, k_ref, v_ref, a_ref, b_ref, o_ref, h_ref):
    c = pl.program_id(1)

    @pl.when(c == 0)
    def _():
        h_ref[...] = jnp.zeros_like(h_ref)

    L = CHUNK
    row = lax.broadcasted_iota(jnp.int32, (L, PAIR), 0)
    lane = lax.broadcasted_iota(jnp.int32, (L, PAIR), 1)
    head0 = lane < HEAD
    ri = lax.broadcasted_iota(jnp.int32, (2 * L, 2 * L), 0)
    ci = lax.broadcasted_iota(jnp.int32, (2 * L, 2 * L), 1)
    strict_bd = (ri & (L - 1)) > (ci & (L - 1))
    incl_wide = row >= (lane & (L - 1))

    def expand(x):
        return jnp.concatenate([jnp.where(head0, x, 0.0), jnp.where(head0, 0.0, x)], axis=0)

    for p in range(N_PAIRS):
        sl = slice(p * PAIR, (p + 1) * PAIR)
        w = w_ref[0, :, sl]
        cum = w
        for sh in (1, 2, 4, 8, 16, 32):
            cum = cum + jnp.where(row >= sh, pltpu.roll(cum, sh, 0), 0.0)
        tot = cum[L - 1:L, :]
        e_neg = jnp.exp(-cum)
        e_rem = jnp.exp(tot - cum)
        r_t = r_ref[0, :, sl] * jnp.exp(cum)
        a_e = expand(a_ref[0, :, sl] * jnp.exp(cum - w))
        b = b_ref[0, :, sl]
        k = k_ref[0, :, sl]
        v_e = expand(v_ref[0, :, sl]).astype(BF16)

        sc = _dot_nt(jnp.concatenate([a_e, r_t], axis=0),
                     jnp.concatenate([expand(b * e_neg), expand(k * e_neg)], axis=0))
        t_ab = jnp.where(strict_bd, sc[0:2 * L, 0:2 * L], 0.0).astype(BF16)
        t_ak = jnp.where(strict_bd, sc[0:2 * L, 2 * L:4 * L], 0.0)
        t_rb = jnp.where(incl_wide, sc[2 * L:3 * L, 0:2 * L], 0.0)
        t_rk = jnp.where(incl_wide, sc[2 * L:3 * L, 2 * L:4 * L], 0.0)

        pows = [t_ab]
        for _ in range(5):
            pows.append(_dot(pows[-1], pows[-1]).astype(BF16))
        x = jnp.concatenate([a_e, _dot(t_ak, v_e)], axis=1)
        for tk in reversed(pows):
            x = x + _dot(tk, x)
        p_e = x[:, 0:PAIR]
        q_e = x[:, PAIR:2 * PAIR]

        h = h_ref[p]
        hb = h.astype(BF16)
        u_e = (_dot(p_e, hb) + q_e).astype(BF16)
        o_ref[0, :, sl] = _dot(jnp.concatenate([r_t, t_rb, t_rk], axis=1),
                               jnp.concatenate([hb, u_e, v_e], axis=0))
        h_add = _dot_tn(jnp.concatenate([expand(b * e_rem), expand(k * e_rem)], axis=0),
                        jnp.concatenate([u_e, v_e], axis=0))
        w_col = jnp.transpose(jnp.broadcast_to(jnp.exp(tot), (PAIR, PAIR)))
        h_ref[p] = h * w_col + h_add


def _rwkv_call(r, w, k, v, a, b):
    bsz, s, _ = r.shape
    spec = pl.BlockSpec((1, CHUNK, D_RWKV), lambda bi, c: (bi, c, 0))
    return pl.pallas_call(
        _rwkv_kernel,
        grid=(bsz, s // CHUNK),
        in_specs=[spec] * 6,
        out_specs=spec,
        out_shape=jax.ShapeDtypeStruct((bsz, s, D_RWKV), F32),
        scratch_shapes=[pltpu.VMEM((N_PAIRS, PAIR, PAIR), F32)],
        compiler_params=pltpu.CompilerParams(
            dimension_semantics=("parallel", "arbitrary"), vmem_limit_bytes=VMEM_LIMIT),
        name="rwkv",
    )(r, w, k, v, a, b)


def _layer_norm(x, g, b):
    mu = jnp.mean(x, axis=-1, keepdims=True)
    xc = x - mu
    var = jnp.mean(xc * xc, axis=-1, keepdims=True)
    return xc * lax.rsqrt(var + LN_EPS) * g + b


def _post_kernel(x_ref, pool_ref, o_ref, r_ref, k_ref, v_ref, g_ref,
                 lnxg_ref, lnxb_ref, rk_ref, ones_ref, wop_ref, wor_ref, ln1g_ref, ln1b_ref,
                 rw_hi_ref, rw_lo_ref, rb_ref, tri_ref,
                 h_o, meta_o, cnt_o, base_ref):
    i = pl.program_id(0)
    ts = x_ref.shape[0]

    @pl.when(i == 0)
    def _():
        base_ref[...] = jnp.zeros_like(base_ref)

    ones = ones_ref[...]
    o = o_ref[...]
    mu = _head_sum(o, ones) * (1.0 / HEAD)
    oc = o - mu
    var = _head_sum(oc * oc, ones) * (1.0 / HEAD)
    y = oc * lax.rsqrt(var + LNX_EPS) * lnxg_ref[...] + lnxb_ref[...]
    v = v_ref[...]
    bonus = _head_sum(r_ref[...] * k_ref[...] * rk_ref[...], ones) * v
    y = (y + bonus) * g_ref[...]
    mixed = _dot(pool_ref[...], wop_ref[...]) + _dot(y, wor_ref[...])
    h = _layer_norm(ALPHA * x_ref[...] + mixed, ln1g_ref[...], ln1b_ref[...])
    h_o[...] = h

    h_hi = h.astype(BF16)
    h_lo = (h - h_hi.astype(F32)).astype(BF16)
    logits = (jnp.dot(h_hi, rw_hi_ref[...], preferred_element_type=F32)
              + jnp.dot(h_lo, rw_hi_ref[...], preferred_element_type=F32)
              + jnp.dot(h_hi, rw_lo_ref[...], preferred_element_type=F32)) + rb_ref[...]
    lane = lax.broadcasted_iota(jnp.int32, (ts, LANES), 1)
    neg = -jnp.inf
    gl = jnp.where(lane < N_GROUPS, logits, neg)
    gmax = jnp.max(gl, axis=-1, keepdims=True)
    g_idx = jnp.min(jnp.where(gl == gmax, lane, LANES), axis=-1, keepdims=True)
    g_top_p = 1.0 / jnp.sum(jnp.exp(gl - gmax), axis=-1, keepdims=True)
    lo_lane = EXPERT_LANE0 + g_idx * EXPERTS_PER_GROUP
    el = jnp.where((lane >= lo_lane) & (lane < lo_lane + EXPERTS_PER_GROUP), logits, neg)
    m1 = jnp.max(el, axis=-1, keepdims=True)
    i1 = jnp.min(jnp.where(el == m1, lane, LANES), axis=-1, keepdims=True)
    el2 = jnp.where(lane == i1, neg, el)
    m2 = jnp.max(el2, axis=-1, keepdims=True)
    i2 = jnp.min(jnp.where(el2 == m2, lane, LANES), axis=-1, keepdims=True)
    e21 = jnp.exp(m2 - m1)
    wgt1 = g_top_p / (1.0 + e21)
    wgt2 = g_top_p * e21 / (1.0 + e21)

    sel1 = lane == i1
    sel2 = lane == i2
    onehot = jnp.where(sel1 | sel2, 1.0, 0.0)
    before = jnp.dot(tri_ref[...], onehot.astype(BF16), preferred_element_type=F32)
    posn = base_ref[0:1, :] + before
    rank1 = jnp.sum(jnp.where(sel1, posn, 0.0), axis=-1, keepdims=True)
    rank2 = jnp.sum(jnp.where(sel2, posn, 0.0), axis=-1, keepdims=True)
    new_base = base_ref[0:1, :] + jnp.sum(onehot, axis=0, keepdims=True)
    base_ref[...] = jnp.broadcast_to(new_base, base_ref.shape)
    cnt_o[...] = jnp.broadcast_to(new_base, cnt_o.shape)

    e1 = (i1 - EXPERT_LANE0).astype(F32)
    e2 = (i2 - EXPERT_LANE0).astype(F32)
    meta = jnp.where(lane == 0, e1, jnp.where(lane == 1, e2, jnp.where(lane == 2, wgt1,
           jnp.where(lane == 3, wgt2, jnp.where(lane == 4, rank1, jnp.where(lane == 5, rank2, 0.0))))))
    meta_o[...] = meta


def _post_call(x2, pool2, o2, r2, k2, v2, g2, lnx_g, lnx_b, rk, ones_bd, wo_pool, wo_rwkv,
               ln1_g, ln1_b, rw_hi, rw_lo, rb, tri):
    n = x2.shape[0]
    full = lambda arr: pl.BlockSpec(arr.shape, lambda i: (0,) * arr.ndim)
    tok = lambda width: pl.BlockSpec((TS, width), lambda i: (i, 0))
    params = (lnx_g, lnx_b, rk, ones_bd, wo_pool, wo_rwkv, ln1_g, ln1_b, rw_hi, rw_lo, rb, tri)
    return pl.pallas_call(
        _post_kernel,
        grid=(n // TS,),
        in_specs=[tok(D_MODEL), tok(D_POOL)] + [tok(D_RWKV)] * 5 + [full(a) for a in params],
        out_specs=[tok(D_MODEL), tok(LANES), pl.BlockSpec((8, LANES), lambda i: (0, 0))],
        out_shape=[jax.ShapeDtypeStruct((n, D_MODEL), F32), jax.ShapeDtypeStruct((n, LANES), F32),
                   jax.ShapeDtypeStruct((8, LANES), F32)],
        scratch_shapes=[pltpu.VMEM((8, LANES), F32)],
        compiler_params=pltpu.CompilerParams(
            dimension_semantics=("arbitrary",), vmem_limit_bytes=VMEM_LIMIT),
        name="post",
    )(x2, pool2, o2, r2, k2, v2, g2, *params)


def _dispatch_kernel(pad_start_ref, pad_cnt_ref, n_used_ref, dest_ref, h_ref, xs_ref, zero_ref,
                     sem, zsem):
    i = pl.program_id(0)
    ts = h_ref.shape[0]
    n_blocks = xs_ref.shape[0] // BM

    def row_copy(t, slot):
        return pltpu.make_async_copy(h_ref.at[pl.ds(t, 1)],
                                     xs_ref.at[pl.ds(dest_ref[0, 0, 2 * t + slot], 1)], sem)

    def zero_row_copy(row):
        return pltpu.make_async_copy(zero_ref.at[pl.ds(0, 1)], xs_ref.at[pl.ds(row, 1)], sem)

    def zero_block_copy(blk):
        return pltpu.make_async_copy(zero_ref, xs_ref.at[pl.ds(pl.multiple_of(blk * BM, BM), BM)], zsem)

    @pl.when(i == 0)
    def _():
        zero_ref[...] = jnp.zeros_like(zero_ref)

        def fill(e, carry):
            start = pad_start_ref[e]

            def one(j, c):
                zero_row_copy(start + j).start()
                return c

            return lax.fori_loop(0, pad_cnt_ref[e], one, carry)

        def fill_wait(e, carry):
            def one(j, c):
                zero_row_copy(0).wait()
                return c

            return lax.fori_loop(0, pad_cnt_ref[e], one, carry)

        def tail(blk, carry):
            zero_block_copy(blk).start()
            return carry

        def tail_wait(blk, carry):
            zero_block_copy(blk).wait()
            return carry

        lax.fori_loop(0, N_EXPERTS, fill, 0)
        lax.fori_loop(n_used_ref[0], n_blocks, tail, 0)
        lax.fori_loop(0, N_EXPERTS, fill_wait, 0)
        lax.fori_loop(n_used_ref[0], n_blocks, tail_wait, 0)

    def issue(t, carry):
        row_copy(t, 0).start()
        row_copy(t, 1).start()
        return carry

    def drain(t, carry):
        row_copy(t, 0).wait()
        row_copy(t, 1).wait()
        return carry

    lax.fori_loop(0, ts, issue, 0)
    lax.fori_loop(0, ts, drain, 0)


def _dispatch_call(pad_start, pad_cnt, n_used, dest, h, m_pad):
    n = h.shape[0]
    grid_spec = pltpu.PrefetchScalarGridSpec(
        num_scalar_prefetch=3,
        grid=(n // TS,),
        in_specs=[pl.BlockSpec((1, 1, 2 * TS), lambda i, *_: (i, 0, 0), memory_space=pltpu.SMEM),
                  pl.BlockSpec((TS, D_MODEL), lambda i, *_: (i, 0))],
        out_specs=pl.BlockSpec(memory_space=pl.ANY),
        scratch_shapes=[pltpu.VMEM((BM, D_MODEL), F32), pltpu.SemaphoreType.DMA(()),
                        pltpu.SemaphoreType.DMA(())],
    )
    return pl.pallas_call(
        _dispatch_kernel,
        grid_spec=grid_spec,
        out_shape=jax.ShapeDtypeStruct((m_pad, D_MODEL), F32),
        compiler_params=pltpu.CompilerParams(
            dimension_semantics=("arbitrary",), vmem_limit_bytes=VMEM_LIMIT),
        name="dispatch",
    )(pad_start, pad_cnt, n_used, dest, h)


def _expert_kernel(blk_e_ref, n_used_ref, xs_ref, wg_ref, wu_ref, wd_ref, ys_ref):
    i = pl.program_id(0)

    @pl.when(i < n_used_ref[0])
    def _():
        xb = xs_ref[...].astype(BF16)
        gate = jnp.dot(xb, wg_ref[0].astype(BF16), preferred_element_type=F32)
        up = jnp.dot(xb, wu_ref[0].astype(BF16), preferred_element_type=F32)
        hid = gate * _sigmoid(gate) * up
        ys_ref[...] = _dot(hid, wd_ref[0])

    @pl.when(i >= n_used_ref[0])
    def _():
        ys_ref[...] = jnp.zeros_like(ys_ref)


def _expert_call(blk_e, n_used, xs, exp_gate, exp_up, exp_down):
    m_pad = xs.shape[0]
    n_blocks = m_pad // BM
    row_blk = lambda i, be, nu: (jnp.minimum(i, nu[0] - 1), 0)
    out_blk = lambda i, be, nu: (i, 0)
    w_blk = lambda i, be, nu: (be[jnp.minimum(i, nu[0] - 1)], 0, 0)
    grid_spec = pltpu.PrefetchScalarGridSpec(
        num_scalar_prefetch=2,
        grid=(n_blocks,),
        in_specs=[pl.BlockSpec((BM, D_MODEL), row_blk),
                  pl.BlockSpec((1, D_MODEL, D_EXPERT), w_blk),
                  pl.BlockSpec((1, D_MODEL, D_EXPERT), w_blk),
                  pl.BlockSpec((1, D_EXPERT, D_MODEL), w_blk)],
        out_specs=pl.BlockSpec((BM, D_MODEL), out_blk),
    )
    return pl.pallas_call(
        _expert_kernel,
        grid_spec=grid_spec,
        out_shape=jax.ShapeDtypeStruct((m_pad, D_MODEL), F32),
        compiler_params=pltpu.CompilerParams(
            dimension_semantics=("arbitrary",), vmem_limit_bytes=VMEM_LIMIT),
        name="experts",
    )(blk_e, n_used, xs, exp_gate, exp_up, exp_down)


def _combine_kernel(dest_ref, h_ref, meta_ref, ys_ref, g_ref, b_ref, out_ref, y1_ref, y2_ref, sem):
    ts = h_ref.shape[0]

    def row_copy(t, slot, buf):
        return pltpu.make_async_copy(ys_ref.at[pl.ds(dest_ref[0, 0, 2 * t + slot], 1)],
                                     buf.at[pl.ds(t, 1)], sem)

    def issue(t, carry):
        row_copy(t, 0, y1_ref).start()
        row_copy(t, 1, y2_ref).start()
        return carry

    def drain(t, carry):
        row_copy(t, 0, y1_ref).wait()
        row_copy(t, 1, y2_ref).wait()
        return carry

    lax.fori_loop(0, ts, issue, 0)
    lax.fori_loop(0, ts, drain, 0)
    meta = meta_ref[...]
    ffn = y1_ref[...] * meta[:, 2:3] + y2_ref[...] * meta[:, 3:4]
    out_ref[...] = _layer_norm(ALPHA * h_ref[...] + ffn, g_ref[...], b_ref[...])


def _combine_call(dest, h, meta, ys, ln2_g, ln2_b):
    n = h.shape[0]
    full = lambda arr: pl.BlockSpec(arr.shape, lambda i: (0,) * arr.ndim)
    return pl.pallas_call(
        _combine_kernel,
        grid=(n // TS,),
        in_specs=[pl.BlockSpec((1, 1, 2 * TS), lambda i: (i, 0, 0), memory_space=pltpu.SMEM),
                  pl.BlockSpec((TS, D_MODEL), lambda i: (i, 0)),
                  pl.BlockSpec((TS, LANES), lambda i: (i, 0)),
                  pl.BlockSpec(memory_space=pl.ANY),
                  full(ln2_g), full(ln2_b)],
        out_specs=pl.BlockSpec((TS, D_MODEL), lambda i: (i, 0)),
        out_shape=jax.ShapeDtypeStruct((n, D_MODEL), F32),
        scratch_shapes=[pltpu.VMEM((TS, D_MODEL), F32), pltpu.VMEM((TS, D_MODEL), F32),
                        pltpu.SemaphoreType.DMA(())],
        compiler_params=pltpu.CompilerParams(
            dimension_semantics=("arbitrary",), vmem_limit_bytes=VMEM_LIMIT),
        name="combine",
    )(dest, h, meta, ys, ln2_g, ln2_b)


def _block_diag(blocks):
    g, c, d = blocks.shape
    eye = jnp.eye(g, dtype=blocks.dtype)
    return (eye[:, None, :, None] * blocks[:, :, None, :]).reshape(g * c, g * d)


def _layer(x, w_in, pool_w, pool_scale, mu_shift, w0, w_up, a0, a_up, g_up, k_k, k_a, r_k,
           lnx_g, lnx_b, w_out, ln1_g, ln1_b, router_group, router_group_b, router_expert,
           router_expert_b, exp_gate, exp_up, exp_down, ln2_g, ln2_b):
    b, s, d = x.shape
    n = b * s
    row2 = lambda t: t.reshape(1, -1)

    ones_bd = _block_diag(jnp.ones((2, HEAD, HEAD), BF16))
    lora_w = _block_diag(jnp.stack([w_up, a_up])).astype(BF16)
    poolw_bd = _block_diag(pool_w).astype(BF16)
    rw = jnp.zeros((D_MODEL, LANES), F32)
    rw = rw.at[:, 0:N_GROUPS].set(router_group).at[:, EXPERT_LANE0:EXPERT_LANE0 + N_EXPERTS].set(router_expert)
    rw_hi = rw.astype(BF16)
    rw_lo = (rw - rw_hi.astype(F32)).astype(BF16)
    rb = jnp.zeros((1, LANES), F32)
    rb = rb.at[0, 0:N_GROUPS].set(router_group_b).at[0, EXPERT_LANE0:EXPERT_LANE0 + N_EXPERTS].set(router_expert_b)
    tri = (lax.broadcasted_iota(jnp.int32, (TS, TS), 0) > lax.broadcasted_iota(jnp.int32, (TS, TS), 1)).astype(BF16)

    pool_out, r, w, k, v, a_vec, b_vec, g = _prep_call(
        x, w_in.astype(BF16), row2(mu_shift), row2(w0), row2(a0), lora_w, g_up.astype(BF16),
        row2(k_k), row2(k_a), poolw_bd, row2(pool_scale), ones_bd)
    o = _rwkv_call(r, w, k, v, a_vec, b_vec)

    flat = lambda t: t.reshape(n, t.shape[-1])
    w_out_bf = w_out.astype(BF16)
    h, meta, counts = _post_call(
        flat(x), flat(pool_out), flat(o), flat(r), flat(k), flat(v), flat(g),
        row2(lnx_g), row2(lnx_b), row2(r_k), ones_bd, w_out_bf[0:D_POOL], w_out_bf[D_POOL:],
        row2(ln1_g), row2(ln1_b), rw_hi, rw_lo, rb, tri)

    cnt = counts[0, EXPERT_LANE0:EXPERT_LANE0 + N_EXPERTS].astype(jnp.int32)
    padded = (cnt + BM - 1) // BM * BM
    pend = jnp.cumsum(padded)
    pstart = pend - padded
    e_ids = meta[:, 0:2].astype(jnp.int32)
    dest = (pstart[e_ids] + meta[:, 4:6].astype(jnp.int32)).reshape(n // TS, 1, 2 * TS)
    m_pad = 2 * n + N_EXPERTS * BM
    n_blocks = m_pad // BM
    blk_start = jnp.arange(n_blocks, dtype=jnp.int32) * BM
    blk_e = jnp.minimum(jnp.searchsorted(pend, blk_start, side='right'), N_EXPERTS - 1).astype(jnp.int32)
    n_used = (pend[-1:] // BM).astype(jnp.int32)

    xs = _dispatch_call(pstart + cnt, padded - cnt, n_used, dest, h, m_pad)
    ys = _expert_call(blk_e, n_used, xs, exp_gate, exp_up, exp_down)
    out = _combine_call(dest, h, meta, ys, row2(ln2_g), row2(ln2_b))
    return out.reshape(b, s, d)


def kernel(x, w_in, pool_w, pool_scale, mu_shift, w0, w_up, a0, a_up, g_up, k_k, k_a, r_k, lnx_g, lnx_b, w_out, ln1_g, ln1_b, router_group, router_group_b, router_expert, router_expert_b, exp_gate, exp_up, exp_down, ln2_g, ln2_b):
    depth = w_in.shape[0]
    for l in range(depth):
        x = _layer(x, w_in[l], pool_w[l], pool_scale[l], mu_shift[l], w0[l], w_up[l], a0[l],
                   a_up[l], g_up[l], k_k[l], k_a[l], r_k[l], lnx_g[l], lnx_b[l], w_out[l],
                   ln1_g[l], ln1_b[l], router_group[l], router_group_b[l], router_expert[l],
                   router_expert_b[l], exp_gate[l], exp_up[l], exp_down[l], ln2_g[l], ln2_b[l])
    return x
```

```python
import functools

import jax
import jax.numpy as jnp
from jax import lax
from jax.experimental import pallas as pl
from jax.experimental.pallas import tpu as pltpu

F32 = jnp.float32
BF16 = jnp.bfloat16

D_MODEL = 1024
D_POOL = 256
POOL_WINDOWS = (2, 4, 8, 16)
POOL_GROUP = 64
POOL_HALO = 16
D_RWKV = 768
HEAD = 64
D_DECAY_LORA = 64
D_AAA_LORA = 64
D_GATE_LORA = 128
D_RWKV_IN = 3 * D_RWKV + D_DECAY_LORA + D_AAA_LORA + D_GATE_LORA
D_IN = D_POOL + D_RWKV_IN
N_GROUPS = 4
EXPERTS_PER_GROUP = 8
N_EXPERTS = 32
D_EXPERT = 256
LN_EPS = 1e-5
LNX_EPS = 64e-5
ALPHA = 2.0 ** 0.25

LANES = 128
VMEM_LIMIT = 56 * 1024 * 1024

PAIR = 2 * HEAD
N_PAIRS = D_RWKV // PAIR
CHUNK = 64
RWKV_TILE = 2 * CHUNK
TS = 256
BM = 256
EXPERT_LANE0 = 32


def _dot(a, b):
    return jnp.dot(a.astype(BF16), b.astype(BF16), preferred_element_type=F32)


def _dot_nt(a, b):
    return lax.dot_general(a.astype(BF16), b.astype(BF16), (((1,), (1,)), ((), ())),
                           preferred_element_type=F32)


def _dot_tn(a, b):
    return lax.dot_general(a.astype(BF16), b.astype(BF16), (((0,), (0,)), ((), ())),
                           preferred_element_type=F32)


def _dot_split(x, w_bf16):
    hi = x.astype(BF16)
    lo = (x - hi.astype(F32)).astype(BF16)
    return (jnp.dot(hi, w_bf16, preferred_element_type=F32)
            + jnp.dot(lo, w_bf16, preferred_element_type=F32))


def _head_sum(x, ones_bd):
    parts = [_dot_split(x[:, p * PAIR:(p + 1) * PAIR], ones_bd) for p in range(N_PAIRS)]
    return jnp.concatenate(parts, axis=1)


def _sigmoid(x):
    return 1.0 / (1.0 + jnp.exp(-x))


def _prep_kernel(x_ref, win_ref, mu_ref, w0_ref, a0_ref, lora_ref, gup_ref, kk_ref, ka_ref,
                 poolw_ref, pools_ref, ones_ref,
                 pool_o, r_o, w_o, k_o, v_o, a_o, b_o, g_o,
                 proj_ref, halo_ref):
    i = pl.program_id(1)
    ts = x_ref.shape[1]

    @pl.when(i == 0)
    def _():
        halo_ref[...] = jnp.zeros_like(halo_ref)

    proj_ref[...] = jnp.dot(x_ref[0].astype(BF16), win_ref[...], preferred_element_type=F32)

    row = lax.broadcasted_iota(jnp.int32, (ts, 1), 0)

    def shifted(off, width):
        z = proj_ref[:, off:off + width]
        prev = jnp.where(row == 0, halo_ref[POOL_HALO - 1:POOL_HALO, off:off + width],
                         pltpu.roll(z, 1, 0))
        return z + (prev - z) * mu_ref[:, off - D_POOL:off - D_POOL + width]

    p = proj_ref[:, 0:D_POOL]
    ext = jnp.concatenate([halo_ref[:, 0:D_POOL], p], axis=0)
    s2 = ext + pltpu.roll(ext, 1, 0)
    s4 = s2 + pltpu.roll(s2, 2, 0)
    s8 = s4 + pltpu.roll(s4, 4, 0)
    s16 = s8 + pltpu.roll(s8, 8, 0)
    lane = lax.broadcasted_iota(jnp.int32, (ts, D_POOL), 1)
    grp = lane // POOL_GROUP
    wsum = jnp.where(grp == 0, s2[POOL_HALO:], jnp.where(grp == 1, s4[POOL_HALO:],
                     jnp.where(grp == 2, s8[POOL_HALO:], s16[POOL_HALO:])))
    win = jnp.where(grp == 0, 2.0, jnp.where(grp == 1, 4.0, jnp.where(grp == 2, 8.0, 16.0)))
    pos = (i * ts + row + 1).astype(F32)
    diff = wsum / jnp.minimum(pos, win) - p
    pool_o[0] = _dot(diff, poolw_ref[...]) * pools_ref[...]

    o = D_POOL
    r = shifted(o, D_RWKV)
    k = shifted(o + D_RWKV, D_RWKV)
    v = shifted(o + 2 * D_RWKV, D_RWKV)
    lw = shifted(o + 3 * D_RWKV, D_DECAY_LORA + D_AAA_LORA)
    gd = shifted(o + 3 * D_RWKV + D_DECAY_LORA + D_AAA_LORA, D_GATE_LORA)

    lane128 = lax.broadcasted_iota(jnp.int32, (ts, LANES), 1)
    lora_in = jnp.where(lane128 < D_DECAY_LORA, jnp.tanh(lw), lw)
    lora = _dot(lora_in, lora_ref[...])
    wpre = -(w0_ref[...] + lora[:, 0:D_RWKV])
    softplus = jnp.maximum(wpre, 0.0) + jnp.log(1.0 + jnp.exp(-jnp.abs(wpre)))
    w_log = -softplus - 0.5
    w_o[0] = -jnp.exp(w_log)
    eta = _sigmoid(a0_ref[...] + lora[:, D_RWKV:2 * D_RWKV])
    g_o[0] = _dot(_sigmoid(gd), gup_ref[...])
    kk = k * kk_ref[...]
    ss = _head_sum(kk * kk, ones_ref[...])
    kkn = kk / jnp.maximum(jnp.sqrt(ss), 1e-12)
    r_o[0] = r
    v_o[0] = v
    k_o[0] = k * (1.0 + (eta - 1.0) * ka_ref[...])
    a_o[0] = -kkn
    b_o[0] = kkn * eta

    halo_ref[...] = proj_ref[ts - POOL_HALO:ts, :]


def _prep_call(x, win_bf, mu, w0, a0, lora_w, gup_bf, k_k, k_a, poolw_bd, pool_scale, ones_bd):
    b, s, _ = x.shape
    grid = (b, s // TS)
    full = lambda arr: pl.BlockSpec(arr.shape, lambda bi, i: (0,) * arr.ndim)
    tok = lambda width: pl.BlockSpec((1, TS, width), lambda bi, i: (bi, i, 0))
    outs = [jax.ShapeDtypeStruct((b, s, D_POOL), F32)] + [jax.ShapeDtypeStruct((b, s, D_RWKV), F32)] * 7
    params = (win_bf, mu, w0, a0, lora_w, gup_bf, k_k, k_a, poolw_bd, pool_scale, ones_bd)
    return pl.pallas_call(
        _prep_kernel,
        grid=grid,
        in_specs=[tok(D_MODEL)] + [full(a) for a in params],
        out_specs=[tok(D_POOL)] + [tok(D_RWKV)] * 7,
        out_shape=outs,
        scratch_shapes=[pltpu.VMEM((TS, D_IN), F32), pltpu.VMEM((POOL_HALO, D_IN), F32)],
        compiler_params=pltpu.CompilerParams(
            dimension_semantics=("parallel", "arbitrary"), vmem_limit_bytes=VMEM_LIMIT),
        name="prep",
    )(x, *params)


def _rwkv_kernel(r_ref, w_ref, k_ref, v_ref, a_ref, b_ref, o_ref, h_ref):
    c = pl.program_id(1)

    @pl.when(c == 0)
    def _():
        h_ref[...] = jnp.zeros_like(h_ref)

    L = CHUNK
    n_chunks = r_ref.shape[1] // L
    row = lax.broadcasted_iota(jnp.int32, (L, PAIR), 0)
    lane = lax.broadcasted_iota(jnp.int32, (L, PAIR), 1)
    head0 = lane < HEAD
    strict = row > (lane & (L - 1))
    incl = row >= (lane & (L - 1))
    eye_wide = jnp.where(row == (lane & (L - 1)), 1.0, 0.0)

    def expand(x):
        xb = x.astype(BF16)
        zero = jnp.zeros_like(xb)
        return jnp.concatenate([jnp.where(head0, xb, zero), jnp.where(head0, zero, xb)], axis=0)

    def tile(ref, j, p):
        return ref[0, j * L:(j + 1) * L, p * PAIR:(p + 1) * PAIR]

    streams = [(j, p) for j in range(n_chunks) for p in range(N_PAIRS)]
    st = []
    for j, p in streams:
        w = tile(w_ref, j, p)
        cum = w
        for sh in (1, 2, 4, 8, 16, 32):
            cum = cum + jnp.where(row >= sh, pltpu.roll(cum, sh, 0), 0.0)
        tot = cum[L - 1:L, :]
        e_neg = jnp.exp(-cum)
        e_rem = jnp.exp(tot - cum)
        b = tile(b_ref, j, p)
        k = tile(k_ref, j, p)
        a_n = tile(a_ref, j, p) * jnp.exp(cum - w)
        st.append(dict(
            ar=jnp.concatenate([a_n, tile(r_ref, j, p) * jnp.exp(cum)], axis=0).astype(BF16),
            a_e=expand(a_n),
            v_e=expand(tile(v_ref, j, p)),
            bk_t=jnp.concatenate([expand(b * e_neg), expand(k * e_neg)], axis=0),
            bk_h=jnp.concatenate([expand(b * e_rem), expand(k * e_rem)], axis=0),
            w_tot=jnp.exp(tot)))
    for d in st:
        sc = _dot_nt(d['ar'], d['bk_t'])
        d['t'] = jnp.where(strict, sc[0:L, 0:PAIR], 0.0)
        d['t_ak'] = jnp.where(strict, sc[0:L, PAIR:2 * PAIR], 0.0)
        d['r_all'] = jnp.concatenate([d['ar'][L:2 * L],
                                      jnp.where(incl, sc[L:2 * L, 0:PAIR], 0.0).astype(BF16),
                                      jnp.where(incl, sc[L:2 * L, PAIR:2 * PAIR], 0.0).astype(BF16)],
                                     axis=1)
    for d in st:
        d['takv'] = _dot(d['t_ak'], d['v_e'])
        d['m'] = eye_wide + d['t']
        d['t'] = _dot(d['t'], expand(d['t']))
    for step in range(5):
        for d in st:
            if step < 4:
                both = _dot(d['t'], jnp.concatenate([expand(d['t']), expand(d['m'])], axis=1))
                d['t'] = both[:, 0:PAIR]
                d['m'] = d['m'] + both[:, PAIR:2 * PAIR]
            else:
                d['m'] = d['m'] + _dot(d['t'], expand(d['m']))
    for d in st:
        d['pq'] = _dot(d['m'], jnp.concatenate([d['a_e'], expand(d['takv'])], axis=1))

    h = [h_ref[p] for p in range(N_PAIRS)]
    for j in range(n_chunks):
        ds = st[j * N_PAIRS:(j + 1) * N_PAIRS]
        hb = [hp.astype(BF16) for hp in h]
        u_e = [expand(_dot(d['pq'][:, 0:PAIR], hb[p]) + d['pq'][:, PAIR:2 * PAIR])
               for p, d in enumerate(ds)]
        for p, d in enumerate(ds):
            o_ref[0, j * L:(j + 1) * L, p * PAIR:(p + 1) * PAIR] = _dot(
                d['r_all'], jnp.concatenate([hb[p], u_e[p], d['v_e']], axis=0))
        for p, d in enumerate(ds):
            h_add = _dot_tn(d['bk_h'], jnp.concatenate([u_e[p], d['v_e']], axis=0))
            w_col = jnp.transpose(jnp.broadcast_to(d['w_tot'], (PAIR, PAIR)))
            h[p] = h[p] * w_col + h_add
    for p in range(N_PAIRS):
        h_ref[p] = h[p]


def _rwkv_call(r, w, k, v, a, b):
    bsz, s, _ = r.shape
    spec = pl.BlockSpec((1, RWKV_TILE, D_RWKV), lambda bi, c: (bi, c, 0))
    return pl.pallas_call(
        _rwkv_kernel,
        grid=(bsz, s // RWKV_TILE),
        in_specs=[spec] * 6,
        out_specs=spec,
        out_shape=jax.ShapeDtypeStruct((bsz, s, D_RWKV), F32),
        scratch_shapes=[pltpu.VMEM((N_PAIRS, PAIR, PAIR), F32)],
        compiler_params=pltpu.CompilerParams(
            dimension_semantics=("parallel", "arbitrary"), vmem_limit_bytes=VMEM_LIMIT),
        name="rwkv",
    )(r, w, k, v, a, b)


def _layer_norm(x, g, b):
    mu = jnp.mean(x, axis=-1, keepdims=True)
    xc = x - mu
    var = jnp.mean(xc * xc, axis=-1, keepdims=True)
    return xc * lax.rsqrt(var + LN_EPS) * g + b


def _post_kernel(x_ref, pool_ref, o_ref, r_ref, k_ref, v_ref, g_ref,
                 lnxg_ref, lnxb_ref, rk_ref, ones_ref, wop_ref, wor_ref, ln1g_ref, ln1b_ref,
                 rw_hi_ref, rw_lo_ref, rb_ref, tri_ref,
                 h_o, meta_o, meta_t_o, cnt_o, base_ref):
    i = pl.program_id(0)
    ts = x_ref.shape[0]

    @pl.when(i == 0)
    def _():
        base_ref[...] = jnp.zeros_like(base_ref)

    ones = ones_ref[...]
    o = o_ref[...]
    mu = _head_sum(o, ones) * (1.0 / HEAD)
    oc = o - mu
    var = _head_sum(oc * oc, ones) * (1.0 / HEAD)
    y = oc * lax.rsqrt(var + LNX_EPS) * lnxg_ref[...] + lnxb_ref[...]
    v = v_ref[...]
    bonus = _head_sum(r_ref[...] * k_ref[...] * rk_ref[...], ones) * v
    y = (y + bonus) * g_ref[...]
    mixed = _dot(pool_ref[...], wop_ref[...]) + _dot(y, wor_ref[...])
    h = _layer_norm(ALPHA * x_ref[...] + mixed, ln1g_ref[...], ln1b_ref[...])
    h_o[...] = h

    h_hi = h.astype(BF16)
    h_lo = (h - h_hi.astype(F32)).astype(BF16)
    logits = (jnp.dot(h_hi, rw_hi_ref[...], preferred_element_type=F32)
              + jnp.dot(h_lo, rw_hi_ref[...], preferred_element_type=F32)
              + jnp.dot(h_hi, rw_lo_ref[...], preferred_element_type=F32)) + rb_ref[...]
    lane = lax.broadcasted_iota(jnp.int32, (ts, LANES), 1)
    neg = -jnp.inf
    gl = jnp.where(lane < N_GROUPS, logits, neg)
    gmax = jnp.max(gl, axis=-1, keepdims=True)
    g_idx = jnp.min(jnp.where(gl == gmax, lane, LANES), axis=-1, keepdims=True)
    g_top_p = 1.0 / jnp.sum(jnp.exp(gl - gmax), axis=-1, keepdims=True)
    lo_lane = EXPERT_LANE0 + g_idx * EXPERTS_PER_GROUP
    el = jnp.where((lane >= lo_lane) & (lane < lo_lane + EXPERTS_PER_GROUP), logits, neg)
    m1 = jnp.max(el, axis=-1, keepdims=True)
    i1 = jnp.min(jnp.where(el == m1, lane, LANES), axis=-1, keepdims=True)
    el2 = jnp.where(lane == i1, neg, el)
    m2 = jnp.max(el2, axis=-1, keepdims=True)
    i2 = jnp.min(jnp.where(el2 == m2, lane, LANES), axis=-1, keepdims=True)
    e21 = jnp.exp(m2 - m1)
    wgt1 = g_top_p / (1.0 + e21)
    wgt2 = g_top_p * e21 / (1.0 + e21)

    sel1 = lane == i1
    sel2 = lane == i2
    onehot = jnp.where(sel1 | sel2, 1.0, 0.0)
    before = jnp.dot(tri_ref[...], onehot.astype(BF16), preferred_element_type=F32)
    posn = base_ref[0:1, :] + before
    rank1 = jnp.sum(jnp.where(sel1, posn, 0.0), axis=-1, keepdims=True)
    rank2 = jnp.sum(jnp.where(sel2, posn, 0.0), axis=-1, keepdims=True)
    new_base = base_ref[0:1, :] + jnp.sum(onehot, axis=0, keepdims=True)
    base_ref[...] = jnp.broadcast_to(new_base, base_ref.shape)
    cnt_o[...] = jnp.broadcast_to(new_base, cnt_o.shape)

    e1 = (i1 - EXPERT_LANE0).astype(F32)
    e2 = (i2 - EXPERT_LANE0).astype(F32)
    meta = jnp.where(lane == 0, e1, jnp.where(lane == 1, e2, jnp.where(lane == 2, wgt1,
           jnp.where(lane == 3, wgt2, jnp.where(lane == 4, rank1, jnp.where(lane == 5, rank2, 0.0))))))
    meta_o[...] = meta
    meta_t_o[...] = jnp.transpose(meta)[0:8, :]


def _post_call(x2, pool2, o2, r2, k2, v2, g2, lnx_g, lnx_b, rk, ones_bd, wo_pool, wo_rwkv,
               ln1_g, ln1_b, rw_hi, rw_lo, rb, tri):
    n = x2.shape[0]
    full = lambda arr: pl.BlockSpec(arr.shape, lambda i: (0,) * arr.ndim)
    tok = lambda width: pl.BlockSpec((TS, width), lambda i: (i, 0))
    params = (lnx_g, lnx_b, rk, ones_bd, wo_pool, wo_rwkv, ln1_g, ln1_b, rw_hi, rw_lo, rb, tri)
    return pl.pallas_call(
        _post_kernel,
        grid=(n // TS,),
        in_specs=[tok(D_MODEL), tok(D_POOL)] + [tok(D_RWKV)] * 5 + [full(a) for a in params],
        out_specs=[tok(D_MODEL), tok(LANES), pl.BlockSpec((8, TS), lambda i: (0, i)),
                   pl.BlockSpec((8, LANES), lambda i: (0, 0))],
        out_shape=[jax.ShapeDtypeStruct((n, D_MODEL), F32), jax.ShapeDtypeStruct((n, LANES), F32),
                   jax.ShapeDtypeStruct((8, n), F32), jax.ShapeDtypeStruct((8, LANES), F32)],
        scratch_shapes=[pltpu.VMEM((8, LANES), F32)],
        compiler_params=pltpu.CompilerParams(
            dimension_semantics=("arbitrary",), vmem_limit_bytes=VMEM_LIMIT),
        name="post",
    )(x2, pool2, o2, r2, k2, v2, g2, *params)


def _dispatch_kernel(pad_start_ref, pad_cnt_ref, n_used_ref, dest_ref, h_ref, xs_ref, zero_ref,
                     sem, zsem):
    i = pl.program_id(0)
    ts = h_ref.shape[0]
    n_blocks = xs_ref.shape[0] // BM

    def row_copy(t, slot):
        return pltpu.make_async_copy(h_ref.at[pl.ds(t, 1)],
                                     xs_ref.at[pl.ds(dest_ref[0, 0, slot * ts + t], 1)], sem)

    def zero_row_copy(row):
        return pltpu.make_async_copy(zero_ref.at[pl.ds(0, 1)], xs_ref.at[pl.ds(row, 1)], sem)

    def zero_block_copy(blk):
        return pltpu.make_async_copy(zero_ref, xs_ref.at[pl.ds(pl.multiple_of(blk * BM, BM), BM)], zsem)

    @pl.when(i == 0)
    def _():
        zero_ref[...] = jnp.zeros_like(zero_ref)

        def fill(e, carry):
            start = pad_start_ref[e]

            def one(j, c):
                zero_row_copy(start + j).start()
                return c

            return lax.fori_loop(0, pad_cnt_ref[e], one, carry)

        def fill_wait(e, carry):
            def one(j, c):
                zero_row_copy(0).wait()
                return c

            return lax.fori_loop(0, pad_cnt_ref[e], one, carry)

        def tail(blk, carry):
            zero_block_copy(blk).start()
            return carry

        def tail_wait(blk, carry):
            zero_block_copy(blk).wait()
            return carry

        lax.fori_loop(0, N_EXPERTS, fill, 0)
        lax.fori_loop(n_used_ref[0], n_blocks, tail, 0)
        lax.fori_loop(0, N_EXPERTS, fill_wait, 0)
        lax.fori_loop(n_used_ref[0], n_blocks, tail_wait, 0)

    def issue(t, carry):
        row_copy(t, 0).start()
        row_copy(t, 1).start()
        return carry

    def drain(t, carry):
        row_copy(t, 0).wait()
        row_copy(t, 1).wait()
        return carry

    lax.fori_loop(0, ts, issue, 0)
    lax.fori_loop(0, ts, drain, 0)


def _dispatch_call(pad_start, pad_cnt, n_used, dest, h, m_pad):
    n = h.shape[0]
    grid_spec = pltpu.PrefetchScalarGridSpec(
        num_scalar_prefetch=3,
        grid=(n // TS,),
        in_specs=[pl.BlockSpec((1, 1, 2 * TS), lambda i, *_: (i, 0, 0), memory_space=pltpu.SMEM),
                  pl.BlockSpec((TS, D_MODEL), lambda i, *_: (i, 0))],
        out_specs=pl.BlockSpec(memory_space=pl.ANY),
        scratch_shapes=[pltpu.VMEM((BM, D_MODEL), F32), pltpu.SemaphoreType.DMA(()),
                        pltpu.SemaphoreType.DMA(())],
    )
    return pl.pallas_call(
        _dispatch_kernel,
        grid_spec=grid_spec,
        out_shape=jax.ShapeDtypeStruct((m_pad, D_MODEL), F32),
        compiler_params=pltpu.CompilerParams(
            dimension_semantics=("arbitrary",), vmem_limit_bytes=VMEM_LIMIT),
        name="dispatch",
    )(pad_start, pad_cnt, n_used, dest, h)


def _expert_kernel(blk_e_ref, n_used_ref, xs_ref, wg_ref, wu_ref, wd_ref, ys_ref):
    i = pl.program_id(0)

    @pl.when(i < n_used_ref[0])
    def _():
        xb = xs_ref[...].astype(BF16)
        gate = jnp.dot(xb, wg_ref[0].astype(BF16), preferred_element_type=F32)
        up = jnp.dot(xb, wu_ref[0].astype(BF16), preferred_element_type=F32)
        hid = gate * _sigmoid(gate) * up
        ys_ref[...] = _dot(hid, wd_ref[0])

    @pl.when(i >= n_used_ref[0])
    def _():
        ys_ref[...] = jnp.zeros_like(ys_ref)


def _expert_call(blk_e, n_used, xs, exp_gate, exp_up, exp_down):
    m_pad = xs.shape[0]
    n_blocks = m_pad // BM
    row_blk = lambda i, be, nu: (jnp.minimum(i, nu[0] - 1), 0)
    out_blk = lambda i, be, nu: (i, 0)
    w_blk = lambda i, be, nu: (be[jnp.minimum(i, nu[0] - 1)], 0, 0)
    grid_spec = pltpu.PrefetchScalarGridSpec(
        num_scalar_prefetch=2,
        grid=(n_blocks,),
        in_specs=[pl.BlockSpec((BM, D_MODEL), row_blk),
                  pl.BlockSpec((1, D_MODEL, D_EXPERT), w_blk),
                  pl.BlockSpec((1, D_MODEL, D_EXPERT), w_blk),
                  pl.BlockSpec((1, D_EXPERT, D_MODEL), w_blk)],
        out_specs=pl.BlockSpec((BM, D_MODEL), out_blk),
    )
    return pl.pallas_call(
        _expert_kernel,
        grid_spec=grid_spec,
        out_shape=jax.ShapeDtypeStruct((m_pad, D_MODEL), F32),
        compiler_params=pltpu.CompilerParams(
            dimension_semantics=("arbitrary",), vmem_limit_bytes=VMEM_LIMIT),
        name="experts",
    )(blk_e, n_used, xs, exp_gate, exp_up, exp_down)


def _combine_kernel(dest_ref, h_ref, meta_ref, ys_ref, g_ref, b_ref, out_ref, y1_ref, y2_ref, sem):
    ts = h_ref.shape[0]

    def row_copy(t, slot, buf):
        return pltpu.make_async_copy(ys_ref.at[pl.ds(dest_ref[0, 0, slot * ts + t], 1)],
                                     buf.at[pl.ds(t, 1)], sem)

    def issue(t, carry):
        row_copy(t, 0, y1_ref).start()
        row_copy(t, 1, y2_ref).start()
        return carry

    def drain(t, carry):
        row_copy(t, 0, y1_ref).wait()
        row_copy(t, 1, y2_ref).wait()
        return carry

    lax.fori_loop(0, ts, issue, 0)
    lax.fori_loop(0, ts, drain, 0)
    meta = meta_ref[...]
    ffn = y1_ref[...] * meta[:, 2:3] + y2_ref[...] * meta[:, 3:4]
    out_ref[...] = _layer_norm(ALPHA * h_ref[...] + ffn, g_ref[...], b_ref[...])


def _combine_call(dest, h, meta, ys, ln2_g, ln2_b):
    n = h.shape[0]
    full = lambda arr: pl.BlockSpec(arr.shape, lambda i: (0,) * arr.ndim)
    return pl.pallas_call(
        _combine_kernel,
        grid=(n // TS,),
        in_specs=[pl.BlockSpec((1, 1, 2 * TS), lambda i: (i, 0, 0), memory_space=pltpu.SMEM),
                  pl.BlockSpec((TS, D_MODEL), lambda i: (i, 0)),
                  pl.BlockSpec((TS, LANES), lambda i: (i, 0)),
                  pl.BlockSpec(memory_space=pl.ANY),
                  full(ln2_g), full(ln2_b)],
        out_specs=pl.BlockSpec((TS, D_MODEL), lambda i: (i, 0)),
        out_shape=jax.ShapeDtypeStruct((n, D_MODEL), F32),
        scratch_shapes=[pltpu.VMEM((TS, D_MODEL), F32), pltpu.VMEM((TS, D_MODEL), F32),
                        pltpu.SemaphoreType.DMA(())],
        compiler_params=pltpu.CompilerParams(
            dimension_semantics=("arbitrary",), vmem_limit_bytes=VMEM_LIMIT),
        name="combine",
    )(dest, h, meta, ys, ln2_g, ln2_b)


def _block_diag(blocks):
    g, c, d = blocks.shape
    eye = jnp.eye(g, dtype=blocks.dtype)
    return (eye[:, None, :, None] * blocks[:, :, None, :]).reshape(g * c, g * d)


def _layer(x, w_in, pool_w, pool_scale, mu_shift, w0, w_up, a0, a_up, g_up, k_k, k_a, r_k,
           lnx_g, lnx_b, w_out, ln1_g, ln1_b, router_group, router_group_b, router_expert,
           router_expert_b, exp_gate, exp_up, exp_down, ln2_g, ln2_b):
    b, s, d = x.shape
    n = b * s
    row2 = lambda t: t.reshape(1, -1)

    ones_bd = _block_diag(jnp.ones((2, HEAD, HEAD), BF16))
    lora_w = _block_diag(jnp.stack([w_up, a_up])).astype(BF16)
    poolw_bd = _block_diag(pool_w).astype(BF16)
    rw = jnp.zeros((D_MODEL, LANES), F32)
    rw = rw.at[:, 0:N_GROUPS].set(router_group).at[:, EXPERT_LANE0:EXPERT_LANE0 + N_EXPERTS].set(router_expert)
    rw_hi = rw.astype(BF16)
    rw_lo = (rw - rw_hi.astype(F32)).astype(BF16)
    rb = jnp.zeros((1, LANES), F32)
    rb = rb.at[0, 0:N_GROUPS].set(router_group_b).at[0, EXPERT_LANE0:EXPERT_LANE0 + N_EXPERTS].set(router_expert_b)
    tri = (lax.broadcasted_iota(jnp.int32, (TS, TS), 0) > lax.broadcasted_iota(jnp.int32, (TS, TS), 1)).astype(BF16)

    pool_out, r, w, k, v, a_vec, b_vec, g = _prep_call(
        x, w_in.astype(BF16), row2(mu_shift), row2(w0), row2(a0), lora_w, g_up.astype(BF16),
        row2(k_k), row2(k_a), poolw_bd, row2(pool_scale), ones_bd)
    o = _rwkv_call(r, w, k, v, a_vec, b_vec)

    flat = lambda t: t.reshape(n, t.shape[-1])
    w_out_bf = w_out.astype(BF16)
    h, meta, meta_t, counts = _post_call(
        flat(x), flat(pool_out), flat(o), flat(r), flat(k), flat(v), flat(g),
        row2(lnx_g), row2(lnx_b), row2(r_k), ones_bd, w_out_bf[0:D_POOL], w_out_bf[D_POOL:],
        row2(ln1_g), row2(ln1_b), rw_hi, rw_lo, rb, tri)

    cnt = counts[0, EXPERT_LANE0:EXPERT_LANE0 + N_EXPERTS].astype(jnp.int32)
    padded = (cnt + BM - 1) // BM * BM
    pend = jnp.cumsum(padded)
    pstart = pend - padded
    e_ids = meta_t[0:2].astype(jnp.int32)
    onehot = e_ids[:, :, None] == jnp.arange(N_EXPERTS, dtype=jnp.int32)
    dest = jnp.sum(jnp.where(onehot, pstart, 0), axis=-1) + meta_t[4:6].astype(jnp.int32)
    dest = dest.reshape(2, n // TS, TS).transpose(1, 0, 2).reshape(n // TS, 1, 2 * TS)
    m_pad = 2 * n + N_EXPERTS * BM
    n_blocks = m_pad // BM
    blk_start = jnp.arange(n_blocks, dtype=jnp.int32) * BM
    blk_e = jnp.minimum(jnp.sum(blk_start[:, None] >= pend[None, :], axis=1), N_EXPERTS - 1).astype(jnp.int32)
    n_used = (pend[-1:] // BM).astype(jnp.int32)

    xs = _dispatch_call(pstart + cnt, padded - cnt, n_used, dest, h, m_pad)
    ys = _expert_call(blk_e, n_used, xs, exp_gate, exp_up, exp_down)
    out = _combine_call(dest, h, meta, ys, row2(ln2_g), row2(ln2_b))
    return out.reshape(b, s, d)


def kernel(x, w_in, pool_w, pool_scale, mu_shift, w0, w_up, a0, a_up, g_up, k_k, k_a, r_k, lnx_g, lnx_b, w_out, ln1_g, ln1_b, router_group, router_group_b, router_expert, router_expert_b, exp_gate, exp_up, exp_down, ln2_g, ln2_b):
    depth = w_in.shape[0]
    for l in range(depth):
        x = _layer(x, w_in[l], pool_w[l], pool_scale[l], mu_shift[l], w0[l], w_up[l], a0[l],
                   a_up[l], g_up[l], k_k[l], k_a[l], r_k[l], lnx_g[l], lnx_b[l], w_out[l],
                   ln1_g[l], ln1_b[l], router_group[l], router_group_b[l], router_expert[l],
                   router_expert_b[l], exp_gate[l], exp_up[l], exp_down[l], ln2_g[l], ln2_b[l])
    return x
```

```python
import functools

import jax
import jax.numpy as jnp
from jax import lax
from jax.experimental import pallas as pl
from jax.experimental.pallas import tpu as pltpu

F32 = jnp.float32
BF16 = jnp.bfloat16

D_MODEL = 1024
D_POOL = 256
POOL_WINDOWS = (2, 4, 8, 16)
POOL_GROUP = 64
POOL_HALO = 16
D_RWKV = 768
HEAD = 64
D_DECAY_LORA = 64
D_AAA_LORA = 64
D_GATE_LORA = 128
D_RWKV_IN = 3 * D_RWKV + D_DECAY_LORA + D_AAA_LORA + D_GATE_LORA
D_IN = D_POOL + D_RWKV_IN
N_GROUPS = 4
EXPERTS_PER_GROUP = 8
N_EXPERTS = 32
D_EXPERT = 256
LN_EPS = 1e-5
LNX_EPS = 64e-5
ALPHA = 2.0 ** 0.25

LANES = 128
VMEM_LIMIT = 56 * 1024 * 1024

PAIR = 2 * HEAD
N_PAIRS = D_RWKV // PAIR
CHUNK = 64
RWKV_TILE = 2 * CHUNK
TS = 256
BM = 256
DMA_UNROLL = 8
EXPERT_LANE0 = 32


def _dot(a, b):
    return jnp.dot(a.astype(BF16), b.astype(BF16), preferred_element_type=F32)


def _dot_nt(a, b):
    return lax.dot_general(a.astype(BF16), b.astype(BF16), (((1,), (1,)), ((), ())),
                           preferred_element_type=F32)


def _dot_tn(a, b):
    return lax.dot_general(a.astype(BF16), b.astype(BF16), (((0,), (0,)), ((), ())),
                           preferred_element_type=F32)


def _dot_split(x, w_bf16):
    hi = x.astype(BF16)
    lo = (x - hi.astype(F32)).astype(BF16)
    return (jnp.dot(hi, w_bf16, preferred_element_type=F32)
            + jnp.dot(lo, w_bf16, preferred_element_type=F32))


def _head_sum(x, ones_bd):
    parts = [_dot_split(x[:, p * PAIR:(p + 1) * PAIR], ones_bd) for p in range(N_PAIRS)]
    return jnp.concatenate(parts, axis=1)


def _sigmoid(x):
    return 1.0 / (1.0 + jnp.exp(-x))


def _prep_kernel(x_ref, win_ref, mu_ref, w0_ref, a0_ref, lora_ref, gup_ref, kk_ref, ka_ref,
                 poolw_ref, pools_ref, ones_ref,
                 pool_o, r_o, w_o, k_o, v_o, a_o, b_o, g_o,
                 proj_ref, halo_ref):
    i = pl.program_id(1)
    ts = x_ref.shape[1]

    @pl.when(i == 0)
    def _():
        halo_ref[...] = jnp.zeros_like(halo_ref)

    proj_ref[...] = jnp.dot(x_ref[0].astype(BF16), win_ref[...], preferred_element_type=F32)

    row = lax.broadcasted_iota(jnp.int32, (ts, 1), 0)

    def shifted(off, width):
        z = proj_ref[:, off:off + width]
        prev = jnp.where(row == 0, halo_ref[POOL_HALO - 1:POOL_HALO, off:off + width],
                         pltpu.roll(z, 1, 0))
        return z + (prev - z) * mu_ref[:, off - D_POOL:off - D_POOL + width]

    p = proj_ref[:, 0:D_POOL]
    ext = jnp.concatenate([halo_ref[:, 0:D_POOL], p], axis=0)
    s2 = ext + pltpu.roll(ext, 1, 0)
    s4 = s2 + pltpu.roll(s2, 2, 0)
    s8 = s4 + pltpu.roll(s4, 4, 0)
    s16 = s8 + pltpu.roll(s8, 8, 0)
    lane = lax.broadcasted_iota(jnp.int32, (ts, D_POOL), 1)
    grp = lane // POOL_GROUP
    wsum = jnp.where(grp == 0, s2[POOL_HALO:], jnp.where(grp == 1, s4[POOL_HALO:],
                     jnp.where(grp == 2, s8[POOL_HALO:], s16[POOL_HALO:])))
    win = jnp.where(grp == 0, 2.0, jnp.where(grp == 1, 4.0, jnp.where(grp == 2, 8.0, 16.0)))
    pos = (i * ts + row + 1).astype(F32)
    diff = wsum / jnp.minimum(pos, win) - p
    pool_o[0] = _dot(diff, poolw_ref[...]) * pools_ref[...]

    o = D_POOL
    r = shifted(o, D_RWKV)
    k = shifted(o + D_RWKV, D_RWKV)
    v = shifted(o + 2 * D_RWKV, D_RWKV)
    lw = shifted(o + 3 * D_RWKV, D_DECAY_LORA + D_AAA_LORA)
    gd = shifted(o + 3 * D_RWKV + D_DECAY_LORA + D_AAA_LORA, D_GATE_LORA)

    lane128 = lax.broadcasted_iota(jnp.int32, (ts, LANES), 1)
    lora_in = jnp.where(lane128 < D_DECAY_LORA, jnp.tanh(lw), lw)
    lora = _dot(lora_in, lora_ref[...])
    wpre = -(w0_ref[...] + lora[:, 0:D_RWKV])
    softplus = jnp.maximum(wpre, 0.0) + jnp.log(1.0 + jnp.exp(-jnp.abs(wpre)))
    w_log = -softplus - 0.5
    w_o[0] = -jnp.exp(w_log)
    eta = _sigmoid(a0_ref[...] + lora[:, D_RWKV:2 * D_RWKV])
    g_o[0] = _dot(_sigmoid(gd), gup_ref[...])
    kk = k * kk_ref[...]
    ss = _head_sum(kk * kk, ones_ref[...])
    kkn = kk / jnp.maximum(jnp.sqrt(ss), 1e-12)
    r_o[0] = r
    v_o[0] = v
    k_o[0] = k * (1.0 + (eta - 1.0) * ka_ref[...])
    a_o[0] = -kkn
    b_o[0] = kkn * eta

    halo_ref[...] = proj_ref[ts - POOL_HALO:ts, :]


def _prep_call(x, win_bf, mu, w0, a0, lora_w, gup_bf, k_k, k_a, poolw_bd, pool_scale, ones_bd):
    b, s, _ = x.shape
    grid = (b, s // TS)
    full = lambda arr: pl.BlockSpec(arr.shape, lambda bi, i: (0,) * arr.ndim)
    tok = lambda width: pl.BlockSpec((1, TS, width), lambda bi, i: (bi, i, 0))
    outs = [jax.ShapeDtypeStruct((b, s, D_POOL), F32)] + [jax.ShapeDtypeStruct((b, s, D_RWKV), F32)] * 7
    params = (win_bf, mu, w0, a0, lora_w, gup_bf, k_k, k_a, poolw_bd, pool_scale, ones_bd)
    return pl.pallas_call(
        _prep_kernel,
        grid=grid,
        in_specs=[tok(D_MODEL)] + [full(a) for a in params],
        out_specs=[tok(D_POOL)] + [tok(D_RWKV)] * 7,
        out_shape=outs,
        scratch_shapes=[pltpu.VMEM((TS, D_IN), F32), pltpu.VMEM((POOL_HALO, D_IN), F32)],
        compiler_params=pltpu.CompilerParams(
            dimension_semantics=("parallel", "arbitrary"), vmem_limit_bytes=VMEM_LIMIT),
        name="prep",
    )(x, *params)


def _rwkv_kernel(r_ref, w_ref, k_ref, v_ref, a_ref, b_ref, o_ref, h_ref):
    c = pl.program_id(1)

    @pl.when(c == 0)
    def _():
        h_ref[...] = jnp.zeros_like(h_ref)

    L = CHUNK
    n_chunks = r_ref.shape[1] // L
    row = lax.broadcasted_iota(jnp.int32, (L, PAIR), 0)
    lane = lax.broadcasted_iota(jnp.int32, (L, PAIR), 1)
    head0 = lane < HEAD
    strict = row > (lane & (L - 1))
    incl = row >= (lane & (L - 1))
    eye_wide = jnp.where(row == (lane & (L - 1)), 1.0, 0.0)

    def expand(x):
        xb = x.astype(BF16)
        zero = jnp.zeros_like(xb)
        return jnp.concatenate([jnp.where(head0, xb, zero), jnp.where(head0, zero, xb)], axis=0)

    def tile(ref, j, p):
        return ref[0, j * L:(j + 1) * L, p * PAIR:(p + 1) * PAIR]

    streams = [(j, p) for j in range(n_chunks) for p in range(N_PAIRS)]
    st = []
    for j, p in streams:
        w = tile(w_ref, j, p)
        cum = w
        for sh in (1, 2, 4, 8, 16, 32):
            cum = cum + jnp.where(row >= sh, pltpu.roll(cum, sh, 0), 0.0)
        tot = cum[L - 1:L, :]
        e_neg = jnp.exp(-cum)
        e_rem = jnp.exp(tot - cum)
        b = tile(b_ref, j, p)
        k = tile(k_ref, j, p)
        a_n = tile(a_ref, j, p) * jnp.exp(cum - w)
        st.append(dict(
            ar=jnp.concatenate([a_n, tile(r_ref, j, p) * jnp.exp(cum)], axis=0).astype(BF16),
            a_e=expand(a_n),
            v_e=expand(tile(v_ref, j, p)),
            bk_t=jnp.concatenate([expand(b * e_neg), expand(k * e_neg)], axis=0),
            bk_h=jnp.concatenate([expand(b * e_rem), expand(k * e_rem)], axis=0),
            w_tot=jnp.exp(tot)))
    for d in st:
        sc = _dot_nt(d['ar'], d['bk_t'])
        d['t'] = jnp.where(strict, sc[0:L, 0:PAIR], 0.0)
        d['t_ak'] = jnp.where(strict, sc[0:L, PAIR:2 * PAIR], 0.0)
        d['r_all'] = jnp.concatenate([d['ar'][L:2 * L],
                                      jnp.where(incl, sc[L:2 * L, 0:PAIR], 0.0).astype(BF16),
                                      jnp.where(incl, sc[L:2 * L, PAIR:2 * PAIR], 0.0).astype(BF16)],
                                     axis=1)
    for d in st:
        d['takv'] = _dot(d['t_ak'], d['v_e'])
        d['m'] = eye_wide + d['t']
        d['t'] = _dot(d['t'], expand(d['t']))
    for step in range(5):
        for d in st:
            if step < 4:
                both = _dot(d['t'], jnp.concatenate([expand(d['t']), expand(d['m'])], axis=1))
                d['t'] = both[:, 0:PAIR]
                d['m'] = d['m'] + both[:, PAIR:2 * PAIR]
            else:
                d['m'] = d['m'] + _dot(d['t'], expand(d['m']))
    for d in st:
        d['pq'] = _dot(d['m'], jnp.concatenate([d['a_e'], expand(d['takv'])], axis=1))

    h = [h_ref[p] for p in range(N_PAIRS)]
    for j in range(n_chunks):
        ds = st[j * N_PAIRS:(j + 1) * N_PAIRS]
        hb = [hp.astype(BF16) for hp in h]
        u_e = [expand(_dot(d['pq'][:, 0:PAIR], hb[p]) + d['pq'][:, PAIR:2 * PAIR])
               for p, d in enumerate(ds)]
        for p, d in enumerate(ds):
            o_ref[0, j * L:(j + 1) * L, p * PAIR:(p + 1) * PAIR] = _dot(
                d['r_all'], jnp.concatenate([hb[p], u_e[p], d['v_e']], axis=0))
        for p, d in enumerate(ds):
            h_add = _dot_tn(d['bk_h'], jnp.concatenate([u_e[p], d['v_e']], axis=0))
            w_col = jnp.transpose(jnp.broadcast_to(d['w_tot'], (PAIR, PAIR)))
            h[p] = h[p] * w_col + h_add
    for p in range(N_PAIRS):
        h_ref[p] = h[p]


def _rwkv_call(r, w, k, v, a, b):
    bsz, s, _ = r.shape
    spec = pl.BlockSpec((1, RWKV_TILE, D_RWKV), lambda bi, c: (bi, c, 0))
    return pl.pallas_call(
        _rwkv_kernel,
        grid=(bsz, s // RWKV_TILE),
        in_specs=[spec] * 6,
        out_specs=spec,
        out_shape=jax.ShapeDtypeStruct((bsz, s, D_RWKV), F32),
        scratch_shapes=[pltpu.VMEM((N_PAIRS, PAIR, PAIR), F32)],
        compiler_params=pltpu.CompilerParams(
            dimension_semantics=("parallel", "arbitrary"), vmem_limit_bytes=VMEM_LIMIT),
        name="rwkv",
    )(r, w, k, v, a, b)


def _layer_norm(x, g, b):
    mu = jnp.mean(x, axis=-1, keepdims=True)
    xc = x - mu
    var = jnp.mean(xc * xc, axis=-1, keepdims=True)
    return xc * lax.rsqrt(var + LN_EPS) * g + b


def _post_kernel(x_ref, pool_ref, o_ref, r_ref, k_ref, v_ref, g_ref,
                 lnxg_ref, lnxb_ref, rk_ref, ones_ref, wop_ref, wor_ref, ln1g_ref, ln1b_ref,
                 rw_hi_ref, rw_lo_ref, rb_ref, tri_ref,
                 h_o, meta_o, meta_t_o, cnt_o, base_ref):
    i = pl.program_id(0)
    ts = x_ref.shape[0]

    @pl.when(i == 0)
    def _():
        base_ref[...] = jnp.zeros_like(base_ref)

    ones = ones_ref[...]
    o = o_ref[...]
    mu = _head_sum(o, ones) * (1.0 / HEAD)
    oc = o - mu
    var = _head_sum(oc * oc, ones) * (1.0 / HEAD)
    y = oc * lax.rsqrt(var + LNX_EPS) * lnxg_ref[...] + lnxb_ref[...]
    v = v_ref[...]
    bonus = _head_sum(r_ref[...] * k_ref[...] * rk_ref[...], ones) * v
    y = (y + bonus) * g_ref[...]
    mixed = _dot(pool_ref[...], wop_ref[...]) + _dot(y, wor_ref[...])
    h = _layer_norm(ALPHA * x_ref[...] + mixed, ln1g_ref[...], ln1b_ref[...])
    h_o[...] = h

    h_hi = h.astype(BF16)
    h_lo = (h - h_hi.astype(F32)).astype(BF16)
    logits = (jnp.dot(h_hi, rw_hi_ref[...], preferred_element_type=F32)
              + jnp.dot(h_lo, rw_hi_ref[...], preferred_element_type=F32)
              + jnp.dot(h_hi, rw_lo_ref[...], preferred_element_type=F32)) + rb_ref[...]
    lane = lax.broadcasted_iota(jnp.int32, (ts, LANES), 1)
    neg = -jnp.inf
    gl = jnp.where(lane < N_GROUPS, logits, neg)
    gmax = jnp.max(gl, axis=-1, keepdims=True)
    g_idx = jnp.min(jnp.where(gl == gmax, lane, LANES), axis=-1, keepdims=True)
    g_top_p = 1.0 / jnp.sum(jnp.exp(gl - gmax), axis=-1, keepdims=True)
    lo_lane = EXPERT_LANE0 + g_idx * EXPERTS_PER_GROUP
    el = jnp.where((lane >= lo_lane) & (lane < lo_lane + EXPERTS_PER_GROUP), logits, neg)
    m1 = jnp.max(el, axis=-1, keepdims=True)
    i1 = jnp.min(jnp.where(el == m1, lane, LANES), axis=-1, keepdims=True)
    el2 = jnp.where(lane == i1, neg, el)
    m2 = jnp.max(el2, axis=-1, keepdims=True)
    i2 = jnp.min(jnp.where(el2 == m2, lane, LANES), axis=-1, keepdims=True)
    e21 = jnp.exp(m2 - m1)
    wgt1 = g_top_p / (1.0 + e21)
    wgt2 = g_top_p * e21 / (1.0 + e21)

    sel1 = lane == i1
    sel2 = lane == i2
    onehot = jnp.where(sel1 | sel2, 1.0, 0.0)
    before = jnp.dot(tri_ref[...], onehot.astype(BF16), preferred_element_type=F32)
    posn = base_ref[0:1, :] + before
    rank1 = jnp.sum(jnp.where(sel1, posn, 0.0), axis=-1, keepdims=True)
    rank2 = jnp.sum(jnp.where(sel2, posn, 0.0), axis=-1, keepdims=True)
    new_base = base_ref[0:1, :] + jnp.sum(onehot, axis=0, keepdims=True)
    base_ref[...] = jnp.broadcast_to(new_base, base_ref.shape)
    cnt_o[...] = jnp.broadcast_to(new_base, cnt_o.shape)

    e1 = (i1 - EXPERT_LANE0).astype(F32)
    e2 = (i2 - EXPERT_LANE0).astype(F32)
    meta = jnp.where(lane == 0, e1, jnp.where(lane == 1, e2, jnp.where(lane == 2, wgt1,
           jnp.where(lane == 3, wgt2, jnp.where(lane == 4, rank1, jnp.where(lane == 5, rank2, 0.0))))))
    meta_o[...] = meta
    meta_t_o[...] = jnp.transpose(meta)[0:8, :]


def _post_call(x2, pool2, o2, r2, k2, v2, g2, lnx_g, lnx_b, rk, ones_bd, wo_pool, wo_rwkv,
               ln1_g, ln1_b, rw_hi, rw_lo, rb, tri):
    n = x2.shape[0]
    full = lambda arr: pl.BlockSpec(arr.shape, lambda i: (0,) * arr.ndim)
    tok = lambda width: pl.BlockSpec((TS, width), lambda i: (i, 0))
    params = (lnx_g, lnx_b, rk, ones_bd, wo_pool, wo_rwkv, ln1_g, ln1_b, rw_hi, rw_lo, rb, tri)
    return pl.pallas_call(
        _post_kernel,
        grid=(n // TS,),
        in_specs=[tok(D_MODEL), tok(D_POOL)] + [tok(D_RWKV)] * 5 + [full(a) for a in params],
        out_specs=[tok(D_MODEL), tok(LANES), pl.BlockSpec((8, TS), lambda i: (0, i)),
                   pl.BlockSpec((8, LANES), lambda i: (0, 0))],
        out_shape=[jax.ShapeDtypeStruct((n, D_MODEL), F32), jax.ShapeDtypeStruct((n, LANES), F32),
                   jax.ShapeDtypeStruct((8, n), F32), jax.ShapeDtypeStruct((8, LANES), F32)],
        scratch_shapes=[pltpu.VMEM((8, LANES), F32)],
        compiler_params=pltpu.CompilerParams(
            dimension_semantics=("arbitrary",), vmem_limit_bytes=VMEM_LIMIT),
        name="post",
    )(x2, pool2, o2, r2, k2, v2, g2, *params)


def _dispatch_kernel(pad_start_ref, pad_cnt_ref, n_used_ref, dest_ref, h_ref, xs_ref, zero_ref,
                     sem, zsem):
    i = pl.program_id(0)
    ts = dest_ref.shape[2] // 2
    n_blocks = xs_ref.shape[0] // BM

    def row_copy(t, slot):
        return pltpu.make_async_copy(h_ref.at[pl.ds(i * ts + t, 1)],
                                     xs_ref.at[pl.ds(dest_ref[0, 0, slot * ts + t], 1)], sem)

    def wait_rows(rows):
        pltpu.make_async_copy(h_ref.at[pl.ds(0, rows)], xs_ref.at[pl.ds(0, rows)], sem).wait()

    def zero_row_copy(row):
        return pltpu.make_async_copy(zero_ref.at[pl.ds(0, 1)], xs_ref.at[pl.ds(row, 1)], sem)

    def zero_block_copy(blk):
        return pltpu.make_async_copy(zero_ref, xs_ref.at[pl.ds(pl.multiple_of(blk * BM, BM), BM)], zsem)

    @pl.when(i == 0)
    def _():
        zero_ref[...] = jnp.zeros_like(zero_ref)

        def fill(e, carry):
            start = pad_start_ref[e]

            def one(j, c):
                zero_row_copy(start + j).start()
                return c

            return lax.fori_loop(0, pad_cnt_ref[e], one, carry)

        def fill_wait(e, carry):
            def one(j, c):
                zero_row_copy(0).wait()
                return c

            return lax.fori_loop(0, pad_cnt_ref[e], one, carry)

        def tail(blk, carry):
            zero_block_copy(blk).start()
            return carry

        def tail_wait(blk, carry):
            zero_block_copy(blk).wait()
            return carry

        lax.fori_loop(0, N_EXPERTS, fill, 0)
        lax.fori_loop(n_used_ref[0], n_blocks, tail, 0)
        lax.fori_loop(0, N_EXPERTS, fill_wait, 0)
        lax.fori_loop(n_used_ref[0], n_blocks, tail_wait, 0)

    def issue(t, carry):
        row_copy(t, 0).start()
        row_copy(t, 1).start()
        return carry

    lax.fori_loop(0, ts, issue, 0, unroll=DMA_UNROLL)

    @pl.when(i > 0)
    def _():
        wait_rows(2 * ts)

    @pl.when(i == pl.num_programs(0) - 1)
    def _():
        wait_rows(2 * ts)


def _dispatch_call(pad_start, pad_cnt, n_used, dest, h, m_pad):
    n = h.shape[0]
    grid_spec = pltpu.PrefetchScalarGridSpec(
        num_scalar_prefetch=3,
        grid=(n // TS,),
        in_specs=[pl.BlockSpec((1, 1, 2 * TS), lambda i, *_: (i, 0, 0), memory_space=pltpu.SMEM),
                  pl.BlockSpec(memory_space=pl.ANY)],
        out_specs=pl.BlockSpec(memory_space=pl.ANY),
        scratch_shapes=[pltpu.VMEM((BM, D_MODEL), F32), pltpu.SemaphoreType.DMA(()),
                        pltpu.SemaphoreType.DMA(())],
    )
    return pl.pallas_call(
        _dispatch_kernel,
        grid_spec=grid_spec,
        out_shape=jax.ShapeDtypeStruct((m_pad, D_MODEL), F32),
        compiler_params=pltpu.CompilerParams(
            dimension_semantics=("arbitrary",), vmem_limit_bytes=VMEM_LIMIT,
            disable_bounds_checks=True),
        name="dispatch",
    )(pad_start, pad_cnt, n_used, dest, h)


def _expert_kernel(blk_e_ref, n_used_ref, xs_ref, wg_ref, wu_ref, wd_ref, ys_ref):
    i = pl.program_id(0)

    @pl.when(i < n_used_ref[0])
    def _():
        xb = xs_ref[...].astype(BF16)
        gate = jnp.dot(xb, wg_ref[0].astype(BF16), preferred_element_type=F32)
        up = jnp.dot(xb, wu_ref[0].astype(BF16), preferred_element_type=F32)
        hid = gate * _sigmoid(gate) * up
        ys_ref[...] = _dot(hid, wd_ref[0])

    @pl.when(i >= n_used_ref[0])
    def _():
        ys_ref[...] = jnp.zeros_like(ys_ref)


def _expert_call(blk_e, n_used, xs, exp_gate, exp_up, exp_down):
    m_pad = xs.shape[0]
    n_blocks = m_pad // BM
    row_blk = lambda i, be, nu: (jnp.minimum(i, nu[0] - 1), 0)
    out_blk = lambda i, be, nu: (i, 0)
    w_blk = lambda i, be, nu: (be[jnp.minimum(i, nu[0] - 1)], 0, 0)
    grid_spec = pltpu.PrefetchScalarGridSpec(
        num_scalar_prefetch=2,
        grid=(n_blocks,),
        in_specs=[pl.BlockSpec((BM, D_MODEL), row_blk),
                  pl.BlockSpec((1, D_MODEL, D_EXPERT), w_blk),
                  pl.BlockSpec((1, D_MODEL, D_EXPERT), w_blk),
                  pl.BlockSpec((1, D_EXPERT, D_MODEL), w_blk)],
        out_specs=pl.BlockSpec((BM, D_MODEL), out_blk),
    )
    return pl.pallas_call(
        _expert_kernel,
        grid_spec=grid_spec,
        out_shape=jax.ShapeDtypeStruct((m_pad, D_MODEL), F32),
        compiler_params=pltpu.CompilerParams(
            dimension_semantics=("arbitrary",), vmem_limit_bytes=VMEM_LIMIT),
        name="experts",
    )(blk_e, n_used, xs, exp_gate, exp_up, exp_down)


def _combine_kernel(dest_ref, dest_next_ref, h_ref, meta_ref, ys_ref, g_ref, b_ref, out_ref,
                    y_ref, sem):
    i = pl.program_id(0)
    n_steps = pl.num_programs(0)
    ts = h_ref.shape[0]

    def gather(idx_ref, buf):
        def issue(t, carry):
            for slot in range(2):
                pltpu.make_async_copy(ys_ref.at[pl.ds(idx_ref[0, 0, slot * ts + t], 1)],
                                      y_ref.at[buf, slot, pl.ds(t, 1)], sem.at[buf]).start()
            return carry

        lax.fori_loop(0, ts, issue, 0, unroll=DMA_UNROLL)

    cur = lax.rem(i, 2)

    @pl.when(i == 0)
    def _():
        gather(dest_ref, 0)

    @pl.when(i + 1 < n_steps)
    def _():
        gather(dest_next_ref, 1 - cur)

    for slot in range(2):
        pltpu.make_async_copy(ys_ref.at[pl.ds(0, ts)], y_ref.at[cur, slot], sem.at[cur]).wait()
    meta = meta_ref[...]
    ffn = y_ref[cur, 0] * meta[:, 2:3] + y_ref[cur, 1] * meta[:, 3:4]
    out_ref[...] = _layer_norm(ALPHA * h_ref[...] + ffn, g_ref[...], b_ref[...])


def _combine_call(dest, h, meta, ys, ln2_g, ln2_b):
    n = h.shape[0]
    full = lambda arr: pl.BlockSpec(arr.shape, lambda i: (0,) * arr.ndim)
    return pl.pallas_call(
        _combine_kernel,
        grid=(n // TS,),
        in_specs=[pl.BlockSpec((1, 1, 2 * TS), lambda i: (i, 0, 0), memory_space=pltpu.SMEM),
                  pl.BlockSpec((1, 1, 2 * TS), lambda i: (jnp.minimum(i + 1, n // TS - 1), 0, 0),
                               memory_space=pltpu.SMEM),
                  pl.BlockSpec((TS, D_MODEL), lambda i: (i, 0)),
                  pl.BlockSpec((TS, LANES), lambda i: (i, 0)),
                  pl.BlockSpec(memory_space=pl.ANY),
                  full(ln2_g), full(ln2_b)],
        out_specs=pl.BlockSpec((TS, D_MODEL), lambda i: (i, 0)),
        out_shape=jax.ShapeDtypeStruct((n, D_MODEL), F32),
        scratch_shapes=[pltpu.VMEM((2, 2, TS, D_MODEL), F32), pltpu.SemaphoreType.DMA((2,))],
        compiler_params=pltpu.CompilerParams(
            dimension_semantics=("arbitrary",), vmem_limit_bytes=VMEM_LIMIT,
            disable_bounds_checks=True),
        name="combine",
    )(dest, dest, h, meta, ys, ln2_g, ln2_b)


def _block_diag(blocks):
    g, c, d = blocks.shape
    eye = jnp.eye(g, dtype=blocks.dtype)
    return (eye[:, None, :, None] * blocks[:, :, None, :]).reshape(g * c, g * d)


def _layer(x, w_in, pool_w, pool_scale, mu_shift, w0, w_up, a0, a_up, g_up, k_k, k_a, r_k,
           lnx_g, lnx_b, w_out, ln1_g, ln1_b, router_group, router_group_b, router_expert,
           router_expert_b, exp_gate, exp_up, exp_down, ln2_g, ln2_b):
    b, s, d = x.shape
    n = b * s
    row2 = lambda t: t.reshape(1, -1)

    ones_bd = _block_diag(jnp.ones((2, HEAD, HEAD), BF16))
    lora_w = _block_diag(jnp.stack([w_up, a_up])).astype(BF16)
    poolw_bd = _block_diag(pool_w).astype(BF16)
    rw = jnp.zeros((D_MODEL, LANES), F32)
    rw = rw.at[:, 0:N_GROUPS].set(router_group).at[:, EXPERT_LANE0:EXPERT_LANE0 + N_EXPERTS].set(router_expert)
    rw_hi = rw.astype(BF16)
    rw_lo = (rw - rw_hi.astype(F32)).astype(BF16)
    rb = jnp.zeros((1, LANES), F32)
    rb = rb.at[0, 0:N_GROUPS].set(router_group_b).at[0, EXPERT_LANE0:EXPERT_LANE0 + N_EXPERTS].set(router_expert_b)
    tri = (lax.broadcasted_iota(jnp.int32, (TS, TS), 0) > lax.broadcasted_iota(jnp.int32, (TS, TS), 1)).astype(BF16)

    pool_out, r, w, k, v, a_vec, b_vec, g = _prep_call(
        x, w_in.astype(BF16), row2(mu_shift), row2(w0), row2(a0), lora_w, g_up.astype(BF16),
        row2(k_k), row2(k_a), poolw_bd, row2(pool_scale), ones_bd)
    o = _rwkv_call(r, w, k, v, a_vec, b_vec)

    flat = lambda t: t.reshape(n, t.shape[-1])
    w_out_bf = w_out.astype(BF16)
    h, meta, meta_t, counts = _post_call(
        flat(x), flat(pool_out), flat(o), flat(r), flat(k), flat(v), flat(g),
        row2(lnx_g), row2(lnx_b), row2(r_k), ones_bd, w_out_bf[0:D_POOL], w_out_bf[D_POOL:],
        row2(ln1_g), row2(ln1_b), rw_hi, rw_lo, rb, tri)

    cnt = counts[0, EXPERT_LANE0:EXPERT_LANE0 + N_EXPERTS].astype(jnp.int32)
    padded = (cnt + BM - 1) // BM * BM
    pend = jnp.cumsum(padded)
    pstart = pend - padded
    e_ids = meta_t[0:2].astype(jnp.int32)
    onehot = e_ids[:, :, None] == jnp.arange(N_EXPERTS, dtype=jnp.int32)
    dest = jnp.sum(jnp.where(onehot, pstart, 0), axis=-1) + meta_t[4:6].astype(jnp.int32)
    dest = dest.reshape(2, n // TS, TS).transpose(1, 0, 2).reshape(n // TS, 1, 2 * TS)
    m_pad = 2 * n + N_EXPERTS * BM
    n_blocks = m_pad // BM
    blk_start = jnp.arange(n_blocks, dtype=jnp.int32) * BM
    blk_e = jnp.minimum(jnp.sum(blk_start[:, None] >= pend[None, :], axis=1), N_EXPERTS - 1).astype(jnp.int32)
    n_used = (pend[-1:] // BM).astype(jnp.int32)

    xs = _dispatch_call(pstart + cnt, padded - cnt, n_used, dest, h, m_pad)
    ys = _expert_call(blk_e, n_used, xs, exp_gate, exp_up, exp_down)
    out = _combine_call(dest, h, meta, ys, row2(ln2_g), row2(ln2_b))
    return out.reshape(b, s, d)


def kernel(x, w_in, pool_w, pool_scale, mu_shift, w0, w_up, a0, a_up, g_up, k_k, k_a, r_k, lnx_g, lnx_b, w_out, ln1_g, ln1_b, router_group, router_group_b, router_expert, router_expert_b, exp_gate, exp_up, exp_down, ln2_g, ln2_b):
    depth = w_in.shape[0]
    for l in range(depth):
        x = _layer(x, w_in[l], pool_w[l], pool_scale[l], mu_shift[l], w0[l], w_up[l], a0[l],
                   a_up[l], g_up[l], k_k[l], k_a[l], r_k[l], lnx_g[l], lnx_b[l], w_out[l],
                   ln1_g[l], ln1_b[l], router_group[l], router_group_b[l], router_expert[l],
                   router_expert_b[l], exp_gate[l], exp_up[l], exp_down[l], ln2_g[l], ln2_b[l])
    return x
```

```python
import functools

import jax
import jax.numpy as jnp
from jax import lax
from jax.experimental import pallas as pl
from jax.experimental.pallas import tpu as pltpu

F32 = jnp.float32
BF16 = jnp.bfloat16

D_MODEL = 1024
D_POOL = 256
POOL_WINDOWS = (2, 4, 8, 16)
POOL_GROUP = 64
POOL_HALO = 16
D_RWKV = 768
HEAD = 64
D_DECAY_LORA = 64
D_AAA_LORA = 64
D_GATE_LORA = 128
D_RWKV_IN = 3 * D_RWKV + D_DECAY_LORA + D_AAA_LORA + D_GATE_LORA
D_IN = D_POOL + D_RWKV_IN
N_GROUPS = 4
EXPERTS_PER_GROUP = 8
N_EXPERTS = 32
D_EXPERT = 256
LN_EPS = 1e-5
LNX_EPS = 64e-5
ALPHA = 2.0 ** 0.25

LANES = 128
VMEM_LIMIT = 56 * 1024 * 1024

PAIR = 2 * HEAD
N_PAIRS = D_RWKV // PAIR
CHUNK = 64
RWKV_TILE = 2 * CHUNK
TS = 256
BM = 256
DMA_UNROLL = 8
EXPERT_LANE0 = 32


def _dot(a, b):
    return jnp.dot(a.astype(BF16), b.astype(BF16), preferred_element_type=F32)


def _dot_nt(a, b):
    return lax.dot_general(a.astype(BF16), b.astype(BF16), (((1,), (1,)), ((), ())),
                           preferred_element_type=F32)


def _dot_tn(a, b):
    return lax.dot_general(a.astype(BF16), b.astype(BF16), (((0,), (0,)), ((), ())),
                           preferred_element_type=F32)


def _dot_split(x, w_bf16):
    hi = x.astype(BF16)
    lo = (x - hi.astype(F32)).astype(BF16)
    return (jnp.dot(hi, w_bf16, preferred_element_type=F32)
            + jnp.dot(lo, w_bf16, preferred_element_type=F32))


def _head_sum(x, ones_bd):
    parts = [_dot_split(x[:, p * PAIR:(p + 1) * PAIR], ones_bd) for p in range(N_PAIRS)]
    return jnp.concatenate(parts, axis=1)


def _sigmoid(x):
    return 1.0 / (1.0 + jnp.exp(-x))


def _prep_kernel(x_ref, win_ref, mu_ref, w0_ref, a0_ref, lora_ref, gup_ref, kk_ref, ka_ref,
                 poolw_ref, pools_ref, ones_ref,
                 pool_o, r_o, w_o, k_o, v_o, a_o, b_o, g_o,
                 proj_ref, halo_ref):
    i = pl.program_id(1)
    ts = x_ref.shape[1]

    @pl.when(i == 0)
    def _():
        halo_ref[...] = jnp.zeros_like(halo_ref)

    proj_ref[...] = jnp.dot(x_ref[0].astype(BF16), win_ref[...], preferred_element_type=F32)

    row = lax.broadcasted_iota(jnp.int32, (ts, 1), 0)

    def shifted(off, width):
        z = proj_ref[:, off:off + width]
        prev = jnp.where(row == 0, halo_ref[POOL_HALO - 1:POOL_HALO, off:off + width],
                         pltpu.roll(z, 1, 0))
        return z + (prev - z) * mu_ref[:, off - D_POOL:off - D_POOL + width]

    p = proj_ref[:, 0:D_POOL]
    ext = jnp.concatenate([halo_ref[:, 0:D_POOL], p], axis=0)
    s2 = ext + pltpu.roll(ext, 1, 0)
    s4 = s2 + pltpu.roll(s2, 2, 0)
    s8 = s4 + pltpu.roll(s4, 4, 0)
    s16 = s8 + pltpu.roll(s8, 8, 0)
    lane = lax.broadcasted_iota(jnp.int32, (ts, D_POOL), 1)
    grp = lane // POOL_GROUP
    wsum = jnp.where(grp == 0, s2[POOL_HALO:], jnp.where(grp == 1, s4[POOL_HALO:],
                     jnp.where(grp == 2, s8[POOL_HALO:], s16[POOL_HALO:])))
    win = jnp.where(grp == 0, 2.0, jnp.where(grp == 1, 4.0, jnp.where(grp == 2, 8.0, 16.0)))
    pos = (i * ts + row + 1).astype(F32)
    diff = wsum / jnp.minimum(pos, win) - p
    pool_o[0] = _dot(diff, poolw_ref[...]) * pools_ref[...]

    o = D_POOL
    r = shifted(o, D_RWKV)
    k = shifted(o + D_RWKV, D_RWKV)
    v = shifted(o + 2 * D_RWKV, D_RWKV)
    lw = shifted(o + 3 * D_RWKV, D_DECAY_LORA + D_AAA_LORA)
    gd = shifted(o + 3 * D_RWKV + D_DECAY_LORA + D_AAA_LORA, D_GATE_LORA)

    lane128 = lax.broadcasted_iota(jnp.int32, (ts, LANES), 1)
    lora_in = jnp.where(lane128 < D_DECAY_LORA, jnp.tanh(lw), lw)
    lora = _dot(lora_in, lora_ref[...])
    wpre = -(w0_ref[...] + lora[:, 0:D_RWKV])
    softplus = jnp.maximum(wpre, 0.0) + jnp.log(1.0 + jnp.exp(-jnp.abs(wpre)))
    w_log = -softplus - 0.5
    w_o[0] = -jnp.exp(w_log)
    eta = _sigmoid(a0_ref[...] + lora[:, D_RWKV:2 * D_RWKV])
    g_o[0] = _dot(_sigmoid(gd), gup_ref[...])
    kk = k * kk_ref[...]
    ss = _head_sum(kk * kk, ones_ref[...])
    kkn = kk / jnp.maximum(jnp.sqrt(ss), 1e-12)
    r_o[0] = r
    v_o[0] = v
    k_o[0] = k * (1.0 + (eta - 1.0) * ka_ref[...])
    a_o[0] = -kkn
    b_o[0] = kkn * eta

    halo_ref[...] = proj_ref[ts - POOL_HALO:ts, :]


def _prep_call(x, win_bf, mu, w0, a0, lora_w, gup_bf, k_k, k_a, poolw_bd, pool_scale, ones_bd):
    b, s, _ = x.shape
    grid = (b, s // TS)
    full = lambda arr: pl.BlockSpec(arr.shape, lambda bi, i: (0,) * arr.ndim)
    tok = lambda width: pl.BlockSpec((1, TS, width), lambda bi, i: (bi, i, 0))
    outs = [jax.ShapeDtypeStruct((b, s, D_POOL), F32)] + [jax.ShapeDtypeStruct((b, s, D_RWKV), F32)] * 7
    params = (win_bf, mu, w0, a0, lora_w, gup_bf, k_k, k_a, poolw_bd, pool_scale, ones_bd)
    return pl.pallas_call(
        _prep_kernel,
        grid=grid,
        in_specs=[tok(D_MODEL)] + [full(a) for a in params],
        out_specs=[tok(D_POOL)] + [tok(D_RWKV)] * 7,
        out_shape=outs,
        scratch_shapes=[pltpu.VMEM((TS, D_IN), F32), pltpu.VMEM((POOL_HALO, D_IN), F32)],
        compiler_params=pltpu.CompilerParams(
            dimension_semantics=("parallel", "arbitrary"), vmem_limit_bytes=VMEM_LIMIT),
        name="prep",
    )(x, *params)


def _rwkv_kernel(r_ref, w_ref, k_ref, v_ref, a_ref, b_ref, o_ref, h_ref):
    c = pl.program_id(1)

    @pl.when(c == 0)
    def _():
        h_ref[...] = jnp.zeros_like(h_ref)

    L = CHUNK
    n_chunks = r_ref.shape[1] // L
    row = lax.broadcasted_iota(jnp.int32, (L, PAIR), 0)
    lane = lax.broadcasted_iota(jnp.int32, (L, PAIR), 1)
    head0 = lane < HEAD
    strict = row > (lane & (L - 1))
    incl = row >= (lane & (L - 1))
    eye_wide = jnp.where(row == (lane & (L - 1)), 1.0, 0.0)

    def expand(x):
        xb = x.astype(BF16)
        zero = jnp.zeros_like(xb)
        return jnp.concatenate([jnp.where(head0, xb, zero), jnp.where(head0, zero, xb)], axis=0)

    def tile(ref, j, p):
        return ref[0, j * L:(j + 1) * L, p * PAIR:(p + 1) * PAIR]

    streams = [(j, p) for j in range(n_chunks) for p in range(N_PAIRS)]
    st = []
    for j, p in streams:
        w = tile(w_ref, j, p)
        cum = w
        for sh in (1, 2, 4, 8, 16, 32):
            cum = cum + jnp.where(row >= sh, pltpu.roll(cum, sh, 0), 0.0)
        tot = cum[L - 1:L, :]
        e_neg = jnp.exp(-cum)
        e_rem = jnp.exp(tot - cum)
        b = tile(b_ref, j, p)
        k = tile(k_ref, j, p)
        a_n = tile(a_ref, j, p) * jnp.exp(cum - w)
        st.append(dict(
            ar=jnp.concatenate([a_n, tile(r_ref, j, p) * jnp.exp(cum)], axis=0).astype(BF16),
            a_e=expand(a_n),
            v_e=expand(tile(v_ref, j, p)),
            bk_t=jnp.concatenate([expand(b * e_neg), expand(k * e_neg)], axis=0),
            bk_h=jnp.concatenate([expand(b * e_rem), expand(k * e_rem)], axis=0),
            w_tot=jnp.exp(tot)))
    for d in st:
        sc = _dot_nt(d['ar'], d['bk_t'])
        d['t'] = jnp.where(strict, sc[0:L, 0:PAIR], 0.0)
        d['t_ak'] = jnp.where(strict, sc[0:L, PAIR:2 * PAIR], 0.0)
        d['r_all'] = jnp.concatenate([d['ar'][L:2 * L],
                                      jnp.where(incl, sc[L:2 * L, 0:PAIR], 0.0).astype(BF16),
                                      jnp.where(incl, sc[L:2 * L, PAIR:2 * PAIR], 0.0).astype(BF16)],
                                     axis=1)
    for d in st:
        d['takv'] = _dot(d['t_ak'], d['v_e'])
        d['m'] = eye_wide + d['t']
        d['t'] = _dot(d['t'], expand(d['t']))
    for step in range(5):
        for d in st:
            if step < 4:
                both = _dot(d['t'], jnp.concatenate([expand(d['t']), expand(d['m'])], axis=1))
                d['t'] = both[:, 0:PAIR]
                d['m'] = d['m'] + both[:, PAIR:2 * PAIR]
            else:
                d['m'] = d['m'] + _dot(d['t'], expand(d['m']))
    for d in st:
        d['pq'] = _dot(d['m'], jnp.concatenate([d['a_e'], expand(d['takv'])], axis=1))

    h = [h_ref[p] for p in range(N_PAIRS)]
    for j in range(n_chunks):
        ds = st[j * N_PAIRS:(j + 1) * N_PAIRS]
        hb = [hp.astype(BF16) for hp in h]
        u_e = [expand(_dot(d['pq'][:, 0:PAIR], hb[p]) + d['pq'][:, PAIR:2 * PAIR])
               for p, d in enumerate(ds)]
        for p, d in enumerate(ds):
            o_ref[0, j * L:(j + 1) * L, p * PAIR:(p + 1) * PAIR] = _dot(
                d['r_all'], jnp.concatenate([hb[p], u_e[p], d['v_e']], axis=0))
        for p, d in enumerate(ds):
            h_add = _dot_tn(d['bk_h'], jnp.concatenate([u_e[p], d['v_e']], axis=0))
            w_col = jnp.transpose(jnp.broadcast_to(d['w_tot'], (PAIR, PAIR)))
            h[p] = h[p] * w_col + h_add
    for p in range(N_PAIRS):
        h_ref[p] = h[p]


def _rwkv_call(r, w, k, v, a, b):
    bsz, s, _ = r.shape
    spec = pl.BlockSpec((1, RWKV_TILE, D_RWKV), lambda bi, c: (bi, c, 0))
    return pl.pallas_call(
        _rwkv_kernel,
        grid=(bsz, s // RWKV_TILE),
        in_specs=[spec] * 6,
        out_specs=spec,
        out_shape=jax.ShapeDtypeStruct((bsz, s, D_RWKV), F32),
        scratch_shapes=[pltpu.VMEM((N_PAIRS, PAIR, PAIR), F32)],
        compiler_params=pltpu.CompilerParams(
            dimension_semantics=("parallel", "arbitrary"), vmem_limit_bytes=VMEM_LIMIT),
        name="rwkv",
    )(r, w, k, v, a, b)


def _layer_norm(x, g, b):
    mu = jnp.mean(x, axis=-1, keepdims=True)
    xc = x - mu
    var = jnp.mean(xc * xc, axis=-1, keepdims=True)
    return xc * lax.rsqrt(var + LN_EPS) * g + b


def _post_kernel(x_ref, pool_ref, o_ref, r_ref, k_ref, v_ref, g_ref,
                 lnxg_ref, lnxb_ref, rk_ref, ones_ref, wop_ref, wor_ref, ln1g_ref, ln1b_ref,
                 rw_hi_ref, rw_lo_ref, rb_ref, tri_ref,
                 h_o, meta_o, meta_t_o, cnt_o, base_ref):
    i = pl.program_id(0)
    ts = x_ref.shape[0]

    @pl.when(i == 0)
    def _():
        base_ref[...] = jnp.zeros_like(base_ref)

    ones = ones_ref[...]
    o = o_ref[...]
    mu = _head_sum(o, ones) * (1.0 / HEAD)
    oc = o - mu
    var = _head_sum(oc * oc, ones) * (1.0 / HEAD)
    y = oc * lax.rsqrt(var + LNX_EPS) * lnxg_ref[...] + lnxb_ref[...]
    v = v_ref[...]
    bonus = _head_sum(r_ref[...] * k_ref[...] * rk_ref[...], ones) * v
    y = (y + bonus) * g_ref[...]
    mixed = _dot(pool_ref[...], wop_ref[...]) + _dot(y, wor_ref[...])
    h = _layer_norm(ALPHA * x_ref[...] + mixed, ln1g_ref[...], ln1b_ref[...])
    h_o[...] = h

    h_hi = h.astype(BF16)
    h_lo = (h - h_hi.astype(F32)).astype(BF16)
    logits = (jnp.dot(h_hi, rw_hi_ref[...], preferred_element_type=F32)
              + jnp.dot(h_lo, rw_hi_ref[...], preferred_element_type=F32)
              + jnp.dot(h_hi, rw_lo_ref[...], preferred_element_type=F32)) + rb_ref[...]
    lane = lax.broadcasted_iota(jnp.int32, (ts, LANES), 1)
    neg = -jnp.inf
    gl = jnp.where(lane < N_GROUPS, logits, neg)
    gmax = jnp.max(gl, axis=-1, keepdims=True)
    g_idx = jnp.min(jnp.where(gl == gmax, lane, LANES), axis=-1, keepdims=True)
    g_top_p = 1.0 / jnp.sum(jnp.exp(gl - gmax), axis=-1, keepdims=True)
    lo_lane = EXPERT_LANE0 + g_idx * EXPERTS_PER_GROUP
    el = jnp.where((lane >= lo_lane) & (lane < lo_lane + EXPERTS_PER_GROUP), logits, neg)
    m1 = jnp.max(el, axis=-1, keepdims=True)
    i1 = jnp.min(jnp.where(el == m1, lane, LANES), axis=-1, keepdims=True)
    el2 = jnp.where(lane == i1, neg, el)
    m2 = jnp.max(el2, axis=-1, keepdims=True)
    i2 = jnp.min(jnp.where(el2 == m2, lane, LANES), axis=-1, keepdims=True)
    e21 = jnp.exp(m2 - m1)
    wgt1 = g_top_p / (1.0 + e21)
    wgt2 = g_top_p * e21 / (1.0 + e21)

    sel1 = lane == i1
    sel2 = lane == i2
    onehot = jnp.where(sel1 | sel2, 1.0, 0.0)
    before = jnp.dot(tri_ref[...], onehot.astype(BF16), preferred_element_type=F32)
    posn = base_ref[0:1, :] + before
    rank1 = jnp.sum(jnp.where(sel1, posn, 0.0), axis=-1, keepdims=True)
    rank2 = jnp.sum(jnp.where(sel2, posn, 0.0), axis=-1, keepdims=True)
    new_base = base_ref[0:1, :] + jnp.sum(onehot, axis=0, keepdims=True)
    base_ref[...] = jnp.broadcast_to(new_base, base_ref.shape)
    cnt_o[...] = jnp.broadcast_to(new_base, cnt_o.shape)

    e1 = (i1 - EXPERT_LANE0).astype(F32)
    e2 = (i2 - EXPERT_LANE0).astype(F32)
    meta = jnp.where(lane == 0, e1, jnp.where(lane == 1, e2, jnp.where(lane == 2, wgt1,
           jnp.where(lane == 3, wgt2, jnp.where(lane == 4, rank1, jnp.where(lane == 5, rank2, 0.0))))))
    meta_o[...] = meta
    meta_t_o[...] = jnp.transpose(meta)[0:8, :]


def _post_call(x2, pool2, o2, r2, k2, v2, g2, lnx_g, lnx_b, rk, ones_bd, wo_pool, wo_rwkv,
               ln1_g, ln1_b, rw_hi, rw_lo, rb, tri):
    n = x2.shape[0]
    full = lambda arr: pl.BlockSpec(arr.shape, lambda i: (0,) * arr.ndim)
    tok = lambda width: pl.BlockSpec((TS, width), lambda i: (i, 0))
    params = (lnx_g, lnx_b, rk, ones_bd, wo_pool, wo_rwkv, ln1_g, ln1_b, rw_hi, rw_lo, rb, tri)
    return pl.pallas_call(
        _post_kernel,
        grid=(n // TS,),
        in_specs=[tok(D_MODEL), tok(D_POOL)] + [tok(D_RWKV)] * 5 + [full(a) for a in params],
        out_specs=[tok(D_MODEL), tok(LANES), pl.BlockSpec((8, TS), lambda i: (0, i)),
                   pl.BlockSpec((8, LANES), lambda i: (0, 0))],
        out_shape=[jax.ShapeDtypeStruct((n, D_MODEL), F32), jax.ShapeDtypeStruct((n, LANES), F32),
                   jax.ShapeDtypeStruct((8, n), F32), jax.ShapeDtypeStruct((8, LANES), F32)],
        scratch_shapes=[pltpu.VMEM((8, LANES), F32)],
        compiler_params=pltpu.CompilerParams(
            dimension_semantics=("arbitrary",), vmem_limit_bytes=VMEM_LIMIT),
        name="post",
    )(x2, pool2, o2, r2, k2, v2, g2, *params)


def _invert_kernel(dest_ref, tok_ref):
    n = dest_ref.shape[0] // 2

    def clear(m, carry):
        tok_ref[m] = 0
        return carry

    def put(j, carry):
        tok_ref[dest_ref[j]] = j & (n - 1)
        return carry

    lax.fori_loop(0, tok_ref.shape[0], clear, 0, unroll=DMA_UNROLL)
    lax.fori_loop(0, 2 * n, put, 0, unroll=DMA_UNROLL)


def _invert_call(dest_flat, m_pad):
    assert (dest_flat.shape[0] // 2) & (dest_flat.shape[0] // 2 - 1) == 0
    return pl.pallas_call(
        _invert_kernel,
        in_specs=[pl.BlockSpec(memory_space=pltpu.SMEM)],
        out_specs=pl.BlockSpec(memory_space=pltpu.SMEM),
        out_shape=jax.ShapeDtypeStruct((m_pad,), jnp.int32),
        name="invert",
    )(dest_flat)


def _expert_kernel(blk_e_ref, n_used_ref, tok_ref, tok_next_ref, h_ref, wg_ref, wu_ref, wd_ref,
                   ys_ref, x_ref, sem):
    i = pl.program_id(0)
    n_used = n_used_ref[0]
    cur = lax.rem(i, 2)

    def gather(idx_ref, buf):
        def issue(t, carry):
            pltpu.make_async_copy(h_ref.at[pl.ds(idx_ref[0, 0, t], 1)],
                                  x_ref.at[buf, pl.ds(t, 1)], sem.at[buf]).start()
            return carry

        lax.fori_loop(0, BM, issue, 0, unroll=DMA_UNROLL)

    @pl.when(i == 0)
    def _():
        gather(tok_ref, 0)

    @pl.when(i + 1 < n_used)
    def _():
        gather(tok_next_ref, 1 - cur)

    @pl.when(i < n_used)
    def _():
        pltpu.make_async_copy(h_ref.at[pl.ds(0, BM)], x_ref.at[cur], sem.at[cur]).wait()
        xb = x_ref[cur].astype(BF16)
        gate = jnp.dot(xb, wg_ref[0].astype(BF16), preferred_element_type=F32)
        up = jnp.dot(xb, wu_ref[0].astype(BF16), preferred_element_type=F32)
        hid = gate * _sigmoid(gate) * up
        ys_ref[...] = _dot(hid, wd_ref[0])

    @pl.when(i >= n_used_ref[0])
    def _():
        ys_ref[...] = jnp.zeros_like(ys_ref)


def _expert_call(blk_e, n_used, slot_tok, h, exp_gate, exp_up, exp_down):
    n_blocks = slot_tok.shape[0]
    m_pad = n_blocks * BM
    tok_blk = lambda i, be, nu: (jnp.minimum(i, nu[0] - 1), 0, 0)
    tok_next_blk = lambda i, be, nu: (jnp.minimum(i + 1, nu[0] - 1), 0, 0)
    out_blk = lambda i, be, nu: (i, 0)
    w_blk = lambda i, be, nu: (be[jnp.minimum(i, nu[0] - 1)], 0, 0)
    grid_spec = pltpu.PrefetchScalarGridSpec(
        num_scalar_prefetch=2,
        grid=(n_blocks,),
        in_specs=[pl.BlockSpec((1, 1, BM), tok_blk, memory_space=pltpu.SMEM),
                  pl.BlockSpec((1, 1, BM), tok_next_blk, memory_space=pltpu.SMEM),
                  pl.BlockSpec(memory_space=pl.ANY),
                  pl.BlockSpec((1, D_MODEL, D_EXPERT), w_blk),
                  pl.BlockSpec((1, D_MODEL, D_EXPERT), w_blk),
                  pl.BlockSpec((1, D_EXPERT, D_MODEL), w_blk)],
        out_specs=pl.BlockSpec((BM, D_MODEL), out_blk),
        scratch_shapes=[pltpu.VMEM((2, BM, D_MODEL), F32), pltpu.SemaphoreType.DMA((2,))],
    )
    return pl.pallas_call(
        _expert_kernel,
        grid_spec=grid_spec,
        out_shape=jax.ShapeDtypeStruct((m_pad, D_MODEL), F32),
        compiler_params=pltpu.CompilerParams(
            dimension_semantics=("arbitrary",), vmem_limit_bytes=VMEM_LIMIT,
            disable_bounds_checks=True),
        name="experts",
    )(blk_e, n_used, slot_tok, slot_tok, h, exp_gate, exp_up, exp_down)


def _combine_kernel(dest_ref, dest_next_ref, h_ref, meta_ref, ys_ref, g_ref, b_ref, out_ref,
                    y_ref, sem):
    i = pl.program_id(0)
    n_steps = pl.num_programs(0)
    ts = h_ref.shape[0]

    def gather(idx_ref, buf):
        def issue(t, carry):
            for slot in range(2):
                pltpu.make_async_copy(ys_ref.at[pl.ds(idx_ref[0, 0, slot * ts + t], 1)],
                                      y_ref.at[buf, slot, pl.ds(t, 1)], sem.at[buf]).start()
            return carry

        lax.fori_loop(0, ts, issue, 0, unroll=DMA_UNROLL)

    cur = lax.rem(i, 2)

    @pl.when(i == 0)
    def _():
        gather(dest_ref, 0)

    @pl.when(i + 1 < n_steps)
    def _():
        gather(dest_next_ref, 1 - cur)

    for slot in range(2):
        pltpu.make_async_copy(ys_ref.at[pl.ds(0, ts)], y_ref.at[cur, slot], sem.at[cur]).wait()
    meta = meta_ref[...]
    ffn = y_ref[cur, 0] * meta[:, 2:3] + y_ref[cur, 1] * meta[:, 3:4]
    out_ref[...] = _layer_norm(ALPHA * h_ref[...] + ffn, g_ref[...], b_ref[...])


def _combine_call(dest, h, meta, ys, ln2_g, ln2_b):
    n = h.shape[0]
    full = lambda arr: pl.BlockSpec(arr.shape, lambda i: (0,) * arr.ndim)
    return pl.pallas_call(
        _combine_kernel,
        grid=(n // TS,),
        in_specs=[pl.BlockSpec((1, 1, 2 * TS), lambda i: (i, 0, 0), memory_space=pltpu.SMEM),
                  pl.BlockSpec((1, 1, 2 * TS), lambda i: (jnp.minimum(i + 1, n // TS - 1), 0, 0),
                               memory_space=pltpu.SMEM),
                  pl.BlockSpec((TS, D_MODEL), lambda i: (i, 0)),
                  pl.BlockSpec((TS, LANES), lambda i: (i, 0)),
                  pl.BlockSpec(memory_space=pl.ANY),
                  full(ln2_g), full(ln2_b)],
        out_specs=pl.BlockSpec((TS, D_MODEL), lambda i: (i, 0)),
        out_shape=jax.ShapeDtypeStruct((n, D_MODEL), F32),
        scratch_shapes=[pltpu.VMEM((2, 2, TS, D_MODEL), F32), pltpu.SemaphoreType.DMA((2,))],
        compiler_params=pltpu.CompilerParams(
            dimension_semantics=("arbitrary",), vmem_limit_bytes=VMEM_LIMIT,
            disable_bounds_checks=True),
        name="combine",
    )(dest, dest, h, meta, ys, ln2_g, ln2_b)


def _block_diag(blocks):
    g, c, d = blocks.shape
    eye = jnp.eye(g, dtype=blocks.dtype)
    return (eye[:, None, :, None] * blocks[:, :, None, :]).reshape(g * c, g * d)


def _layer(x, w_in, pool_w, pool_scale, mu_shift, w0, w_up, a0, a_up, g_up, k_k, k_a, r_k,
           lnx_g, lnx_b, w_out, ln1_g, ln1_b, router_group, router_group_b, router_expert,
           router_expert_b, exp_gate, exp_up, exp_down, ln2_g, ln2_b):
    b, s, d = x.shape
    n = b * s
    row2 = lambda t: t.reshape(1, -1)

    ones_bd = _block_diag(jnp.ones((2, HEAD, HEAD), BF16))
    lora_w = _block_diag(jnp.stack([w_up, a_up])).astype(BF16)
    poolw_bd = _block_diag(pool_w).astype(BF16)
    rw = jnp.zeros((D_MODEL, LANES), F32)
    rw = rw.at[:, 0:N_GROUPS].set(router_group).at[:, EXPERT_LANE0:EXPERT_LANE0 + N_EXPERTS].set(router_expert)
    rw_hi = rw.astype(BF16)
    rw_lo = (rw - rw_hi.astype(F32)).astype(BF16)
    rb = jnp.zeros((1, LANES), F32)
    rb = rb.at[0, 0:N_GROUPS].set(router_group_b).at[0, EXPERT_LANE0:EXPERT_LANE0 + N_EXPERTS].set(router_expert_b)
    tri = (lax.broadcasted_iota(jnp.int32, (TS, TS), 0) > lax.broadcasted_iota(jnp.int32, (TS, TS), 1)).astype(BF16)

    pool_out, r, w, k, v, a_vec, b_vec, g = _prep_call(
        x, w_in.astype(BF16), row2(mu_shift), row2(w0), row2(a0), lora_w, g_up.astype(BF16),
        row2(k_k), row2(k_a), poolw_bd, row2(pool_scale), ones_bd)
    o = _rwkv_call(r, w, k, v, a_vec, b_vec)

    flat = lambda t: t.reshape(n, t.shape[-1])
    w_out_bf = w_out.astype(BF16)
    h, meta, meta_t, counts = _post_call(
        flat(x), flat(pool_out), flat(o), flat(r), flat(k), flat(v), flat(g),
        row2(lnx_g), row2(lnx_b), row2(r_k), ones_bd, w_out_bf[0:D_POOL], w_out_bf[D_POOL:],
        row2(ln1_g), row2(ln1_b), rw_hi, rw_lo, rb, tri)

    cnt = counts[0, EXPERT_LANE0:EXPERT_LANE0 + N_EXPERTS].astype(jnp.int32)
    padded = (cnt + BM - 1) // BM * BM
    pend = jnp.cumsum(padded)
    pstart = pend - padded
    e_ids = meta_t[0:2].astype(jnp.int32)
    onehot = e_ids[:, :, None] == jnp.arange(N_EXPERTS, dtype=jnp.int32)
    dest2 = jnp.sum(jnp.where(onehot, pstart, 0), axis=-1) + meta_t[4:6].astype(jnp.int32)
    dest = dest2.reshape(2, n // TS, TS).transpose(1, 0, 2).reshape(n // TS, 1, 2 * TS)
    m_pad = 2 * n + N_EXPERTS * BM
    n_blocks = m_pad // BM
    blk_start = jnp.arange(n_blocks, dtype=jnp.int32) * BM
    blk_e = jnp.minimum(jnp.sum(blk_start[:, None] >= pend[None, :], axis=1), N_EXPERTS - 1).astype(jnp.int32)
    n_used = (pend[-1:] // BM).astype(jnp.int32)

    slot_tok = _invert_call(dest2.reshape(2 * n), m_pad).reshape(n_blocks, 1, BM)
    ys = _expert_call(blk_e, n_used, slot_tok, h, exp_gate, exp_up, exp_down)
    out = _combine_call(dest, h, meta, ys, row2(ln2_g), row2(ln2_b))
    return out.reshape(b, s, d)


def kernel(x, w_in, pool_w, pool_scale, mu_shift, w0, w_up, a0, a_up, g_up, k_k, k_a, r_k, lnx_g, lnx_b, w_out, ln1_g, ln1_b, router_group, router_group_b, router_expert, router_expert_b, exp_gate, exp_up, exp_down, ln2_g, ln2_b):
    depth = w_in.shape[0]
    for l in range(depth):
        x = _layer(x, w_in[l], pool_w[l], pool_scale[l], mu_shift[l], w0[l], w_up[l], a0[l],
                   a_up[l], g_up[l], k_k[l], k_a[l], r_k[l], lnx_g[l], lnx_b[l], w_out[l],
                   ln1_g[l], ln1_b[l], router_group[l], router_group_b[l], router_expert[l],
                   router_expert_b[l], exp_gate[l], exp_up[l], exp_down[l], ln2_g[l], ln2_b[l])
    return x
```

```python
import functools

import jax
import jax.numpy as jnp
from jax import lax
from jax.experimental import pallas as pl
from jax.experimental.pallas import tpu as pltpu

F32 = jnp.float32
BF16 = jnp.bfloat16

D_MODEL = 1024
D_POOL = 256
POOL_WINDOWS = (2, 4, 8, 16)
POOL_GROUP = 64
POOL_HALO = 16
D_RWKV = 768
HEAD = 64
D_DECAY_LORA = 64
D_AAA_LORA = 64
D_GATE_LORA = 128
D_RWKV_IN = 3 * D_RWKV + D_DECAY_LORA + D_AAA_LORA + D_GATE_LORA
D_IN = D_POOL + D_RWKV_IN
N_GROUPS = 4
EXPERTS_PER_GROUP = 8
N_EXPERTS = 32
D_EXPERT = 256
LN_EPS = 1e-5
LNX_EPS = 64e-5
ALPHA = 2.0 ** 0.25

LANES = 128
VMEM_LIMIT = 56 * 1024 * 1024

PAIR = 2 * HEAD
N_PAIRS = D_RWKV // PAIR
CHUNK = 64
RWKV_TILE = 4 * CHUNK
TS = 256
BM = 256
DMA_UNROLL = 8
EXPERT_LANE0 = 32


def _dot(a, b):
    return jnp.dot(a.astype(BF16), b.astype(BF16), preferred_element_type=F32)


def _dot_nt(a, b):
    return lax.dot_general(a.astype(BF16), b.astype(BF16), (((1,), (1,)), ((), ())),
                           preferred_element_type=F32)


def _dot_tn(a, b):
    return lax.dot_general(a.astype(BF16), b.astype(BF16), (((0,), (0,)), ((), ())),
                           preferred_element_type=F32)


def _dot_split(x, w_bf16):
    hi = x.astype(BF16)
    lo = (x - hi.astype(F32)).astype(BF16)
    return (jnp.dot(hi, w_bf16, preferred_element_type=F32)
            + jnp.dot(lo, w_bf16, preferred_element_type=F32))


def _head_sum(x, ones_bd):
    parts = [_dot_split(x[:, p * PAIR:(p + 1) * PAIR], ones_bd) for p in range(N_PAIRS)]
    return jnp.concatenate(parts, axis=1)


def _sigmoid(x):
    return 1.0 / (1.0 + jnp.exp(-x))


def _prep_kernel(x_ref, win_ref, mu_ref, w0_ref, a0_ref, lora_ref, gup_ref, kk_ref, ka_ref,
                 poolw_ref, pools_ref, ones_ref,
                 pool_o, r_o, w_o, k_o, v_o, a_o, b_o, g_o,
                 proj_ref, halo_ref):
    i = pl.program_id(1)
    ts = x_ref.shape[1]

    @pl.when(i == 0)
    def _():
        halo_ref[...] = jnp.zeros_like(halo_ref)

    proj_ref[...] = jnp.dot(x_ref[0].astype(BF16), win_ref[...], preferred_element_type=F32)

    row = lax.broadcasted_iota(jnp.int32, (ts, 1), 0)

    def shifted(off, width):
        z = proj_ref[:, off:off + width]
        prev = jnp.where(row == 0, halo_ref[POOL_HALO - 1:POOL_HALO, off:off + width],
                         pltpu.roll(z, 1, 0))
        return z + (prev - z) * mu_ref[:, off - D_POOL:off - D_POOL + width]

    p = proj_ref[:, 0:D_POOL]
    ext = jnp.concatenate([halo_ref[:, 0:D_POOL], p], axis=0)
    s2 = ext + pltpu.roll(ext, 1, 0)
    s4 = s2 + pltpu.roll(s2, 2, 0)
    s8 = s4 + pltpu.roll(s4, 4, 0)
    s16 = s8 + pltpu.roll(s8, 8, 0)
    lane = lax.broadcasted_iota(jnp.int32, (ts, D_POOL), 1)
    grp = lane // POOL_GROUP
    wsum = jnp.where(grp == 0, s2[POOL_HALO:], jnp.where(grp == 1, s4[POOL_HALO:],
                     jnp.where(grp == 2, s8[POOL_HALO:], s16[POOL_HALO:])))
    win = jnp.where(grp == 0, 2.0, jnp.where(grp == 1, 4.0, jnp.where(grp == 2, 8.0, 16.0)))
    pos = (i * ts + row + 1).astype(F32)
    diff = wsum / jnp.minimum(pos, win) - p
    pool_o[0] = _dot(diff, poolw_ref[...]) * pools_ref[...]

    o = D_POOL
    r = shifted(o, D_RWKV)
    k = shifted(o + D_RWKV, D_RWKV)
    v = shifted(o + 2 * D_RWKV, D_RWKV)
    lw = shifted(o + 3 * D_RWKV, D_DECAY_LORA + D_AAA_LORA)
    gd = shifted(o + 3 * D_RWKV + D_DECAY_LORA + D_AAA_LORA, D_GATE_LORA)

    lane128 = lax.broadcasted_iota(jnp.int32, (ts, LANES), 1)
    lora_in = jnp.where(lane128 < D_DECAY_LORA, jnp.tanh(lw), lw)
    lora = _dot(lora_in, lora_ref[...])
    wpre = -(w0_ref[...] + lora[:, 0:D_RWKV])
    softplus = jnp.maximum(wpre, 0.0) + jnp.log(1.0 + jnp.exp(-jnp.abs(wpre)))
    w_log = -softplus - 0.5
    w_o[0] = -jnp.exp(w_log)
    eta = _sigmoid(a0_ref[...] + lora[:, D_RWKV:2 * D_RWKV])
    g_o[0] = _dot(_sigmoid(gd), gup_ref[...])
    kk = k * kk_ref[...]
    ss = _head_sum(kk * kk, ones_ref[...])
    kkn = kk / jnp.maximum(jnp.sqrt(ss), 1e-12)
    r_o[0] = r
    v_o[0] = v
    k_o[0] = k * (1.0 + (eta - 1.0) * ka_ref[...])
    a_o[0] = -kkn
    b_o[0] = kkn * eta

    halo_ref[...] = proj_ref[ts - POOL_HALO:ts, :]


def _prep_call(x, win_bf, mu, w0, a0, lora_w, gup_bf, k_k, k_a, poolw_bd, pool_scale, ones_bd):
    b, s, _ = x.shape
    grid = (b, s // TS)
    full = lambda arr: pl.BlockSpec(arr.shape, lambda bi, i: (0,) * arr.ndim)
    tok = lambda width: pl.BlockSpec((1, TS, width), lambda bi, i: (bi, i, 0))
    outs = [jax.ShapeDtypeStruct((b, s, D_POOL), F32)] + [jax.ShapeDtypeStruct((b, s, D_RWKV), F32)] * 7
    params = (win_bf, mu, w0, a0, lora_w, gup_bf, k_k, k_a, poolw_bd, pool_scale, ones_bd)
    return pl.pallas_call(
        _prep_kernel,
        grid=grid,
        in_specs=[tok(D_MODEL)] + [full(a) for a in params],
        out_specs=[tok(D_POOL)] + [tok(D_RWKV)] * 7,
        out_shape=outs,
        scratch_shapes=[pltpu.VMEM((TS, D_IN), F32), pltpu.VMEM((POOL_HALO, D_IN), F32)],
        compiler_params=pltpu.CompilerParams(
            dimension_semantics=("parallel", "arbitrary"), vmem_limit_bytes=VMEM_LIMIT),
        name="prep",
    )(x, *params)


def _rwkv_kernel(r_ref, w_ref, k_ref, v_ref, a_ref, b_ref, o_ref, h_ref):
    c = pl.program_id(1)

    @pl.when(c == 0)
    def _():
        h_ref[...] = jnp.zeros_like(h_ref)

    L = CHUNK
    n_chunks = r_ref.shape[1] // L
    row = lax.broadcasted_iota(jnp.int32, (L, PAIR), 0)
    lane = lax.broadcasted_iota(jnp.int32, (L, PAIR), 1)
    head0 = lane < HEAD
    strict = row > (lane & (L - 1))
    incl = row >= (lane & (L - 1))
    eye_wide = jnp.where(row == (lane & (L - 1)), 1.0, 0.0)

    def expand(x):
        xb = x.astype(BF16)
        zero = jnp.zeros_like(xb)
        return jnp.concatenate([jnp.where(head0, xb, zero), jnp.where(head0, zero, xb)], axis=0)

    def tile(ref, j, p):
        return ref[0, j * L:(j + 1) * L, p * PAIR:(p + 1) * PAIR]

    streams = [(j, p) for j in range(n_chunks) for p in range(N_PAIRS)]
    st = []
    for j, p in streams:
        w = tile(w_ref, j, p)
        cum = w
        for sh in (1, 2, 4, 8, 16, 32):
            cum = cum + jnp.where(row >= sh, pltpu.roll(cum, sh, 0), 0.0)
        tot = cum[L - 1:L, :]
        e_neg = jnp.exp(-cum)
        e_rem = jnp.exp(tot - cum)
        b = tile(b_ref, j, p)
        k = tile(k_ref, j, p)
        a_n = tile(a_ref, j, p) * jnp.exp(cum - w)
        st.append(dict(
            ar=jnp.concatenate([a_n, tile(r_ref, j, p) * jnp.exp(cum)], axis=0).astype(BF16),
            a_e=expand(a_n),
            v_e=expand(tile(v_ref, j, p)),
            bk_t=jnp.concatenate([expand(b * e_neg), expand(k * e_neg)], axis=0),
            bk_h=jnp.concatenate([expand(b * e_rem), expand(k * e_rem)], axis=0),
            w_tot=jnp.exp(tot)))
    for d in st:
        sc = _dot_nt(d['ar'], d['bk_t'])
        d['t'] = jnp.where(strict, sc[0:L, 0:PAIR], 0.0)
        d['t_ak'] = jnp.where(strict, sc[0:L, PAIR:2 * PAIR], 0.0)
        d['r_all'] = jnp.concatenate([d['ar'][L:2 * L],
                                      jnp.where(incl, sc[L:2 * L, 0:PAIR], 0.0).astype(BF16),
                                      jnp.where(incl, sc[L:2 * L, PAIR:2 * PAIR], 0.0).astype(BF16)],
                                     axis=1)
    for d in st:
        d['takv'] = _dot(d['t_ak'], d['v_e'])
        d['m'] = eye_wide + d['t']
        d['t'] = _dot(d['t'], expand(d['t']))
    for step in range(5):
        for d in st:
            if step < 4:
                both = _dot(d['t'], jnp.concatenate([expand(d['t']), expand(d['m'])], axis=1))
                d['t'] = both[:, 0:PAIR]
                d['m'] = d['m'] + both[:, PAIR:2 * PAIR]
            else:
                d['m'] = d['m'] + _dot(d['t'], expand(d['m']))
    for d in st:
        d['pq'] = _dot(d['m'], jnp.concatenate([d['a_e'], expand(d['takv'])], axis=1))

    h = [h_ref[p] for p in range(N_PAIRS)]
    for j in range(n_chunks):
        ds = st[j * N_PAIRS:(j + 1) * N_PAIRS]
        hb = [hp.astype(BF16) for hp in h]
        u_e = [expand(_dot(d['pq'][:, 0:PAIR], hb[p]) + d['pq'][:, PAIR:2 * PAIR])
               for p, d in enumerate(ds)]
        for p, d in enumerate(ds):
            o_ref[0, j * L:(j + 1) * L, p * PAIR:(p + 1) * PAIR] = _dot(
                d['r_all'], jnp.concatenate([hb[p], u_e[p], d['v_e']], axis=0))
        for p, d in enumerate(ds):
            h_add = _dot_tn(d['bk_h'], jnp.concatenate([u_e[p], d['v_e']], axis=0))
            w_col = jnp.transpose(jnp.broadcast_to(d['w_tot'], (PAIR, PAIR)))
            h[p] = h[p] * w_col + h_add
    for p in range(N_PAIRS):
        h_ref[p] = h[p]


def _rwkv_call(r, w, k, v, a, b):
    bsz, s, _ = r.shape
    spec = pl.BlockSpec((1, RWKV_TILE, D_RWKV), lambda bi, c: (bi, c, 0))
    return pl.pallas_call(
        _rwkv_kernel,
        grid=(bsz, s // RWKV_TILE),
        in_specs=[spec] * 6,
        out_specs=spec,
        out_shape=jax.ShapeDtypeStruct((bsz, s, D_RWKV), F32),
        scratch_shapes=[pltpu.VMEM((N_PAIRS, PAIR, PAIR), F32)],
        compiler_params=pltpu.CompilerParams(
            dimension_semantics=("parallel", "arbitrary"), vmem_limit_bytes=VMEM_LIMIT),
        name="rwkv",
    )(r, w, k, v, a, b)


def _layer_norm(x, g, b):
    mu = jnp.mean(x, axis=-1, keepdims=True)
    xc = x - mu
    var = jnp.mean(xc * xc, axis=-1, keepdims=True)
    return xc * lax.rsqrt(var + LN_EPS) * g + b


def _post_kernel(x_ref, pool_ref, o_ref, r_ref, k_ref, v_ref, g_ref,
                 lnxg_ref, lnxb_ref, rk_ref, ones_ref, wop_ref, wor_ref, ln1g_ref, ln1b_ref,
                 rw_hi_ref, rw_lo_ref, rb_ref, tri_ref,
                 h_o, meta_o, meta_t_o, cnt_o, base_ref):
    i = pl.program_id(0)
    ts = x_ref.shape[0]

    @pl.when(i == 0)
    def _():
        base_ref[...] = jnp.zeros_like(base_ref)

    ones = ones_ref[...]
    o = o_ref[...]
    mu = _head_sum(o, ones) * (1.0 / HEAD)
    oc = o - mu
    var = _head_sum(oc * oc, ones) * (1.0 / HEAD)
    y = oc * lax.rsqrt(var + LNX_EPS) * lnxg_ref[...] + lnxb_ref[...]
    v = v_ref[...]
    bonus = _head_sum(r_ref[...] * k_ref[...] * rk_ref[...], ones) * v
    y = (y + bonus) * g_ref[...]
    mixed = _dot(pool_ref[...], wop_ref[...]) + _dot(y, wor_ref[...])
    h = _layer_norm(ALPHA * x_ref[...] + mixed, ln1g_ref[...], ln1b_ref[...])
    h_o[...] = h

    h_hi = h.astype(BF16)
    h_lo = (h - h_hi.astype(F32)).astype(BF16)
    logits = (jnp.dot(h_hi, rw_hi_ref[...], preferred_element_type=F32)
              + jnp.dot(h_lo, rw_hi_ref[...], preferred_element_type=F32)
              + jnp.dot(h_hi, rw_lo_ref[...], preferred_element_type=F32)) + rb_ref[...]
    lane = lax.broadcasted_iota(jnp.int32, (ts, LANES), 1)
    neg = -jnp.inf
    gl = jnp.where(lane < N_GROUPS, logits, neg)
    gmax = jnp.max(gl, axis=-1, keepdims=True)
    g_idx = jnp.min(jnp.where(gl == gmax, lane, LANES), axis=-1, keepdims=True)
    g_top_p = 1.0 / jnp.sum(jnp.exp(gl - gmax), axis=-1, keepdims=True)
    lo_lane = EXPERT_LANE0 + g_idx * EXPERTS_PER_GROUP
    el = jnp.where((lane >= lo_lane) & (lane < lo_lane + EXPERTS_PER_GROUP), logits, neg)
    m1 = jnp.max(el, axis=-1, keepdims=True)
    i1 = jnp.min(jnp.where(el == m1, lane, LANES), axis=-1, keepdims=True)
    el2 = jnp.where(lane == i1, neg, el)
    m2 = jnp.max(el2, axis=-1, keepdims=True)
    i2 = jnp.min(jnp.where(el2 == m2, lane, LANES), axis=-1, keepdims=True)
    e21 = jnp.exp(m2 - m1)
    wgt1 = g_top_p / (1.0 + e21)
    wgt2 = g_top_p * e21 / (1.0 + e21)

    sel1 = lane == i1
    sel2 = lane == i2
    onehot = jnp.where(sel1 | sel2, 1.0, 0.0)
    before = jnp.dot(tri_ref[...], onehot.astype(BF16), preferred_element_type=F32)
    posn = base_ref[0:1, :] + before
    rank1 = jnp.sum(jnp.where(sel1, posn, 0.0), axis=-1, keepdims=True)
    rank2 = jnp.sum(jnp.where(sel2, posn, 0.0), axis=-1, keepdims=True)
    new_base = base_ref[0:1, :] + jnp.sum(onehot, axis=0, keepdims=True)
    base_ref[...] = jnp.broadcast_to(new_base, base_ref.shape)
    cnt_o[...] = jnp.broadcast_to(new_base, cnt_o.shape)

    e1 = (i1 - EXPERT_LANE0).astype(F32)
    e2 = (i2 - EXPERT_LANE0).astype(F32)
    meta = jnp.where(lane == 0, e1, jnp.where(lane == 1, e2, jnp.where(lane == 2, wgt1,
           jnp.where(lane == 3, wgt2, jnp.where(lane == 4, rank1, jnp.where(lane == 5, rank2, 0.0))))))
    meta_o[...] = meta
    meta_t_o[...] = jnp.transpose(meta)[0:8, :]


def _post_call(x2, pool2, o2, r2, k2, v2, g2, lnx_g, lnx_b, rk, ones_bd, wo_pool, wo_rwkv,
               ln1_g, ln1_b, rw_hi, rw_lo, rb, tri):
    n = x2.shape[0]
    full = lambda arr: pl.BlockSpec(arr.shape, lambda i: (0,) * arr.ndim)
    tok = lambda width: pl.BlockSpec((TS, width), lambda i: (i, 0))
    params = (lnx_g, lnx_b, rk, ones_bd, wo_pool, wo_rwkv, ln1_g, ln1_b, rw_hi, rw_lo, rb, tri)
    return pl.pallas_call(
        _post_kernel,
        grid=(n // TS,),
        in_specs=[tok(D_MODEL), tok(D_POOL)] + [tok(D_RWKV)] * 5 + [full(a) for a in params],
        out_specs=[tok(D_MODEL), tok(LANES), pl.BlockSpec((8, TS), lambda i: (0, i)),
                   pl.BlockSpec((8, LANES), lambda i: (0, 0))],
        out_shape=[jax.ShapeDtypeStruct((n, D_MODEL), F32), jax.ShapeDtypeStruct((n, LANES), F32),
                   jax.ShapeDtypeStruct((8, n), F32), jax.ShapeDtypeStruct((8, LANES), F32)],
        scratch_shapes=[pltpu.VMEM((8, LANES), F32)],
        compiler_params=pltpu.CompilerParams(
            dimension_semantics=("arbitrary",), vmem_limit_bytes=VMEM_LIMIT),
        name="post",
    )(x2, pool2, o2, r2, k2, v2, g2, *params)


def _invert_kernel(pad_start_ref, pad_cnt_ref, dest_ref, tok_ref):
    n = dest_ref.shape[0] // 2

    def clear_range(e, carry):
        start = pad_start_ref[e]

        def one(m, c):
            tok_ref[start + m] = 0
            return c

        return lax.fori_loop(0, pad_cnt_ref[e], one, carry)

    def put(j, carry):
        tok_ref[dest_ref[j]] = j & (n - 1)
        return carry

    lax.fori_loop(0, pad_start_ref.shape[0], clear_range, 0)
    lax.fori_loop(0, 2 * n, put, 0, unroll=DMA_UNROLL)


def _invert_call(pad_start, pad_cnt, dest_flat, m_pad):
    assert (dest_flat.shape[0] // 2) & (dest_flat.shape[0] // 2 - 1) == 0
    smem = pl.BlockSpec(memory_space=pltpu.SMEM)
    return pl.pallas_call(
        _invert_kernel,
        in_specs=[smem, smem, smem],
        out_specs=smem,
        out_shape=jax.ShapeDtypeStruct((m_pad,), jnp.int32),
        name="invert",
    )(pad_start, pad_cnt, dest_flat)


def _expert_kernel(blk_e_ref, n_used_ref, tok_ref, tok_next_ref, h_ref, wg_ref, wu_ref, wd_ref,
                   ys_ref, xa_ref, xb_ref, sem):
    i = pl.program_id(0)
    n_used = n_used_ref[0]
    bufs = (xa_ref, xb_ref)

    def wait_block(b):
        pltpu.make_async_copy(h_ref.at[pl.ds(0, BM)], bufs[b], sem.at[b]).wait()

    def start_row(idx_ref, t, b):
        pltpu.make_async_copy(h_ref.at[pl.ds(idx_ref[0, 0, t], 1)], bufs[b].at[pl.ds(t, 1)],
                              sem.at[b]).start()

    @pl.when(i == 0)
    def _():
        lax.fori_loop(0, BM, lambda t, c: (start_row(tok_ref, t, 0), c)[1], 0, unroll=DMA_UNROLL)

    for b in range(2):
        @pl.when((i < n_used) & (lax.rem(i, 2) == b))
        def _(b=b):
            wait_block(b)
            for t in range(BM):
                start_row(tok_next_ref, t, 1 - b)
            xb = bufs[b][...].astype(BF16)
            gate = jnp.dot(xb, wg_ref[0].astype(BF16), preferred_element_type=F32)
            up = jnp.dot(xb, wu_ref[0].astype(BF16), preferred_element_type=F32)
            hid = gate * _sigmoid(gate) * up
            ys_ref[...] = _dot(hid, wd_ref[0])

        @pl.when((i == n_used - 1) & (lax.rem(i, 2) == b))
        def _(b=b):
            wait_block(1 - b)

    @pl.when(i >= n_used_ref[0])
    def _():
        ys_ref[...] = jnp.zeros_like(ys_ref)


def _expert_call(blk_e, n_used, slot_tok, h, exp_gate, exp_up, exp_down):
    n_blocks = slot_tok.shape[0]
    m_pad = n_blocks * BM
    tok_blk = lambda i, be, nu: (jnp.minimum(i, nu[0] - 1), 0, 0)
    tok_next_blk = lambda i, be, nu: (jnp.minimum(i + 1, nu[0] - 1), 0, 0)
    out_blk = lambda i, be, nu: (i, 0)
    w_blk = lambda i, be, nu: (be[jnp.minimum(i, nu[0] - 1)], 0, 0)
    grid_spec = pltpu.PrefetchScalarGridSpec(
        num_scalar_prefetch=2,
        grid=(n_blocks,),
        in_specs=[pl.BlockSpec((1, 1, BM), tok_blk, memory_space=pltpu.SMEM),
                  pl.BlockSpec((1, 1, BM), tok_next_blk, memory_space=pltpu.SMEM),
                  pl.BlockSpec(memory_space=pl.ANY),
                  pl.BlockSpec((1, D_MODEL, D_EXPERT), w_blk),
                  pl.BlockSpec((1, D_MODEL, D_EXPERT), w_blk),
                  pl.BlockSpec((1, D_EXPERT, D_MODEL), w_blk)],
        out_specs=pl.BlockSpec((BM, D_MODEL), out_blk),
        scratch_shapes=[pltpu.VMEM((BM, D_MODEL), F32), pltpu.VMEM((BM, D_MODEL), F32),
                        pltpu.SemaphoreType.DMA((2,))],
    )
    return pl.pallas_call(
        _expert_kernel,
        grid_spec=grid_spec,
        out_shape=jax.ShapeDtypeStruct((m_pad, D_MODEL), F32),
        compiler_params=pltpu.CompilerParams(
            dimension_semantics=("arbitrary",), vmem_limit_bytes=VMEM_LIMIT,
            disable_bounds_checks=True),
        name="experts",
    )(blk_e, n_used, slot_tok, slot_tok, h, exp_gate, exp_up, exp_down)


def _combine_kernel(dest_ref, dest_next_ref, h_ref, meta_ref, ys_ref, g_ref, b_ref, out_ref,
                    y_ref, sem):
    i = pl.program_id(0)
    n_steps = pl.num_programs(0)
    ts = h_ref.shape[0]

    def gather(idx_ref, buf):
        def issue(t, carry):
            for slot in range(2):
                pltpu.make_async_copy(ys_ref.at[pl.ds(idx_ref[0, 0, slot * ts + t], 1)],
                                      y_ref.at[buf, slot, pl.ds(t, 1)], sem.at[buf]).start()
            return carry

        lax.fori_loop(0, ts, issue, 0, unroll=DMA_UNROLL)

    cur = lax.rem(i, 2)

    @pl.when(i == 0)
    def _():
        gather(dest_ref, 0)

    @pl.when(i + 1 < n_steps)
    def _():
        gather(dest_next_ref, 1 - cur)

    for slot in range(2):
        pltpu.make_async_copy(ys_ref.at[pl.ds(0, ts)], y_ref.at[cur, slot], sem.at[cur]).wait()
    meta = meta_ref[...]
    ffn = y_ref[cur, 0] * meta[:, 2:3] + y_ref[cur, 1] * meta[:, 3:4]
    out_ref[...] = _layer_norm(ALPHA * h_ref[...] + ffn, g_ref[...], b_ref[...])


def _combine_call(dest, h, meta, ys, ln2_g, ln2_b):
    n = h.shape[0]
    full = lambda arr: pl.BlockSpec(arr.shape, lambda i: (0,) * arr.ndim)
    return pl.pallas_call(
        _combine_kernel,
        grid=(n // TS,),
        in_specs=[pl.BlockSpec((1, 1, 2 * TS), lambda i: (i, 0, 0), memory_space=pltpu.SMEM),
                  pl.BlockSpec((1, 1, 2 * TS), lambda i: (jnp.minimum(i + 1, n // TS - 1), 0, 0),
                               memory_space=pltpu.SMEM),
                  pl.BlockSpec((TS, D_MODEL), lambda i: (i, 0)),
                  pl.BlockSpec((TS, LANES), lambda i: (i, 0)),
                  pl.BlockSpec(memory_space=pl.ANY),
                  full(ln2_g), full(ln2_b)],
        out_specs=pl.BlockSpec((TS, D_MODEL), lambda i: (i, 0)),
        out_shape=jax.ShapeDtypeStruct((n, D_MODEL), F32),
        scratch_shapes=[pltpu.VMEM((2, 2, TS, D_MODEL), F32), pltpu.SemaphoreType.DMA((2,))],
        compiler_params=pltpu.CompilerParams(
            dimension_semantics=("arbitrary",), vmem_limit_bytes=VMEM_LIMIT,
            disable_bounds_checks=True),
        name="combine",
    )(dest, dest, h, meta, ys, ln2_g, ln2_b)


def _block_diag(blocks):
    g, c, d = blocks.shape
    eye = jnp.eye(g, dtype=blocks.dtype)
    return (eye[:, None, :, None] * blocks[:, :, None, :]).reshape(g * c, g * d)


def _layer(x, w_in, pool_w, pool_scale, mu_shift, w0, w_up, a0, a_up, g_up, k_k, k_a, r_k,
           lnx_g, lnx_b, w_out, ln1_g, ln1_b, router_group, router_group_b, router_expert,
           router_expert_b, exp_gate, exp_up, exp_down, ln2_g, ln2_b):
    b, s, d = x.shape
    n = b * s
    row2 = lambda t: t.reshape(1, -1)

    ones_bd = _block_diag(jnp.ones((2, HEAD, HEAD), BF16))
    lora_w = _block_diag(jnp.stack([w_up, a_up])).astype(BF16)
    poolw_bd = _block_diag(pool_w).astype(BF16)
    rw = jnp.zeros((D_MODEL, LANES), F32)
    rw = rw.at[:, 0:N_GROUPS].set(router_group).at[:, EXPERT_LANE0:EXPERT_LANE0 + N_EXPERTS].set(router_expert)
    rw_hi = rw.astype(BF16)
    rw_lo = (rw - rw_hi.astype(F32)).astype(BF16)
    rb = jnp.zeros((1, LANES), F32)
    rb = rb.at[0, 0:N_GROUPS].set(router_group_b).at[0, EXPERT_LANE0:EXPERT_LANE0 + N_EXPERTS].set(router_expert_b)
    tri = (lax.broadcasted_iota(jnp.int32, (TS, TS), 0) > lax.broadcasted_iota(jnp.int32, (TS, TS), 1)).astype(BF16)

    pool_out, r, w, k, v, a_vec, b_vec, g = _prep_call(
        x, w_in.astype(BF16), row2(mu_shift), row2(w0), row2(a0), lora_w, g_up.astype(BF16),
        row2(k_k), row2(k_a), poolw_bd, row2(pool_scale), ones_bd)
    o = _rwkv_call(r, w, k, v, a_vec, b_vec)

    flat = lambda t: t.reshape(n, t.shape[-1])
    w_out_bf = w_out.astype(BF16)
    h, meta, meta_t, counts = _post_call(
        flat(x), flat(pool_out), flat(o), flat(r), flat(k), flat(v), flat(g),
        row2(lnx_g), row2(lnx_b), row2(r_k), ones_bd, w_out_bf[0:D_POOL], w_out_bf[D_POOL:],
        row2(ln1_g), row2(ln1_b), rw_hi, rw_lo, rb, tri)

    cnt = counts[0, EXPERT_LANE0:EXPERT_LANE0 + N_EXPERTS].astype(jnp.int32)
    padded = (cnt + BM - 1) // BM * BM
    pend = jnp.cumsum(padded)
    pstart = pend - padded
    e_ids = meta_t[0:2].astype(jnp.int32)
    onehot = e_ids[:, :, None] == jnp.arange(N_EXPERTS, dtype=jnp.int32)
    dest2 = jnp.sum(jnp.where(onehot, pstart, 0), axis=-1) + meta_t[4:6].astype(jnp.int32)
    dest = dest2.reshape(2, n // TS, TS).transpose(1, 0, 2).reshape(n // TS, 1, 2 * TS)
    m_pad = 2 * n + N_EXPERTS * BM
    n_blocks = m_pad // BM
    blk_start = jnp.arange(n_blocks, dtype=jnp.int32) * BM
    blk_e = jnp.minimum(jnp.sum(blk_start[:, None] >= pend[None, :], axis=1), N_EXPERTS - 1).astype(jnp.int32)
    n_used = (pend[-1:] // BM).astype(jnp.int32)

    free_start = jnp.concatenate([pstart + cnt, pend[-1:]])
    free_cnt = jnp.concatenate([padded - cnt, m_pad - pend[-1:]])
    slot_tok = _invert_call(free_start, free_cnt, dest2.reshape(2 * n), m_pad)
    slot_tok = slot_tok.reshape(n_blocks, 1, BM)
    ys = _expert_call(blk_e, n_used, slot_tok, h, exp_gate, exp_up, exp_down)
    out = _combine_call(dest, h, meta, ys, row2(ln2_g), row2(ln2_b))
    return out.reshape(b, s, d)


def kernel(x, w_in, pool_w, pool_scale, mu_shift, w0, w_up, a0, a_up, g_up, k_k, k_a, r_k, lnx_g, lnx_b, w_out, ln1_g, ln1_b, router_group, router_group_b, router_expert, router_expert_b, exp_gate, exp_up, exp_down, ln2_g, ln2_b):
    depth = w_in.shape[0]
    for l in range(depth):
        x = _layer(x, w_in[l], pool_w[l], pool_scale[l], mu_shift[l], w0[l], w_up[l], a0[l],
                   a_up[l], g_up[l], k_k[l], k_a[l], r_k[l], lnx_g[l], lnx_b[l], w_out[l],
                   ln1_g[l], ln1_b[l], router_group[l], router_group_b[l], router_expert[l],
                   router_expert_b[l], exp_gate[l], exp_up[l], exp_down[l], ln2_g[l], ln2_b[l])
    return x
```

```python
import functools

import jax
import jax.numpy as jnp
from jax import lax
from jax.experimental import pallas as pl
from jax.experimental.pallas import tpu as pltpu

F32 = jnp.float32
BF16 = jnp.bfloat16

D_MODEL = 1024
D_POOL = 256
POOL_WINDOWS = (2, 4, 8, 16)
POOL_GROUP = 64
POOL_HALO = 16
D_RWKV = 768
HEAD = 64
D_DECAY_LORA = 64
D_AAA_LORA = 64
D_GATE_LORA = 128
D_RWKV_IN = 3 * D_RWKV + D_DECAY_LORA + D_AAA_LORA + D_GATE_LORA
D_IN = D_POOL + D_RWKV_IN
N_GROUPS = 4
EXPERTS_PER_GROUP = 8
N_EXPERTS = 32
D_EXPERT = 256
LN_EPS = 1e-5
LNX_EPS = 64e-5
ALPHA = 2.0 ** 0.25

LANES = 128
VMEM_LIMIT = 56 * 1024 * 1024

PAIR = 2 * HEAD
N_PAIRS = D_RWKV // PAIR
CHUNK = 64
RWKV_TILE = 4 * CHUNK
TS = 256
BM = 256
DMA_UNROLL = 8
EXPERT_LANE0 = 32


def _dot(a, b):
    return jnp.dot(a.astype(BF16), b.astype(BF16), preferred_element_type=F32)


def _dot_nt(a, b):
    return lax.dot_general(a.astype(BF16), b.astype(BF16), (((1,), (1,)), ((), ())),
                           preferred_element_type=F32)


def _dot_tn(a, b):
    return lax.dot_general(a.astype(BF16), b.astype(BF16), (((0,), (0,)), ((), ())),
                           preferred_element_type=F32)


def _dot_split(x, w_bf16):
    hi = x.astype(BF16)
    lo = (x - hi.astype(F32)).astype(BF16)
    return (jnp.dot(hi, w_bf16, preferred_element_type=F32)
            + jnp.dot(lo, w_bf16, preferred_element_type=F32))


def _head_sum(x, ones_bd):
    parts = [_dot_split(x[:, p * PAIR:(p + 1) * PAIR], ones_bd) for p in range(N_PAIRS)]
    return jnp.concatenate(parts, axis=1)


def _sigmoid(x):
    return 1.0 / (1.0 + jnp.exp(-x))


def _prep_kernel(x_ref, win_ref, mu_ref, w0_ref, a0_ref, lora_ref, gup_ref, kk_ref, ka_ref,
                 poolw_ref, pools_ref, ones_ref,
                 pool_o, r_o, w_o, k_o, v_o, a_o, b_o, g_o,
                 proj_ref, halo_ref):
    i = pl.program_id(1)
    ts = x_ref.shape[1]

    @pl.when(i == 0)
    def _():
        halo_ref[...] = jnp.zeros_like(halo_ref)

    proj_ref[...] = jnp.dot(x_ref[0].astype(BF16), win_ref[...], preferred_element_type=F32)

    row = lax.broadcasted_iota(jnp.int32, (ts, 1), 0)

    def shifted(off, width):
        z = proj_ref[:, off:off + width]
        prev = jnp.where(row == 0, halo_ref[POOL_HALO - 1:POOL_HALO, off:off + width],
                         pltpu.roll(z, 1, 0))
        return z + (prev - z) * mu_ref[:, off - D_POOL:off - D_POOL + width]

    p = proj_ref[:, 0:D_POOL]
    ext = jnp.concatenate([halo_ref[:, 0:D_POOL], p], axis=0)
    s2 = ext + pltpu.roll(ext, 1, 0)
    s4 = s2 + pltpu.roll(s2, 2, 0)
    s8 = s4 + pltpu.roll(s4, 4, 0)
    s16 = s8 + pltpu.roll(s8, 8, 0)
    lane = lax.broadcasted_iota(jnp.int32, (ts, D_POOL), 1)
    grp = lane // POOL_GROUP
    wsum = jnp.where(grp == 0, s2[POOL_HALO:], jnp.where(grp == 1, s4[POOL_HALO:],
                     jnp.where(grp == 2, s8[POOL_HALO:], s16[POOL_HALO:])))
    win = jnp.where(grp == 0, 2.0, jnp.where(grp == 1, 4.0, jnp.where(grp == 2, 8.0, 16.0)))
    pos = (i * ts + row + 1).astype(F32)
    diff = wsum / jnp.minimum(pos, win) - p
    pool_o[0] = _dot(diff, poolw_ref[...]) * pools_ref[...]

    o = D_POOL
    r = shifted(o, D_RWKV)
    k = shifted(o + D_RWKV, D_RWKV)
    v = shifted(o + 2 * D_RWKV, D_RWKV)
    lw = shifted(o + 3 * D_RWKV, D_DECAY_LORA + D_AAA_LORA)
    gd = shifted(o + 3 * D_RWKV + D_DECAY_LORA + D_AAA_LORA, D_GATE_LORA)

    lane128 = lax.broadcasted_iota(jnp.int32, (ts, LANES), 1)
    lora_in = jnp.where(lane128 < D_DECAY_LORA, jnp.tanh(lw), lw)
    lora = _dot(lora_in, lora_ref[...])
    wpre = -(w0_ref[...] + lora[:, 0:D_RWKV])
    softplus = jnp.maximum(wpre, 0.0) + jnp.log(1.0 + jnp.exp(-jnp.abs(wpre)))
    w_log = -softplus - 0.5
    w_o[0] = -jnp.exp(w_log)
    eta = _sigmoid(a0_ref[...] + lora[:, D_RWKV:2 * D_RWKV])
    g_o[0] = _dot(_sigmoid(gd), gup_ref[...])
    kk = k * kk_ref[...]
    ss = _head_sum(kk * kk, ones_ref[...])
    kkn = kk / jnp.maximum(jnp.sqrt(ss), 1e-12)
    r_o[0] = r
    v_o[0] = v
    k_o[0] = k * (1.0 + (eta - 1.0) * ka_ref[...])
    a_o[0] = -kkn
    b_o[0] = kkn * eta

    halo_ref[...] = proj_ref[ts - POOL_HALO:ts, :]


def _prep_call(x, win_bf, mu, w0, a0, lora_w, gup_bf, k_k, k_a, poolw_bd, pool_scale, ones_bd):
    b, s, _ = x.shape
    grid = (b, s // TS)
    full = lambda arr: pl.BlockSpec(arr.shape, lambda bi, i: (0,) * arr.ndim)
    tok = lambda width: pl.BlockSpec((1, TS, width), lambda bi, i: (bi, i, 0))
    outs = [jax.ShapeDtypeStruct((b, s, D_POOL), F32)] + [jax.ShapeDtypeStruct((b, s, D_RWKV), F32)] * 7
    params = (win_bf, mu, w0, a0, lora_w, gup_bf, k_k, k_a, poolw_bd, pool_scale, ones_bd)
    return pl.pallas_call(
        _prep_kernel,
        grid=grid,
        in_specs=[tok(D_MODEL)] + [full(a) for a in params],
        out_specs=[tok(D_POOL)] + [tok(D_RWKV)] * 7,
        out_shape=outs,
        scratch_shapes=[pltpu.VMEM((TS, D_IN), F32), pltpu.VMEM((POOL_HALO, D_IN), F32)],
        compiler_params=pltpu.CompilerParams(
            dimension_semantics=("parallel", "arbitrary"), vmem_limit_bytes=VMEM_LIMIT),
        name="prep",
    )(x, *params)


def _rwkv_kernel(r_ref, w_ref, k_ref, v_ref, a_ref, b_ref, o_ref, h_ref):
    c = pl.program_id(1)

    @pl.when(c == 0)
    def _():
        h_ref[...] = jnp.zeros_like(h_ref)

    L = CHUNK
    n_chunks = r_ref.shape[1] // L
    row = lax.broadcasted_iota(jnp.int32, (L, PAIR), 0)
    lane = lax.broadcasted_iota(jnp.int32, (L, PAIR), 1)
    head0 = lane < HEAD
    strict = row > (lane & (L - 1))
    incl = row >= (lane & (L - 1))
    eye_wide = jnp.where(row == (lane & (L - 1)), 1.0, 0.0)

    def expand(x):
        xb = x.astype(BF16)
        zero = jnp.zeros_like(xb)
        return jnp.concatenate([jnp.where(head0, xb, zero), jnp.where(head0, zero, xb)], axis=0)

    def tile(ref, j, p):
        return ref[0, j * L:(j + 1) * L, p * PAIR:(p + 1) * PAIR]

    streams = [(j, p) for j in range(n_chunks) for p in range(N_PAIRS)]
    st = []
    for j, p in streams:
        w = tile(w_ref, j, p)
        cum = w
        for sh in (1, 2, 4, 8, 16, 32):
            cum = cum + jnp.where(row >= sh, pltpu.roll(cum, sh, 0), 0.0)
        tot = cum[L - 1:L, :]
        e_neg = jnp.exp(-cum)
        e_rem = jnp.exp(tot - cum)
        b = tile(b_ref, j, p)
        k = tile(k_ref, j, p)
        a_n = tile(a_ref, j, p) * jnp.exp(cum - w)
        st.append(dict(
            ar=jnp.concatenate([a_n, tile(r_ref, j, p) * jnp.exp(cum)], axis=0).astype(BF16),
            a_e=expand(a_n),
            v_e=expand(tile(v_ref, j, p)),
            bk_t=jnp.concatenate([expand(b * e_neg), expand(k * e_neg)], axis=0),
            bk_h=jnp.concatenate([expand(b * e_rem), expand(k * e_rem)], axis=0),
            w_tot=jnp.exp(tot)))
    for d in st:
        sc = _dot_nt(d['ar'], d['bk_t'])
        d['t'] = jnp.where(strict, sc[0:L, 0:PAIR], 0.0)
        d['t_ak'] = jnp.where(strict, sc[0:L, PAIR:2 * PAIR], 0.0)
        d['r_all'] = jnp.concatenate([d['ar'][L:2 * L],
                                      jnp.where(incl, sc[L:2 * L, 0:PAIR], 0.0).astype(BF16),
                                      jnp.where(incl, sc[L:2 * L, PAIR:2 * PAIR], 0.0).astype(BF16)],
                                     axis=1)
    for d in st:
        d['takv'] = _dot(d['t_ak'], d['v_e'])
        d['m'] = eye_wide + d['t']
        d['t'] = _dot(d['t'], expand(d['t']))
    for step in range(5):
        for d in st:
            if step < 4:
                both = _dot(d['t'], jnp.concatenate([expand(d['t']), expand(d['m'])], axis=1))
                d['t'] = both[:, 0:PAIR]
                d['m'] = d['m'] + both[:, PAIR:2 * PAIR]
            else:
                d['m'] = d['m'] + _dot(d['t'], expand(d['m']))
    for d in st:
        d['pq'] = _dot(d['m'], jnp.concatenate([d['a_e'], expand(d['takv'])], axis=1))

    h = [h_ref[p] for p in range(N_PAIRS)]
    for j in range(n_chunks):
        ds = st[j * N_PAIRS:(j + 1) * N_PAIRS]
        hb = [hp.astype(BF16) for hp in h]
        u_e = [expand(_dot(d['pq'][:, 0:PAIR], hb[p]) + d['pq'][:, PAIR:2 * PAIR])
               for p, d in enumerate(ds)]
        for p, d in enumerate(ds):
            o_ref[0, j * L:(j + 1) * L, p * PAIR:(p + 1) * PAIR] = _dot(
                d['r_all'], jnp.concatenate([hb[p], u_e[p], d['v_e']], axis=0))
        for p, d in enumerate(ds):
            h_add = _dot_tn(d['bk_h'], jnp.concatenate([u_e[p], d['v_e']], axis=0))
            w_col = jnp.transpose(jnp.broadcast_to(d['w_tot'], (PAIR, PAIR)))
            h[p] = h[p] * w_col + h_add
    for p in range(N_PAIRS):
        h_ref[p] = h[p]


def _rwkv_call(r, w, k, v, a, b):
    bsz, s, _ = r.shape
    spec = pl.BlockSpec((1, RWKV_TILE, D_RWKV), lambda bi, c: (bi, c, 0))
    return pl.pallas_call(
        _rwkv_kernel,
        grid=(bsz, s // RWKV_TILE),
        in_specs=[spec] * 6,
        out_specs=spec,
        out_shape=jax.ShapeDtypeStruct((bsz, s, D_RWKV), F32),
        scratch_shapes=[pltpu.VMEM((N_PAIRS, PAIR, PAIR), F32)],
        compiler_params=pltpu.CompilerParams(
            dimension_semantics=("parallel", "arbitrary"), vmem_limit_bytes=VMEM_LIMIT),
        name="rwkv",
    )(r, w, k, v, a, b)


def _layer_norm(x, g, b):
    mu = jnp.mean(x, axis=-1, keepdims=True)
    xc = x - mu
    var = jnp.mean(xc * xc, axis=-1, keepdims=True)
    return xc * lax.rsqrt(var + LN_EPS) * g + b


def _post_kernel(x_ref, pool_ref, o_ref, r_ref, k_ref, v_ref, g_ref,
                 lnxg_ref, lnxb_ref, rk_ref, ones_ref, wop_ref, wor_ref, ln1g_ref, ln1b_ref,
                 rw_hi_ref, rw_lo_ref, rb_ref, tri_ref,
                 h_o, meta_o, meta_t_o, cnt_o, base_ref):
    i = pl.program_id(0)
    ts = x_ref.shape[0]

    @pl.when(i == 0)
    def _():
        base_ref[...] = jnp.zeros_like(base_ref)

    ones = ones_ref[...]
    o = o_ref[...]
    mu = _head_sum(o, ones) * (1.0 / HEAD)
    oc = o - mu
    var = _head_sum(oc * oc, ones) * (1.0 / HEAD)
    y = oc * lax.rsqrt(var + LNX_EPS) * lnxg_ref[...] + lnxb_ref[...]
    v = v_ref[...]
    bonus = _head_sum(r_ref[...] * k_ref[...] * rk_ref[...], ones) * v
    y = (y + bonus) * g_ref[...]
    mixed = _dot(pool_ref[...], wop_ref[...]) + _dot(y, wor_ref[...])
    h = _layer_norm(ALPHA * x_ref[...] + mixed, ln1g_ref[...], ln1b_ref[...])
    h_o[...] = h

    h_hi = h.astype(BF16)
    h_lo = (h - h_hi.astype(F32)).astype(BF16)
    logits = (jnp.dot(h_hi, rw_hi_ref[...], preferred_element_type=F32)
              + jnp.dot(h_lo, rw_hi_ref[...], preferred_element_type=F32)
              + jnp.dot(h_hi, rw_lo_ref[...], preferred_element_type=F32)) + rb_ref[...]
    lane = lax.broadcasted_iota(jnp.int32, (ts, LANES), 1)
    neg = -jnp.inf
    gl = jnp.where(lane < N_GROUPS, logits, neg)
    gmax = jnp.max(gl, axis=-1, keepdims=True)
    g_idx = jnp.min(jnp.where(gl == gmax, lane, LANES), axis=-1, keepdims=True)
    g_top_p = 1.0 / jnp.sum(jnp.exp(gl - gmax), axis=-1, keepdims=True)
    lo_lane = EXPERT_LANE0 + g_idx * EXPERTS_PER_GROUP
    el = jnp.where((lane >= lo_lane) & (lane < lo_lane + EXPERTS_PER_GROUP), logits, neg)
    m1 = jnp.max(el, axis=-1, keepdims=True)
    i1 = jnp.min(jnp.where(el == m1, lane, LANES), axis=-1, keepdims=True)
    el2 = jnp.where(lane == i1, neg, el)
    m2 = jnp.max(el2, axis=-1, keepdims=True)
    i2 = jnp.min(jnp.where(el2 == m2, lane, LANES), axis=-1, keepdims=True)
    e21 = jnp.exp(m2 - m1)
    wgt1 = g_top_p / (1.0 + e21)
    wgt2 = g_top_p * e21 / (1.0 + e21)

    sel1 = lane == i1
    sel2 = lane == i2
    onehot = jnp.where(sel1 | sel2, 1.0, 0.0)
    before = jnp.dot(tri_ref[...], onehot.astype(BF16), preferred_element_type=F32)
    posn = base_ref[0:1, :] + before
    rank1 = jnp.sum(jnp.where(sel1, posn, 0.0), axis=-1, keepdims=True)
    rank2 = jnp.sum(jnp.where(sel2, posn, 0.0), axis=-1, keepdims=True)
    new_base = base_ref[0:1, :] + jnp.sum(onehot, axis=0, keepdims=True)
    base_ref[...] = jnp.broadcast_to(new_base, base_ref.shape)
    cnt_o[...] = jnp.broadcast_to(new_base, cnt_o.shape)

    e1 = (i1 - EXPERT_LANE0).astype(F32)
    e2 = (i2 - EXPERT_LANE0).astype(F32)
    meta = jnp.where(lane == 0, e1, jnp.where(lane == 1, e2, jnp.where(lane == 2, wgt1,
           jnp.where(lane == 3, wgt2, jnp.where(lane == 4, rank1, jnp.where(lane == 5, rank2, 0.0))))))
    meta_o[...] = meta
    meta_t_o[...] = jnp.transpose(meta)[0:8, :]


def _post_call(x2, pool2, o2, r2, k2, v2, g2, lnx_g, lnx_b, rk, ones_bd, wo_pool, wo_rwkv,
               ln1_g, ln1_b, rw_hi, rw_lo, rb, tri):
    n = x2.shape[0]
    full = lambda arr: pl.BlockSpec(arr.shape, lambda i: (0,) * arr.ndim)
    tok = lambda width: pl.BlockSpec((TS, width), lambda i: (i, 0))
    params = (lnx_g, lnx_b, rk, ones_bd, wo_pool, wo_rwkv, ln1_g, ln1_b, rw_hi, rw_lo, rb, tri)
    return pl.pallas_call(
        _post_kernel,
        grid=(n // TS,),
        in_specs=[tok(D_MODEL), tok(D_POOL)] + [tok(D_RWKV)] * 5 + [full(a) for a in params],
        out_specs=[tok(D_MODEL), tok(LANES), pl.BlockSpec((8, TS), lambda i: (0, i)),
                   pl.BlockSpec((8, LANES), lambda i: (0, 0))],
        out_shape=[jax.ShapeDtypeStruct((n, D_MODEL), F32), jax.ShapeDtypeStruct((n, LANES), F32),
                   jax.ShapeDtypeStruct((8, n), F32), jax.ShapeDtypeStruct((8, LANES), F32)],
        scratch_shapes=[pltpu.VMEM((8, LANES), F32)],
        compiler_params=pltpu.CompilerParams(
            dimension_semantics=("arbitrary",), vmem_limit_bytes=VMEM_LIMIT),
        name="post",
    )(x2, pool2, o2, r2, k2, v2, g2, *params)


def _start_rows(ts, row_copy):
    def issue(g, carry):
        for u in range(DMA_UNROLL):
            for slot in range(2):
                row_copy(g * DMA_UNROLL + u, slot).start(priority=(2 * u + slot) % 2)
        return carry

    lax.fori_loop(0, ts // DMA_UNROLL, issue, 0)


def _dispatch_kernel(pad_start_ref, pad_cnt_ref, n_used_ref, dest_ref, h_ref, xs_ref, zero_ref,
                     sem, zsem):
    i = pl.program_id(0)
    ts = h_ref.shape[0]
    n_blocks = xs_ref.shape[0] // BM

    def row_copy(t, slot):
        return pltpu.make_async_copy(h_ref.at[pl.ds(t, 1)],
                                     xs_ref.at[pl.ds(dest_ref[0, 0, slot * ts + t], 1)], sem)

    def zero_row_copy(row):
        return pltpu.make_async_copy(zero_ref.at[pl.ds(0, 1)], xs_ref.at[pl.ds(row, 1)], zsem)

    def zero_block_copy(blk):
        return pltpu.make_async_copy(zero_ref, xs_ref.at[pl.ds(pl.multiple_of(blk * BM, BM), BM)], zsem)

    @pl.when(i == 0)
    def _():
        zero_ref[...] = jnp.zeros_like(zero_ref)

        def fill(e, carry):
            start = pad_start_ref[e]

            def one(j, c):
                zero_row_copy(start + j).start()
                return c

            return lax.fori_loop(0, pad_cnt_ref[e], one, carry)

        def fill_wait(e, carry):
            def one(j, c):
                zero_row_copy(0).wait()
                return c

            return lax.fori_loop(0, pad_cnt_ref[e], one, carry)

        def tail(blk, carry):
            zero_block_copy(blk).start()
            return carry

        def tail_wait(blk, carry):
            zero_block_copy(blk).wait()
            return carry

        lax.fori_loop(0, N_EXPERTS, fill, 0)
        lax.fori_loop(n_used_ref[0], n_blocks, tail, 0)
        lax.fori_loop(0, N_EXPERTS, fill_wait, 0)
        lax.fori_loop(n_used_ref[0], n_blocks, tail_wait, 0)

    _start_rows(ts, row_copy)
    for _ in range(2):
        pltpu.make_async_copy(h_ref, xs_ref.at[pl.ds(0, ts)], sem).wait()


def _dispatch_call(pad_start, pad_cnt, n_used, dest, h, m_pad):
    n = h.shape[0]
    grid_spec = pltpu.PrefetchScalarGridSpec(
        num_scalar_prefetch=3,
        grid=(n // TS,),
        in_specs=[pl.BlockSpec((1, 1, 2 * TS), lambda i, *_: (i, 0, 0), memory_space=pltpu.SMEM),
                  pl.BlockSpec((TS, D_MODEL), lambda i, *_: (i, 0))],
        out_specs=pl.BlockSpec(memory_space=pl.ANY),
        scratch_shapes=[pltpu.VMEM((BM, D_MODEL), F32), pltpu.SemaphoreType.DMA(()),
                        pltpu.SemaphoreType.DMA(())],
    )
    return pl.pallas_call(
        _dispatch_kernel,
        grid_spec=grid_spec,
        out_shape=jax.ShapeDtypeStruct((m_pad, D_MODEL), F32),
        compiler_params=pltpu.CompilerParams(
            dimension_semantics=("arbitrary",), vmem_limit_bytes=VMEM_LIMIT,
            disable_bounds_checks=True),
        name="dispatch",
    )(pad_start, pad_cnt, n_used, dest, h)


def _expert_kernel(blk_e_ref, n_used_ref, xs_ref, wg_ref, wu_ref, wd_ref, ys_ref):
    i = pl.program_id(0)

    @pl.when(i < n_used_ref[0])
    def _():
        xb = xs_ref[...].astype(BF16)
        gate = jnp.dot(xb, wg_ref[0].astype(BF16), preferred_element_type=F32)
        up = jnp.dot(xb, wu_ref[0].astype(BF16), preferred_element_type=F32)
        hid = gate * _sigmoid(gate) * up
        ys_ref[...] = _dot(hid, wd_ref[0])

    @pl.when(i >= n_used_ref[0])
    def _():
        ys_ref[...] = jnp.zeros_like(ys_ref)


def _expert_call(blk_e, n_used, xs, exp_gate, exp_up, exp_down):
    m_pad = xs.shape[0]
    n_blocks = m_pad // BM
    row_blk = lambda i, be, nu: (jnp.minimum(i, nu[0] - 1), 0)
    out_blk = lambda i, be, nu: (i, 0)
    w_blk = lambda i, be, nu: (be[jnp.minimum(i, nu[0] - 1)], 0, 0)
    grid_spec = pltpu.PrefetchScalarGridSpec(
        num_scalar_prefetch=2,
        grid=(n_blocks,),
        in_specs=[pl.BlockSpec((BM, D_MODEL), row_blk),
                  pl.BlockSpec((1, D_MODEL, D_EXPERT), w_blk),
                  pl.BlockSpec((1, D_MODEL, D_EXPERT), w_blk),
                  pl.BlockSpec((1, D_EXPERT, D_MODEL), w_blk)],
        out_specs=pl.BlockSpec((BM, D_MODEL), out_blk),
    )
    return pl.pallas_call(
        _expert_kernel,
        grid_spec=grid_spec,
        out_shape=jax.ShapeDtypeStruct((m_pad, D_MODEL), F32),
        compiler_params=pltpu.CompilerParams(
            dimension_semantics=("arbitrary",), vmem_limit_bytes=VMEM_LIMIT),
        name="experts",
    )(blk_e, n_used, xs, exp_gate, exp_up, exp_down)


def _combine_kernel(dest_ref, dest_next_ref, h_ref, meta_ref, ys_ref, g_ref, b_ref, out_ref,
                    y_ref, sem):
    i = pl.program_id(0)
    n_steps = pl.num_programs(0)
    ts = h_ref.shape[0]

    def gather(idx_ref, buf):
        _start_rows(ts, lambda t, slot: pltpu.make_async_copy(
            ys_ref.at[pl.ds(idx_ref[0, 0, slot * ts + t], 1)],
            y_ref.at[buf, slot, pl.ds(t, 1)], sem.at[buf]))

    cur = lax.rem(i, 2)

    @pl.when(i == 0)
    def _():
        gather(dest_ref, 0)

    @pl.when(i + 1 < n_steps)
    def _():
        gather(dest_next_ref, 1 - cur)

    for slot in range(2):
        pltpu.make_async_copy(ys_ref.at[pl.ds(0, ts)], y_ref.at[cur, slot], sem.at[cur]).wait()
    meta = meta_ref[...]
    ffn = y_ref[cur, 0] * meta[:, 2:3] + y_ref[cur, 1] * meta[:, 3:4]
    out_ref[...] = _layer_norm(ALPHA * h_ref[...] + ffn, g_ref[...], b_ref[...])


def _combine_call(dest, h, meta, ys, ln2_g, ln2_b):
    n = h.shape[0]
    full = lambda arr: pl.BlockSpec(arr.shape, lambda i: (0,) * arr.ndim)
    return pl.pallas_call(
        _combine_kernel,
        grid=(n // TS,),
        in_specs=[pl.BlockSpec((1, 1, 2 * TS), lambda i: (i, 0, 0), memory_space=pltpu.SMEM),
                  pl.BlockSpec((1, 1, 2 * TS), lambda i: (jnp.minimum(i + 1, n // TS - 1), 0, 0),
                               memory_space=pltpu.SMEM),
                  pl.BlockSpec((TS, D_MODEL), lambda i: (i, 0)),
                  pl.BlockSpec((TS, LANES), lambda i: (i, 0)),
                  pl.BlockSpec(memory_space=pl.ANY),
                  full(ln2_g), full(ln2_b)],
        out_specs=pl.BlockSpec((TS, D_MODEL), lambda i: (i, 0)),
        out_shape=jax.ShapeDtypeStruct((n, D_MODEL), F32),
        scratch_shapes=[pltpu.VMEM((2, 2, TS, D_MODEL), F32), pltpu.SemaphoreType.DMA((2,))],
        compiler_params=pltpu.CompilerParams(
            dimension_semantics=("arbitrary",), vmem_limit_bytes=VMEM_LIMIT,
            disable_bounds_checks=True),
        name="combine",
    )(dest, dest, h, meta, ys, ln2_g, ln2_b)


def _block_diag(blocks):
    g, c, d = blocks.shape
    eye = jnp.eye(g, dtype=blocks.dtype)
    return (eye[:, None, :, None] * blocks[:, :, None, :]).reshape(g * c, g * d)


def _layer(x, w_in, pool_w, pool_scale, mu_shift, w0, w_up, a0, a_up, g_up, k_k, k_a, r_k,
           lnx_g, lnx_b, w_out, ln1_g, ln1_b, router_group, router_group_b, router_expert,
           router_expert_b, exp_gate, exp_up, exp_down, ln2_g, ln2_b):
    b, s, d = x.shape
    n = b * s
    row2 = lambda t: t.reshape(1, -1)

    ones_bd = _block_diag(jnp.ones((2, HEAD, HEAD), BF16))
    lora_w = _block_diag(jnp.stack([w_up, a_up])).astype(BF16)
    poolw_bd = _block_diag(pool_w).astype(BF16)
    rw = jnp.zeros((D_MODEL, LANES), F32)
    rw = rw.at[:, 0:N_GROUPS].set(router_group).at[:, EXPERT_LANE0:EXPERT_LANE0 + N_EXPERTS].set(router_expert)
    rw_hi = rw.astype(BF16)
    rw_lo = (rw - rw_hi.astype(F32)).astype(BF16)
    rb = jnp.zeros((1, LANES), F32)
    rb = rb.at[0, 0:N_GROUPS].set(router_group_b).at[0, EXPERT_LANE0:EXPERT_LANE0 + N_EXPERTS].set(router_expert_b)
    tri = (lax.broadcasted_iota(jnp.int32, (TS, TS), 0) > lax.broadcasted_iota(jnp.int32, (TS, TS), 1)).astype(BF16)

    pool_out, r, w, k, v, a_vec, b_vec, g = _prep_call(
        x, w_in.astype(BF16), row2(mu_shift), row2(w0), row2(a0), lora_w, g_up.astype(BF16),
        row2(k_k), row2(k_a), poolw_bd, row2(pool_scale), ones_bd)
    o = _rwkv_call(r, w, k, v, a_vec, b_vec)

    flat = lambda t: t.reshape(n, t.shape[-1])
    w_out_bf = w_out.astype(BF16)
    h, meta, meta_t, counts = _post_call(
        flat(x), flat(pool_out), flat(o), flat(r), flat(k), flat(v), flat(g),
        row2(lnx_g), row2(lnx_b), row2(r_k), ones_bd, w_out_bf[0:D_POOL], w_out_bf[D_POOL:],
        row2(ln1_g), row2(ln1_b), rw_hi, rw_lo, rb, tri)

    cnt = counts[0, EXPERT_LANE0:EXPERT_LANE0 + N_EXPERTS].astype(jnp.int32)
    padded = (cnt + BM - 1) // BM * BM
    pend = jnp.cumsum(padded)
    pstart = pend - padded
    e_ids = meta_t[0:2].astype(jnp.int32)
    onehot = e_ids[:, :, None] == jnp.arange(N_EXPERTS, dtype=jnp.int32)
    dest = jnp.sum(jnp.where(onehot, pstart, 0), axis=-1) + meta_t[4:6].astype(jnp.int32)
    dest = dest.reshape(2, n // TS, TS).transpose(1, 0, 2).reshape(n // TS, 1, 2 * TS)
    m_pad = 2 * n + N_EXPERTS * BM
    n_blocks = m_pad // BM
    blk_start = jnp.arange(n_blocks, dtype=jnp.int32) * BM
    blk_e = jnp.minimum(jnp.sum(blk_start[:, None] >= pend[None, :], axis=1), N_EXPERTS - 1).astype(jnp.int32)
    n_used = (pend[-1:] // BM).astype(jnp.int32)

    xs = _dispatch_call(pstart + cnt, padded - cnt, n_used, dest, h, m_pad)
    ys = _expert_call(blk_e, n_used, xs, exp_gate, exp_up, exp_down)
    out = _combine_call(dest, h, meta, ys, row2(ln2_g), row2(ln2_b))
    return out.reshape(b, s, d)


def kernel(x, w_in, pool_w, pool_scale, mu_shift, w0, w_up, a0, a_up, g_up, k_k, k_a, r_k, lnx_g, lnx_b, w_out, ln1_g, ln1_b, router_group, router_group_b, router_expert, router_expert_b, exp_gate, exp_up, exp_down, ln2_g, ln2_b):
    depth = w_in.shape[0]
    for l in range(depth):
        x = _layer(x, w_in[l], pool_w[l], pool_scale[l], mu_shift[l], w0[l], w_up[l], a0[l],
                   a_up[l], g_up[l], k_k[l], k_a[l], r_k[l], lnx_g[l], lnx_b[l], w_out[l],
                   ln1_g[l], ln1_b[l], router_group[l], router_group_b[l], router_expert[l],
                   router_expert_b[l], exp_gate[l], exp_up[l], exp_down[l], ln2_g[l], ln2_b[l])
    return x
```

```python
import functools

import jax
import jax.numpy as jnp
from jax import lax
from jax.experimental import pallas as pl
from jax.experimental.pallas import tpu as pltpu

F32 = jnp.float32
BF16 = jnp.bfloat16

D_MODEL = 1024
D_POOL = 256
POOL_WINDOWS = (2, 4, 8, 16)
POOL_GROUP = 64
POOL_HALO = 16
D_RWKV = 768
HEAD = 64
D_DECAY_LORA = 64
D_AAA_LORA = 64
D_GATE_LORA = 128
D_RWKV_IN = 3 * D_RWKV + D_DECAY_LORA + D_AAA_LORA + D_GATE_LORA
D_IN = D_POOL + D_RWKV_IN
N_GROUPS = 4
EXPERTS_PER_GROUP = 8
N_EXPERTS = 32
D_EXPERT = 256
LN_EPS = 1e-5
LNX_EPS = 64e-5
ALPHA = 2.0 ** 0.25
DECAY_SCALE = 0.6065306597126334
KK_NORM_FLOOR = 1e-24

LANES = 128
VMEM_LIMIT = 56 * 1024 * 1024

PAIR = 2 * HEAD
N_PAIRS = D_RWKV // PAIR
CHUNK = 64
RWKV_TILE = 4 * CHUNK
TS = 512
BM = 256
DMA_UNROLL = 8
EXPERT_LANE0 = 32


def _dot(a, b):
    return jnp.dot(a.astype(BF16), b.astype(BF16), preferred_element_type=F32)


def _dot_nt(a, b):
    return lax.dot_general(a.astype(BF16), b.astype(BF16), (((1,), (1,)), ((), ())),
                           preferred_element_type=F32)


def _dot_tn(a, b):
    return lax.dot_general(a.astype(BF16), b.astype(BF16), (((0,), (0,)), ((), ())),
                           preferred_element_type=F32)


def _head_sum(x, ones_bd):
    parts = [_dot(x[:, p * PAIR:(p + 1) * PAIR], ones_bd) for p in range(N_PAIRS)]
    return jnp.concatenate(parts, axis=1)


def _sigmoid(x):
    return 1.0 / (1.0 + jnp.exp(-x))


def _prep_kernel(x_ref, win_ref, mu_ref, w0_ref, a0_ref, lora_ref, gup_ref, kk_ref, ka_ref,
                 poolw_ref, pools_ref, ones_ref,
                 pool_o, r_o, w_o, k_o, v_o, a_o, b_o, g_o,
                 proj_ref, halo_ref):
    i = pl.program_id(1)
    ts = x_ref.shape[1]

    @pl.when(i == 0)
    def _():
        halo_ref[...] = jnp.zeros_like(halo_ref)

    proj_ref[...] = jnp.dot(x_ref[0].astype(BF16), win_ref[...], preferred_element_type=F32)

    row = lax.broadcasted_iota(jnp.int32, (ts, 1), 0)

    def shifted(off, width):
        z = proj_ref[:, off:off + width]
        prev = jnp.where(row == 0, halo_ref[POOL_HALO - 1:POOL_HALO, off:off + width],
                         pltpu.roll(z, 1, 0))
        return z + (prev - z) * mu_ref[:, off - D_POOL:off - D_POOL + width]

    p = proj_ref[:, 0:D_POOL]
    ext = jnp.concatenate([halo_ref[:, 0:D_POOL], p], axis=0)
    s2 = ext + pltpu.roll(ext, 1, 0)
    s4 = s2 + pltpu.roll(s2, 2, 0)
    s8 = s4 + pltpu.roll(s4, 4, 0)
    s16 = s8 + pltpu.roll(s8, 8, 0)
    lane = lax.broadcasted_iota(jnp.int32, (ts, D_POOL), 1)
    grp = lane // POOL_GROUP
    wsum = jnp.where(grp == 0, s2[POOL_HALO:], jnp.where(grp == 1, s4[POOL_HALO:],
                     jnp.where(grp == 2, s8[POOL_HALO:], s16[POOL_HALO:])))
    win = jnp.where(grp == 0, 2.0, jnp.where(grp == 1, 4.0, jnp.where(grp == 2, 8.0, 16.0)))
    pos = (i * ts + row + 1).astype(F32)
    diff = wsum / jnp.minimum(pos, win) - p
    pool_o[0] = _dot(diff, poolw_ref[...]) * pools_ref[...]

    o = D_POOL
    r = shifted(o, D_RWKV)
    k = shifted(o + D_RWKV, D_RWKV)
    v = shifted(o + 2 * D_RWKV, D_RWKV)
    lw = shifted(o + 3 * D_RWKV, D_DECAY_LORA + D_AAA_LORA)
    gd = shifted(o + 3 * D_RWKV + D_DECAY_LORA + D_AAA_LORA, D_GATE_LORA)

    lane128 = lax.broadcasted_iota(jnp.int32, (ts, LANES), 1)
    lora_in = jnp.where(lane128 < D_DECAY_LORA, jnp.tanh(lw), lw)
    lora = _dot(lora_in, lora_ref[...])
    w_o[0] = -DECAY_SCALE * _sigmoid(w0_ref[...] + lora[:, 0:D_RWKV])
    eta = _sigmoid(a0_ref[...] + lora[:, D_RWKV:2 * D_RWKV])
    g_o[0] = _dot(_sigmoid(gd), gup_ref[...])
    kk = k * kk_ref[...]
    ss = _head_sum(kk * kk, ones_ref[...])
    kkn = kk * lax.rsqrt(jnp.maximum(ss, KK_NORM_FLOOR))
    r_o[0] = r
    v_o[0] = v
    k_o[0] = k * (1.0 + (eta - 1.0) * ka_ref[...])
    a_o[0] = -kkn
    b_o[0] = kkn * eta

    halo_ref[...] = proj_ref[ts - POOL_HALO:ts, :]


def _prep_call(x, win_bf, mu, w0, a0, lora_w, gup_bf, k_k, k_a, poolw_bd, pool_scale, ones_bd):
    b, s, _ = x.shape
    grid = (b, s // TS)
    full = lambda arr: pl.BlockSpec(arr.shape, lambda bi, i: (0,) * arr.ndim)
    tok = lambda width: pl.BlockSpec((1, TS, width), lambda bi, i: (bi, i, 0))
    outs = [jax.ShapeDtypeStruct((b, s, D_POOL), F32)] + [jax.ShapeDtypeStruct((b, s, D_RWKV), F32)] * 7
    params = (win_bf, mu, w0, a0, lora_w, gup_bf, k_k, k_a, poolw_bd, pool_scale, ones_bd)
    return pl.pallas_call(
        _prep_kernel,
        grid=grid,
        in_specs=[tok(D_MODEL)] + [full(a) for a in params],
        out_specs=[tok(D_POOL)] + [tok(D_RWKV)] * 7,
        out_shape=outs,
        scratch_shapes=[pltpu.VMEM((TS, D_IN), F32), pltpu.VMEM((POOL_HALO, D_IN), F32)],
        compiler_params=pltpu.CompilerParams(
            dimension_semantics=("parallel", "arbitrary"), vmem_limit_bytes=VMEM_LIMIT),
        name="prep",
    )(x, *params)


def _rwkv_kernel(r_ref, w_ref, k_ref, v_ref, a_ref, b_ref, o_ref, h_ref):
    c = pl.program_id(1)

    @pl.when(c == 0)
    def _():
        h_ref[...] = jnp.zeros_like(h_ref)

    L = CHUNK
    n_chunks = r_ref.shape[1] // L
    row = lax.broadcasted_iota(jnp.int32, (L, PAIR), 0)
    lane = lax.broadcasted_iota(jnp.int32, (L, PAIR), 1)
    head0 = lane < HEAD
    strict = row > (lane & (L - 1))
    incl = row >= (lane & (L - 1))
    eye_wide = jnp.where(row == (lane & (L - 1)), 1.0, 0.0)

    def expand(x):
        xb = x.astype(BF16)
        zero = jnp.zeros_like(xb)
        return jnp.concatenate([jnp.where(head0, xb, zero), jnp.where(head0, zero, xb)], axis=0)

    def tile(ref, j, p):
        return ref[0, j * L:(j + 1) * L, p * PAIR:(p + 1) * PAIR]

    streams = [(j, p) for j in range(n_chunks) for p in range(N_PAIRS)]
    st = []
    for j, p in streams:
        w = tile(w_ref, j, p)
        cum = w
        for sh in (1, 2, 4, 8, 16, 32):
            cum = cum + jnp.where(row >= sh, pltpu.roll(cum, sh, 0), 0.0)
        tot = cum[L - 1:L, :]
        e_neg = jnp.exp(-cum)
        e_rem = jnp.exp(tot - cum)
        b = tile(b_ref, j, p)
        k = tile(k_ref, j, p)
        a_n = tile(a_ref, j, p) * jnp.exp(cum - w)
        st.append(dict(
            ar=jnp.concatenate([a_n, tile(r_ref, j, p) * jnp.exp(cum)], axis=0).astype(BF16),
            a_e=expand(a_n),
            v_e=expand(tile(v_ref, j, p)),
            bk_t=jnp.concatenate([expand(b * e_neg), expand(k * e_neg)], axis=0),
            bk_h=jnp.concatenate([expand(b * e_rem), expand(k * e_rem)], axis=0),
            w_tot=jnp.exp(tot)))
    for d in st:
        sc = _dot_nt(d['ar'], d['bk_t'])
        d['t'] = jnp.where(strict, sc[0:L, 0:PAIR], 0.0)
        d['t_ak'] = jnp.where(strict, sc[0:L, PAIR:2 * PAIR], 0.0)
        d['r_all'] = jnp.concatenate([d['ar'][L:2 * L],
                                      jnp.where(incl, sc[L:2 * L, 0:PAIR], 0.0).astype(BF16),
                                      jnp.where(incl, sc[L:2 * L, PAIR:2 * PAIR], 0.0).astype(BF16)],
                                     axis=1)
    for d in st:
        d['takv'] = _dot(d['t_ak'], d['v_e'])
        d['m'] = eye_wide + d['t']
        d['t'] = _dot(d['t'], expand(d['t']))
    for step in range(5):
        for d in st:
            if step < 4:
                both = _dot(d['t'], jnp.concatenate([expand(d['t']), expand(d['m'])], axis=1))
                d['t'] = both[:, 0:PAIR]
                d['m'] = d['m'] + both[:, PAIR:2 * PAIR]
            else:
                d['m'] = d['m'] + _dot(d['t'], expand(d['m']))
    for d in st:
        d['pq'] = _dot(d['m'], jnp.concatenate([d['a_e'], expand(d['takv'])], axis=1))

    h = [h_ref[p] for p in range(N_PAIRS)]
    for j in range(n_chunks):
        ds = st[j * N_PAIRS:(j + 1) * N_PAIRS]
        hb = [hp.astype(BF16) for hp in h]
        u_e = [expand(_dot(d['pq'][:, 0:PAIR], hb[p]) + d['pq'][:, PAIR:2 * PAIR])
               for p, d in enumerate(ds)]
        for p, d in enumerate(ds):
            o_ref[0, j * L:(j + 1) * L, p * PAIR:(p + 1) * PAIR] = _dot(
                d['r_all'], jnp.concatenate([hb[p], u_e[p], d['v_e']], axis=0))
        for p, d in enumerate(ds):
            h_add = _dot_tn(d['bk_h'], jnp.concatenate([u_e[p], d['v_e']], axis=0))
            w_col = jnp.transpose(jnp.broadcast_to(d['w_tot'], (PAIR, PAIR)))
            h[p] = h[p] * w_col + h_add
    for p in range(N_PAIRS):
        h_ref[p] = h[p]


def _rwkv_call(r, w, k, v, a, b):
    bsz, s, _ = r.shape
    spec = pl.BlockSpec((1, RWKV_TILE, D_RWKV), lambda bi, c: (bi, c, 0))
    return pl.pallas_call(
        _rwkv_kernel,
        grid=(bsz, s // RWKV_TILE),
        in_specs=[spec] * 6,
        out_specs=spec,
        out_shape=jax.ShapeDtypeStruct((bsz, s, D_RWKV), F32),
        scratch_shapes=[pltpu.VMEM((N_PAIRS, PAIR, PAIR), F32)],
        compiler_params=pltpu.CompilerParams(
            dimension_semantics=("parallel", "arbitrary"), vmem_limit_bytes=VMEM_LIMIT),
        name="rwkv",
    )(r, w, k, v, a, b)


def _layer_norm(x, g, b):
    mu = jnp.mean(x, axis=-1, keepdims=True)
    xc = x - mu
    var = jnp.mean(xc * xc, axis=-1, keepdims=True)
    return xc * lax.rsqrt(var + LN_EPS) * g + b


def _post_kernel(x_ref, pool_ref, o_ref, r_ref, k_ref, v_ref, g_ref,
                 lnxg_ref, lnxb_ref, rk_ref, ones_ref, wop_ref, wor_ref, ln1g_ref, ln1b_ref,
                 rw_hi_ref, rw_lo_ref, rb_ref, tri_ref,
                 h_o, meta_o, meta_t_o, cnt_o, base_ref):
    i = pl.program_id(0)
    ts = x_ref.shape[0]

    @pl.when(i == 0)
    def _():
        base_ref[...] = jnp.zeros_like(base_ref)

    ones = ones_ref[...]
    o = o_ref[...]
    mu = _head_sum(o, ones) * (1.0 / HEAD)
    oc = o - mu
    var = _head_sum(oc * oc, ones) * (1.0 / HEAD)
    y = oc * lax.rsqrt(var + LNX_EPS) * lnxg_ref[...] + lnxb_ref[...]
    v = v_ref[...]
    bonus = _head_sum(r_ref[...] * k_ref[...] * rk_ref[...], ones) * v
    y = (y + bonus) * g_ref[...]
    mixed = _dot(pool_ref[...], wop_ref[...]) + _dot(y, wor_ref[...])
    h = _layer_norm(ALPHA * x_ref[...] + mixed, ln1g_ref[...], ln1b_ref[...])
    h_o[...] = h

    h_hi = h.astype(BF16)
    h_lo = (h - h_hi.astype(F32)).astype(BF16)
    logits = (jnp.dot(h_hi, rw_hi_ref[...], preferred_element_type=F32)
              + jnp.dot(h_lo, rw_hi_ref[...], preferred_element_type=F32)
              + jnp.dot(h_hi, rw_lo_ref[...], preferred_element_type=F32)) + rb_ref[...]
    lane = lax.broadcasted_iota(jnp.int32, (ts, LANES), 1)
    neg = -jnp.inf
    gl = jnp.where(lane < N_GROUPS, logits, neg)
    gmax = jnp.max(gl, axis=-1, keepdims=True)
    g_idx = jnp.min(jnp.where(gl == gmax, lane, LANES), axis=-1, keepdims=True)
    g_top_p = 1.0 / jnp.sum(jnp.exp(gl - gmax), axis=-1, keepdims=True)
    lo_lane = EXPERT_LANE0 + g_idx * EXPERTS_PER_GROUP
    el = jnp.where((lane >= lo_lane) & (lane < lo_lane + EXPERTS_PER_GROUP), logits, neg)
    m1 = jnp.max(el, axis=-1, keepdims=True)
    i1 = jnp.min(jnp.where(el == m1, lane, LANES), axis=-1, keepdims=True)
    el2 = jnp.where(lane == i1, neg, el)
    m2 = jnp.max(el2, axis=-1, keepdims=True)
    i2 = jnp.min(jnp.where(el2 == m2, lane, LANES), axis=-1, keepdims=True)
    e21 = jnp.exp(m2 - m1)
    wgt1 = g_top_p / (1.0 + e21)
    wgt2 = g_top_p * e21 / (1.0 + e21)

    sel1 = lane == i1
    sel2 = lane == i2
    onehot = jnp.where(sel1 | sel2, 1.0, 0.0)
    before = jnp.dot(tri_ref[...], onehot.astype(BF16), preferred_element_type=F32)
    posn = base_ref[0:1, :] + before
    rank1 = jnp.sum(jnp.where(sel1, posn, 0.0), axis=-1, keepdims=True)
    rank2 = jnp.sum(jnp.where(sel2, posn, 0.0), axis=-1, keepdims=True)
    new_base = base_ref[0:1, :] + jnp.sum(onehot, axis=0, keepdims=True)
    base_ref[...] = jnp.broadcast_to(new_base, base_ref.shape)
    cnt_o[...] = jnp.broadcast_to(new_base, cnt_o.shape)

    e1 = (i1 - EXPERT_LANE0).astype(F32)
    e2 = (i2 - EXPERT_LANE0).astype(F32)
    meta = jnp.where(lane == 0, e1, jnp.where(lane == 1, e2, jnp.where(lane == 2, wgt1,
           jnp.where(lane == 3, wgt2, jnp.where(lane == 4, rank1, jnp.where(lane == 5, rank2, 0.0))))))
    meta_o[...] = meta
    meta_t_o[...] = jnp.transpose(meta)[0:8, :]


def _post_call(x2, pool2, o2, r2, k2, v2, g2, lnx_g, lnx_b, rk, ones_bd, wo_pool, wo_rwkv,
               ln1_g, ln1_b, rw_hi, rw_lo, rb, tri):
    n = x2.shape[0]
    full = lambda arr: pl.BlockSpec(arr.shape, lambda i: (0,) * arr.ndim)
    tok = lambda width: pl.BlockSpec((TS, width), lambda i: (i, 0))
    params = (lnx_g, lnx_b, rk, ones_bd, wo_pool, wo_rwkv, ln1_g, ln1_b, rw_hi, rw_lo, rb, tri)
    return pl.pallas_call(
        _post_kernel,
        grid=(n // TS,),
        in_specs=[tok(D_MODEL), tok(D_POOL)] + [tok(D_RWKV)] * 5 + [full(a) for a in params],
        out_specs=[tok(D_MODEL), tok(LANES), pl.BlockSpec((8, TS), lambda i: (0, i)),
                   pl.BlockSpec((8, LANES), lambda i: (0, 0))],
        out_shape=[jax.ShapeDtypeStruct((n, D_MODEL), F32), jax.ShapeDtypeStruct((n, LANES), F32),
                   jax.ShapeDtypeStruct((8, n), F32), jax.ShapeDtypeStruct((8, LANES), F32)],
        scratch_shapes=[pltpu.VMEM((8, LANES), F32)],
        compiler_params=pltpu.CompilerParams(
            dimension_semantics=("arbitrary",), vmem_limit_bytes=VMEM_LIMIT),
        name="post",
    )(x2, pool2, o2, r2, k2, v2, g2, *params)


def _start_rows(ts, row_copy):
    def issue(g, carry):
        for u in range(DMA_UNROLL):
            for slot in range(2):
                row_copy(g * DMA_UNROLL + u, slot).start(priority=(2 * u + slot) % 2)
        return carry

    lax.fori_loop(0, ts // DMA_UNROLL, issue, 0)


def _dispatch_kernel(pad_start_ref, pad_cnt_ref, n_used_ref, dest_ref, h_ref, xs_ref, zero_ref,
                     sem, zsem):
    i = pl.program_id(0)
    ts = h_ref.shape[0]
    n_blocks = xs_ref.shape[0] // BM

    def row_copy(t, slot):
        return pltpu.make_async_copy(h_ref.at[pl.ds(t, 1)],
                                     xs_ref.at[pl.ds(dest_ref[0, 0, slot * ts + t], 1)], sem)

    def zero_row_copy(row):
        return pltpu.make_async_copy(zero_ref.at[pl.ds(0, 1)], xs_ref.at[pl.ds(row, 1)], zsem)

    def zero_block_copy(blk):
        return pltpu.make_async_copy(zero_ref, xs_ref.at[pl.ds(pl.multiple_of(blk * BM, BM), BM)], zsem)

    @pl.when(i == 0)
    def _():
        zero_ref[...] = jnp.zeros_like(zero_ref)

        def fill(e, carry):
            start = pad_start_ref[e]

            def one(j, c):
                zero_row_copy(start + j).start()
                return c

            return lax.fori_loop(0, pad_cnt_ref[e], one, carry)

        def fill_wait(e, carry):
            def one(j, c):
                zero_row_copy(0).wait()
                return c

            return lax.fori_loop(0, pad_cnt_ref[e], one, carry)

        def tail(blk, carry):
            zero_block_copy(blk).start()
            return carry

        def tail_wait(blk, carry):
            zero_block_copy(blk).wait()
            return carry

        lax.fori_loop(0, N_EXPERTS, fill, 0)
        lax.fori_loop(n_used_ref[0], n_blocks, tail, 0)
        lax.fori_loop(0, N_EXPERTS, fill_wait, 0)
        lax.fori_loop(n_used_ref[0], n_blocks, tail_wait, 0)

    _start_rows(ts, row_copy)
    for _ in range(2):
        pltpu.make_async_copy(h_ref, xs_ref.at[pl.ds(0, ts)], sem).wait()


def _dispatch_call(pad_start, pad_cnt, n_used, dest, h, m_pad):
    n = h.shape[0]
    grid_spec = pltpu.PrefetchScalarGridSpec(
        num_scalar_prefetch=3,
        grid=(n // TS,),
        in_specs=[pl.BlockSpec((1, 1, 2 * TS), lambda i, *_: (i, 0, 0), memory_space=pltpu.SMEM),
                  pl.BlockSpec((TS, D_MODEL), lambda i, *_: (i, 0))],
        out_specs=pl.BlockSpec(memory_space=pl.ANY),
        scratch_shapes=[pltpu.VMEM((BM, D_MODEL), F32), pltpu.SemaphoreType.DMA(()),
                        pltpu.SemaphoreType.DMA(())],
    )
    return pl.pallas_call(
        _dispatch_kernel,
        grid_spec=grid_spec,
        out_shape=jax.ShapeDtypeStruct((m_pad, D_MODEL), F32),
        compiler_params=pltpu.CompilerParams(
            dimension_semantics=("arbitrary",), vmem_limit_bytes=VMEM_LIMIT,
            disable_bounds_checks=True),
        name="dispatch",
    )(pad_start, pad_cnt, n_used, dest, h)


def _expert_kernel(blk_e_ref, n_used_ref, xs_ref, wg_ref, wu_ref, wd_ref, ys_ref):
    i = pl.program_id(0)

    @pl.when(i < n_used_ref[0])
    def _():
        xb = xs_ref[...].astype(BF16)
        gate = jnp.dot(xb, wg_ref[0].astype(BF16), preferred_element_type=F32)
        up = jnp.dot(xb, wu_ref[0].astype(BF16), preferred_element_type=F32)
        hid = gate * _sigmoid(gate) * up
        ys_ref[...] = _dot(hid, wd_ref[0])

    @pl.when(i >= n_used_ref[0])
    def _():
        ys_ref[...] = jnp.zeros_like(ys_ref)


def _expert_call(blk_e, n_used, xs, exp_gate, exp_up, exp_down):
    m_pad = xs.shape[0]
    n_blocks = m_pad // BM
    row_blk = lambda i, be, nu: (jnp.minimum(i, nu[0] - 1), 0)
    out_blk = lambda i, be, nu: (i, 0)
    w_blk = lambda i, be, nu: (be[jnp.minimum(i, nu[0] - 1)], 0, 0)
    grid_spec = pltpu.PrefetchScalarGridSpec(
        num_scalar_prefetch=2,
        grid=(n_blocks,),
        in_specs=[pl.BlockSpec((BM, D_MODEL), row_blk),
                  pl.BlockSpec((1, D_MODEL, D_EXPERT), w_blk),
                  pl.BlockSpec((1, D_MODEL, D_EXPERT), w_blk),
                  pl.BlockSpec((1, D_EXPERT, D_MODEL), w_blk)],
        out_specs=pl.BlockSpec((BM, D_MODEL), out_blk),
    )
    return pl.pallas_call(
        _expert_kernel,
        grid_spec=grid_spec,
        out_shape=jax.ShapeDtypeStruct((m_pad, D_MODEL), F32),
        compiler_params=pltpu.CompilerParams(
            dimension_semantics=("arbitrary",), vmem_limit_bytes=VMEM_LIMIT),
        name="experts",
    )(blk_e, n_used, xs, exp_gate, exp_up, exp_down)


def _combine_kernel(dest_ref, dest_next_ref, h_ref, meta_ref, ys_ref, g_ref, b_ref, out_ref,
                    y_ref, sem):
    i = pl.program_id(0)
    n_steps = pl.num_programs(0)
    ts = h_ref.shape[0]

    def gather(idx_ref, buf):
        _start_rows(ts, lambda t, slot: pltpu.make_async_copy(
            ys_ref.at[pl.ds(idx_ref[0, 0, slot * ts + t], 1)],
            y_ref.at[buf, slot, pl.ds(t, 1)], sem.at[buf]))

    cur = lax.rem(i, 2)

    @pl.when(i == 0)
    def _():
        gather(dest_ref, 0)

    @pl.when(i + 1 < n_steps)
    def _():
        gather(dest_next_ref, 1 - cur)

    for slot in range(2):
        pltpu.make_async_copy(ys_ref.at[pl.ds(0, ts)], y_ref.at[cur, slot], sem.at[cur]).wait()
    meta = meta_ref[...]
    ffn = y_ref[cur, 0] * meta[:, 2:3] + y_ref[cur, 1] * meta[:, 3:4]
    out_ref[...] = _layer_norm(ALPHA * h_ref[...] + ffn, g_ref[...], b_ref[...])


def _combine_call(dest, h, meta, ys, ln2_g, ln2_b):
    n = h.shape[0]
    full = lambda arr: pl.BlockSpec(arr.shape, lambda i: (0,) * arr.ndim)
    return pl.pallas_call(
        _combine_kernel,
        grid=(n // TS,),
        in_specs=[pl.BlockSpec((1, 1, 2 * TS), lambda i: (i, 0, 0), memory_space=pltpu.SMEM),
                  pl.BlockSpec((1, 1, 2 * TS), lambda i: (jnp.minimum(i + 1, n // TS - 1), 0, 0),
                               memory_space=pltpu.SMEM),
                  pl.BlockSpec((TS, D_MODEL), lambda i: (i, 0)),
                  pl.BlockSpec((TS, LANES), lambda i: (i, 0)),
                  pl.BlockSpec(memory_space=pl.ANY),
                  full(ln2_g), full(ln2_b)],
        out_specs=pl.BlockSpec((TS, D_MODEL), lambda i: (i, 0)),
        out_shape=jax.ShapeDtypeStruct((n, D_MODEL), F32),
        scratch_shapes=[pltpu.VMEM((2, 2, TS, D_MODEL), F32), pltpu.SemaphoreType.DMA((2,))],
        compiler_params=pltpu.CompilerParams(
            dimension_semantics=("arbitrary",), vmem_limit_bytes=VMEM_LIMIT,
            disable_bounds_checks=True),
        name="combine",
    )(dest, dest, h, meta, ys, ln2_g, ln2_b)


def _block_diag(blocks):
    g, c, d = blocks.shape
    eye = jnp.eye(g, dtype=blocks.dtype)
    return (eye[:, None, :, None] * blocks[:, :, None, :]).reshape(g * c, g * d)


def _layer(x, w_in, pool_w, pool_scale, mu_shift, w0, w_up, a0, a_up, g_up, k_k, k_a, r_k,
           lnx_g, lnx_b, w_out, ln1_g, ln1_b, router_group, router_group_b, router_expert,
           router_expert_b, exp_gate, exp_up, exp_down, ln2_g, ln2_b):
    b, s, d = x.shape
    n = b * s
    row2 = lambda t: t.reshape(1, -1)

    ones_bd = _block_diag(jnp.ones((2, HEAD, HEAD), BF16))
    lora_w = _block_diag(jnp.stack([w_up, a_up])).astype(BF16)
    poolw_bd = _block_diag(pool_w).astype(BF16)
    rw = jnp.zeros((D_MODEL, LANES), F32)
    rw = rw.at[:, 0:N_GROUPS].set(router_group).at[:, EXPERT_LANE0:EXPERT_LANE0 + N_EXPERTS].set(router_expert)
    rw_hi = rw.astype(BF16)
    rw_lo = (rw - rw_hi.astype(F32)).astype(BF16)
    rb = jnp.zeros((1, LANES), F32)
    rb = rb.at[0, 0:N_GROUPS].set(router_group_b).at[0, EXPERT_LANE0:EXPERT_LANE0 + N_EXPERTS].set(router_expert_b)
    tri = (lax.broadcasted_iota(jnp.int32, (TS, TS), 0) > lax.broadcasted_iota(jnp.int32, (TS, TS), 1)).astype(BF16)

    pool_out, r, w, k, v, a_vec, b_vec, g = _prep_call(
        x, w_in.astype(BF16), row2(mu_shift), row2(w0), row2(a0), lora_w, g_up.astype(BF16),
        row2(k_k), row2(k_a), poolw_bd, row2(pool_scale), ones_bd)
    o = _rwkv_call(r, w, k, v, a_vec, b_vec)

    flat = lambda t: t.reshape(n, t.shape[-1])
    w_out_bf = w_out.astype(BF16)
    h, meta, meta_t, counts = _post_call(
        flat(x), flat(pool_out), flat(o), flat(r), flat(k), flat(v), flat(g),
        row2(lnx_g), row2(lnx_b), row2(r_k), ones_bd, w_out_bf[0:D_POOL], w_out_bf[D_POOL:],
        row2(ln1_g), row2(ln1_b), rw_hi, rw_lo, rb, tri)

    cnt = counts[0, EXPERT_LANE0:EXPERT_LANE0 + N_EXPERTS].astype(jnp.int32)
    padded = (cnt + BM - 1) // BM * BM
    pend = jnp.cumsum(padded)
    pstart = pend - padded
    e_ids = meta_t[0:2].astype(jnp.int32)
    onehot = e_ids[:, :, None] == jnp.arange(N_EXPERTS, dtype=jnp.int32)
    dest = jnp.sum(jnp.where(onehot, pstart, 0), axis=-1) + meta_t[4:6].astype(jnp.int32)
    dest = dest.reshape(2, n // TS, TS).transpose(1, 0, 2).reshape(n // TS, 1, 2 * TS)
    m_pad = 2 * n + N_EXPERTS * BM
    n_blocks = m_pad // BM
    blk_start = jnp.arange(n_blocks, dtype=jnp.int32) * BM
    blk_e = jnp.minimum(jnp.sum(blk_start[:, None] >= pend[None, :], axis=1), N_EXPERTS - 1).astype(jnp.int32)
    n_used = (pend[-1:] // BM).astype(jnp.int32)

    xs = _dispatch_call(pstart + cnt, padded - cnt, n_used, dest, h, m_pad)
    ys = _expert_call(blk_e, n_used, xs, exp_gate, exp_up, exp_down)
    out = _combine_call(dest, h, meta, ys, row2(ln2_g), row2(ln2_b))
    return out.reshape(b, s, d)


def kernel(x, w_in, pool_w, pool_scale, mu_shift, w0, w_up, a0, a_up, g_up, k_k, k_a, r_k, lnx_g, lnx_b, w_out, ln1_g, ln1_b, router_group, router_group_b, router_expert, router_expert_b, exp_gate, exp_up, exp_down, ln2_g, ln2_b):
    depth = w_in.shape[0]
    for l in range(depth):
        x = _layer(x, w_in[l], pool_w[l], pool_scale[l], mu_shift[l], w0[l], w_up[l], a0[l],
                   a_up[l], g_up[l], k_k[l], k_a[l], r_k[l], lnx_g[l], lnx_b[l], w_out[l],
                   ln1_g[l], ln1_b[l], router_group[l], router_group_b[l], router_expert[l],
                   router_expert_b[l], exp_gate[l], exp_up[l], exp_down[l], ln2_g[l], ln2_b[l])
    return x
```

```python
import functools

import jax
import jax.numpy as jnp
from jax import lax
from jax.experimental import pallas as pl
from jax.experimental.pallas import tpu as pltpu

F32 = jnp.float32
BF16 = jnp.bfloat16

D_MODEL = 1024
D_POOL = 256
POOL_WINDOWS = (2, 4, 8, 16)
POOL_GROUP = 64
POOL_HALO = 16
D_RWKV = 768
HEAD = 64
D_DECAY_LORA = 64
D_AAA_LORA = 64
D_GATE_LORA = 128
D_RWKV_IN = 3 * D_RWKV + D_DECAY_LORA + D_AAA_LORA + D_GATE_LORA
D_IN = D_POOL + D_RWKV_IN
N_GROUPS = 4
EXPERTS_PER_GROUP = 8
N_EXPERTS = 32
D_EXPERT = 256
LN_EPS = 1e-5
LNX_EPS = 64e-5
ALPHA = 2.0 ** 0.25
DECAY_SCALE = 0.6065306597126334
KK_NORM_FLOOR = 1e-24

LANES = 128
VMEM_LIMIT = 56 * 1024 * 1024

PAIR = 2 * HEAD
N_PAIRS = D_RWKV // PAIR
CHUNK = 64
RWKV_TILE = 4 * CHUNK
TS = 512
BM = 512
DMA_UNROLL = 8
EXPERT_LANE0 = 32
D_EXT = D_MODEL + LANES
META_GROUP = 8
META_RANK = 9


def _dot(a, b):
    return jnp.dot(a.astype(BF16), b.astype(BF16), preferred_element_type=F32)


def _dot_nt(a, b):
    return lax.dot_general(a.astype(BF16), b.astype(BF16), (((1,), (1,)), ((), ())),
                           preferred_element_type=F32)


def _dot_tn(a, b):
    return lax.dot_general(a.astype(BF16), b.astype(BF16), (((0,), (0,)), ((), ())),
                           preferred_element_type=F32)


def _head_sum(x, ones_bd):
    parts = [_dot(x[:, p * PAIR:(p + 1) * PAIR], ones_bd) for p in range(N_PAIRS)]
    return jnp.concatenate(parts, axis=1)


def _sigmoid(x):
    return 1.0 / (1.0 + jnp.exp(-x))


def _prep_kernel(x_ref, win_ref, mu_ref, w0_ref, a0_ref, lora_ref, gup_ref, kk_ref, ka_ref,
                 poolw_ref, pools_ref, ones_ref,
                 pool_o, r_o, w_o, k_o, v_o, a_o, b_o, g_o,
                 proj_ref, halo_ref):
    i = pl.program_id(1)
    ts = x_ref.shape[1]

    @pl.when(i == 0)
    def _():
        halo_ref[...] = jnp.zeros_like(halo_ref)

    proj_ref[...] = jnp.dot(x_ref[0].astype(BF16), win_ref[...], preferred_element_type=F32)

    row = lax.broadcasted_iota(jnp.int32, (ts, 1), 0)

    def shifted(off, width):
        z = proj_ref[:, off:off + width]
        prev = jnp.where(row == 0, halo_ref[POOL_HALO - 1:POOL_HALO, off:off + width],
                         pltpu.roll(z, 1, 0))
        return z + (prev - z) * mu_ref[:, off - D_POOL:off - D_POOL + width]

    p = proj_ref[:, 0:D_POOL]
    ext = jnp.concatenate([halo_ref[:, 0:D_POOL], p], axis=0)
    s2 = ext + pltpu.roll(ext, 1, 0)
    s4 = s2 + pltpu.roll(s2, 2, 0)
    s8 = s4 + pltpu.roll(s4, 4, 0)
    s16 = s8 + pltpu.roll(s8, 8, 0)
    lane = lax.broadcasted_iota(jnp.int32, (ts, D_POOL), 1)
    grp = lane // POOL_GROUP
    wsum = jnp.where(grp == 0, s2[POOL_HALO:], jnp.where(grp == 1, s4[POOL_HALO:],
                     jnp.where(grp == 2, s8[POOL_HALO:], s16[POOL_HALO:])))
    win = jnp.where(grp == 0, 2.0, jnp.where(grp == 1, 4.0, jnp.where(grp == 2, 8.0, 16.0)))
    pos = (i * ts + row + 1).astype(F32)
    diff = wsum / jnp.minimum(pos, win) - p
    pool_o[0] = _dot(diff, poolw_ref[...]) * pools_ref[...]

    o = D_POOL
    r = shifted(o, D_RWKV)
    k = shifted(o + D_RWKV, D_RWKV)
    v = shifted(o + 2 * D_RWKV, D_RWKV)
    lw = shifted(o + 3 * D_RWKV, D_DECAY_LORA + D_AAA_LORA)
    gd = shifted(o + 3 * D_RWKV + D_DECAY_LORA + D_AAA_LORA, D_GATE_LORA)

    lane128 = lax.broadcasted_iota(jnp.int32, (ts, LANES), 1)
    lora_in = jnp.where(lane128 < D_DECAY_LORA, jnp.tanh(lw), lw)
    lora = _dot(lora_in, lora_ref[...])
    w_o[0] = -DECAY_SCALE * _sigmoid(w0_ref[...] + lora[:, 0:D_RWKV])
    eta = _sigmoid(a0_ref[...] + lora[:, D_RWKV:2 * D_RWKV])
    g_o[0] = _dot(_sigmoid(gd), gup_ref[...])
    kk = k * kk_ref[...]
    ss = _head_sum(kk * kk, ones_ref[...])
    kkn = kk * lax.rsqrt(jnp.maximum(ss, KK_NORM_FLOOR))
    r_o[0] = r
    v_o[0] = v
    k_o[0] = k * (1.0 + (eta - 1.0) * ka_ref[...])
    a_o[0] = -kkn
    b_o[0] = kkn * eta

    halo_ref[...] = proj_ref[ts - POOL_HALO:ts, :]


def _prep_call(x, win_bf, mu, w0, a0, lora_w, gup_bf, k_k, k_a, poolw_bd, pool_scale, ones_bd):
    b, s, _ = x.shape
    grid = (b, s // TS)
    full = lambda arr: pl.BlockSpec(arr.shape, lambda bi, i: (0,) * arr.ndim)
    tok = lambda width: pl.BlockSpec((1, TS, width), lambda bi, i: (bi, i, 0))
    outs = [jax.ShapeDtypeStruct((b, s, D_POOL), F32)] + [jax.ShapeDtypeStruct((b, s, D_RWKV), F32)] * 7
    params = (win_bf, mu, w0, a0, lora_w, gup_bf, k_k, k_a, poolw_bd, pool_scale, ones_bd)
    return pl.pallas_call(
        _prep_kernel,
        grid=grid,
        in_specs=[tok(D_MODEL)] + [full(a) for a in params],
        out_specs=[tok(D_POOL)] + [tok(D_RWKV)] * 7,
        out_shape=outs,
        scratch_shapes=[pltpu.VMEM((TS, D_IN), F32), pltpu.VMEM((POOL_HALO, D_IN), F32)],
        compiler_params=pltpu.CompilerParams(
            dimension_semantics=("parallel", "arbitrary"), vmem_limit_bytes=VMEM_LIMIT),
        name="prep",
    )(x, *params)


def _rwkv_kernel(r_ref, w_ref, k_ref, v_ref, a_ref, b_ref, o_ref, h_ref):
    c = pl.program_id(1)

    @pl.when(c == 0)
    def _():
        h_ref[...] = jnp.zeros_like(h_ref)

    L = CHUNK
    n_chunks = r_ref.shape[1] // L
    row = lax.broadcasted_iota(jnp.int32, (L, PAIR), 0)
    lane = lax.broadcasted_iota(jnp.int32, (L, PAIR), 1)
    head0 = lane < HEAD
    strict = row > (lane & (L - 1))
    incl = row >= (lane & (L - 1))
    eye_wide = jnp.where(row == (lane & (L - 1)), 1.0, 0.0)

    def expand(x):
        xb = x.astype(BF16)
        zero = jnp.zeros_like(xb)
        return jnp.concatenate([jnp.where(head0, xb, zero), jnp.where(head0, zero, xb)], axis=0)

    def tile(ref, j, p):
        return ref[0, j * L:(j + 1) * L, p * PAIR:(p + 1) * PAIR]

    streams = [(j, p) for j in range(n_chunks) for p in range(N_PAIRS)]
    st = []
    for j, p in streams:
        w = tile(w_ref, j, p)
        cum = w
        for sh in (1, 2, 4, 8, 16, 32):
            cum = cum + jnp.where(row >= sh, pltpu.roll(cum, sh, 0), 0.0)
        tot = cum[L - 1:L, :]
        e_neg = jnp.exp(-cum)
        e_rem = jnp.exp(tot - cum)
        b = tile(b_ref, j, p)
        k = tile(k_ref, j, p)
        a_n = tile(a_ref, j, p) * jnp.exp(cum - w)
        st.append(dict(
            ar=jnp.concatenate([a_n, tile(r_ref, j, p) * jnp.exp(cum)], axis=0).astype(BF16),
            a_e=expand(a_n),
            v_e=expand(tile(v_ref, j, p)),
            bk_t=jnp.concatenate([expand(b * e_neg), expand(k * e_neg)], axis=0),
            bk_h=jnp.concatenate([expand(b * e_rem), expand(k * e_rem)], axis=0),
            w_tot=jnp.exp(tot)))
    for d in st:
        sc = _dot_nt(d['ar'], d['bk_t'])
        d['t'] = jnp.where(strict, sc[0:L, 0:PAIR], 0.0)
        d['t_ak'] = jnp.where(strict, sc[0:L, PAIR:2 * PAIR], 0.0)
        d['r_all'] = jnp.concatenate([d['ar'][L:2 * L],
                                      jnp.where(incl, sc[L:2 * L, 0:PAIR], 0.0).astype(BF16),
                                      jnp.where(incl, sc[L:2 * L, PAIR:2 * PAIR], 0.0).astype(BF16)],
                                     axis=1)
    for d in st:
        d['takv'] = _dot(d['t_ak'], d['v_e'])
        d['m'] = eye_wide + d['t']
        d['t'] = _dot(d['t'], expand(d['t']))
    for step in range(5):
        for d in st:
            if step < 4:
                both = _dot(d['t'], jnp.concatenate([expand(d['t']), expand(d['m'])], axis=1))
                d['t'] = both[:, 0:PAIR]
                d['m'] = d['m'] + both[:, PAIR:2 * PAIR]
            else:
                d['m'] = d['m'] + _dot(d['t'], expand(d['m']))
    for d in st:
        d['pq'] = _dot(d['m'], jnp.concatenate([d['a_e'], expand(d['takv'])], axis=1))

    h = [h_ref[p] for p in range(N_PAIRS)]
    for j in range(n_chunks):
        ds = st[j * N_PAIRS:(j + 1) * N_PAIRS]
        hb = [hp.astype(BF16) for hp in h]
        u_e = [expand(_dot(d['pq'][:, 0:PAIR], hb[p]) + d['pq'][:, PAIR:2 * PAIR])
               for p, d in enumerate(ds)]
        for p, d in enumerate(ds):
            o_ref[0, j * L:(j + 1) * L, p * PAIR:(p + 1) * PAIR] = _dot(
                d['r_all'], jnp.concatenate([hb[p], u_e[p], d['v_e']], axis=0))
        for p, d in enumerate(ds):
            h_add = _dot_tn(d['bk_h'], jnp.concatenate([u_e[p], d['v_e']], axis=0))
            w_col = jnp.transpose(jnp.broadcast_to(d['w_tot'], (PAIR, PAIR)))
            h[p] = h[p] * w_col + h_add
    for p in range(N_PAIRS):
        h_ref[p] = h[p]


def _rwkv_call(r, w, k, v, a, b):
    bsz, s, _ = r.shape
    spec = pl.BlockSpec((1, RWKV_TILE, D_RWKV), lambda bi, c: (bi, c, 0))
    return pl.pallas_call(
        _rwkv_kernel,
        grid=(bsz, s // RWKV_TILE),
        in_specs=[spec] * 6,
        out_specs=spec,
        out_shape=jax.ShapeDtypeStruct((bsz, s, D_RWKV), F32),
        scratch_shapes=[pltpu.VMEM((N_PAIRS, PAIR, PAIR), F32)],
        compiler_params=pltpu.CompilerParams(
            dimension_semantics=("parallel", "arbitrary"), vmem_limit_bytes=VMEM_LIMIT),
        name="rwkv",
    )(r, w, k, v, a, b)


def _layer_norm(x, g, b):
    mu = jnp.mean(x, axis=-1, keepdims=True)
    xc = x - mu
    var = jnp.mean(xc * xc, axis=-1, keepdims=True)
    return xc * lax.rsqrt(var + LN_EPS) * g + b


def _post_kernel(x_ref, pool_ref, o_ref, r_ref, k_ref, v_ref, g_ref,
                 lnxg_ref, lnxb_ref, rk_ref, ones_ref, wop_ref, wor_ref, ln1g_ref, ln1b_ref,
                 rw_hi_ref, rw_lo_ref, rb_ref, tri_ref,
                 h_o, meta_t_o, cnt_o, base_ref):
    i = pl.program_id(0)
    ts = x_ref.shape[0]

    @pl.when(i == 0)
    def _():
        base_ref[...] = jnp.zeros_like(base_ref)

    ones = ones_ref[...]
    o = o_ref[...]
    mu = _head_sum(o, ones) * (1.0 / HEAD)
    oc = o - mu
    var = _head_sum(oc * oc, ones) * (1.0 / HEAD)
    y = oc * lax.rsqrt(var + LNX_EPS) * lnxg_ref[...] + lnxb_ref[...]
    v = v_ref[...]
    bonus = _head_sum(r_ref[...] * k_ref[...] * rk_ref[...], ones) * v
    y = (y + bonus) * g_ref[...]
    mixed = _dot(pool_ref[...], wop_ref[...]) + _dot(y, wor_ref[...])
    h = _layer_norm(ALPHA * x_ref[...] + mixed, ln1g_ref[...], ln1b_ref[...])
    h_o[:, 0:D_MODEL] = h

    h_hi = h.astype(BF16)
    h_lo = (h - h_hi.astype(F32)).astype(BF16)
    logits = (jnp.dot(h_hi, rw_hi_ref[...], preferred_element_type=F32)
              + jnp.dot(h_lo, rw_hi_ref[...], preferred_element_type=F32)
              + jnp.dot(h_hi, rw_lo_ref[...], preferred_element_type=F32)) + rb_ref[...]
    lane = lax.broadcasted_iota(jnp.int32, (ts, LANES), 1)
    neg = -jnp.inf
    gl = jnp.where(lane < N_GROUPS, logits, neg)
    gmax = jnp.max(gl, axis=-1, keepdims=True)
    g_idx = jnp.min(jnp.where(gl == gmax, lane, LANES), axis=-1, keepdims=True)
    g_top_p = 1.0 / jnp.sum(jnp.exp(gl - gmax), axis=-1, keepdims=True)
    lo_lane = EXPERT_LANE0 + g_idx * EXPERTS_PER_GROUP
    el = jnp.where((lane >= lo_lane) & (lane < lo_lane + EXPERTS_PER_GROUP), logits, neg)
    m1 = jnp.max(el, axis=-1, keepdims=True)
    i1 = jnp.min(jnp.where(el == m1, lane, LANES), axis=-1, keepdims=True)
    el2 = jnp.where(lane == i1, neg, el)
    m2 = jnp.max(el2, axis=-1, keepdims=True)
    i2 = jnp.min(jnp.where(el2 == m2, lane, LANES), axis=-1, keepdims=True)
    e21 = jnp.exp(m2 - m1)
    wgt1 = g_top_p / (1.0 + e21)
    wgt2 = g_top_p * e21 / (1.0 + e21)

    in_group = lane == g_idx
    onehot = jnp.where(in_group, 1.0, 0.0)
    before = jnp.dot(tri_ref[...], onehot.astype(BF16), preferred_element_type=F32)
    posn = base_ref[0:1, :] + before
    rank = jnp.sum(jnp.where(in_group, posn, 0.0), axis=-1, keepdims=True)
    new_base = base_ref[0:1, :] + jnp.sum(onehot, axis=0, keepdims=True)
    base_ref[...] = jnp.broadcast_to(new_base, base_ref.shape)
    cnt_o[...] = jnp.broadcast_to(new_base, cnt_o.shape)

    meta = jnp.where(lane == i1 - lo_lane, wgt1, jnp.where(lane == i2 - lo_lane, wgt2, 0.0))
    meta = jnp.where(lane == META_GROUP, g_idx.astype(F32), jnp.where(lane == META_RANK, rank, meta))
    h_o[:, D_MODEL:D_EXT] = meta
    meta_t_o[...] = jnp.transpose(meta)[META_GROUP:META_GROUP + 8, :]


def _post_call(x2, pool2, o2, r2, k2, v2, g2, lnx_g, lnx_b, rk, ones_bd, wo_pool, wo_rwkv,
               ln1_g, ln1_b, rw_hi, rw_lo, rb, tri):
    n = x2.shape[0]
    full = lambda arr: pl.BlockSpec(arr.shape, lambda i: (0,) * arr.ndim)
    tok = lambda width: pl.BlockSpec((TS, width), lambda i: (i, 0))
    params = (lnx_g, lnx_b, rk, ones_bd, wo_pool, wo_rwkv, ln1_g, ln1_b, rw_hi, rw_lo, rb, tri)
    return pl.pallas_call(
        _post_kernel,
        grid=(n // TS,),
        in_specs=[tok(D_MODEL), tok(D_POOL)] + [tok(D_RWKV)] * 5 + [full(a) for a in params],
        out_specs=[tok(D_EXT), pl.BlockSpec((8, TS), lambda i: (0, i)),
                   pl.BlockSpec((8, LANES), lambda i: (0, 0))],
        out_shape=[jax.ShapeDtypeStruct((n, D_EXT), F32), jax.ShapeDtypeStruct((8, n), F32),
                   jax.ShapeDtypeStruct((8, LANES), F32)],
        scratch_shapes=[pltpu.VMEM((8, LANES), F32)],
        compiler_params=pltpu.CompilerParams(
            dimension_semantics=("arbitrary",), vmem_limit_bytes=VMEM_LIMIT),
        name="post",
    )(x2, pool2, o2, r2, k2, v2, g2, *params)


def _start_rows(ts, row_copy):
    def issue(g, carry):
        for u in range(DMA_UNROLL):
            row_copy(g * DMA_UNROLL + u).start(priority=u % 2)
        return carry

    lax.fori_loop(0, ts // DMA_UNROLL, issue, 0)


def _dispatch_kernel(pad_start_ref, pad_cnt_ref, n_used_ref, dest_ref, h_ref, xs_ref, zero_ref,
                     sem, zsem):
    i = pl.program_id(0)
    ts = h_ref.shape[0]
    n_blocks = xs_ref.shape[0] // BM

    def row_copy(t):
        return pltpu.make_async_copy(h_ref.at[pl.ds(t, 1)],
                                     xs_ref.at[pl.ds(dest_ref[0, 0, t], 1)], sem)

    def zero_row_copy(row):
        return pltpu.make_async_copy(zero_ref.at[pl.ds(0, 1)], xs_ref.at[pl.ds(row, 1)], zsem)

    def zero_block_copy(blk):
        return pltpu.make_async_copy(zero_ref, xs_ref.at[pl.ds(pl.multiple_of(blk * BM, BM), BM)], zsem)

    @pl.when(i == 0)
    def _():
        zero_ref[...] = jnp.zeros_like(zero_ref)

        def fill(e, carry):
            start = pad_start_ref[e]

            def one(j, c):
                zero_row_copy(start + j).start()
                return c

            return lax.fori_loop(0, pad_cnt_ref[e], one, carry)

        def fill_wait(e, carry):
            def one(j, c):
                zero_row_copy(0).wait()
                return c

            return lax.fori_loop(0, pad_cnt_ref[e], one, carry)

        def tail(blk, carry):
            zero_block_copy(blk).start()
            return carry

        def tail_wait(blk, carry):
            zero_block_copy(blk).wait()
            return carry

        lax.fori_loop(0, N_GROUPS, fill, 0)
        lax.fori_loop(n_used_ref[0], n_blocks, tail, 0)
        lax.fori_loop(0, N_GROUPS, fill_wait, 0)
        lax.fori_loop(n_used_ref[0], n_blocks, tail_wait, 0)

    _start_rows(ts, row_copy)
    pltpu.make_async_copy(h_ref, xs_ref.at[pl.ds(0, ts)], sem).wait()


def _dispatch_call(pad_start, pad_cnt, n_used, dest, h_ext, m_pad):
    n = h_ext.shape[0]
    grid_spec = pltpu.PrefetchScalarGridSpec(
        num_scalar_prefetch=3,
        grid=(n // TS,),
        in_specs=[pl.BlockSpec((1, 1, TS), lambda i, *_: (i, 0, 0), memory_space=pltpu.SMEM),
                  pl.BlockSpec((TS, D_EXT), lambda i, *_: (i, 0))],
        out_specs=pl.BlockSpec(memory_space=pl.ANY),
        scratch_shapes=[pltpu.VMEM((BM, D_EXT), F32), pltpu.SemaphoreType.DMA(()),
                        pltpu.SemaphoreType.DMA(())],
    )
    return pl.pallas_call(
        _dispatch_kernel,
        grid_spec=grid_spec,
        out_shape=jax.ShapeDtypeStruct((m_pad, D_EXT), F32),
        compiler_params=pltpu.CompilerParams(
            dimension_semantics=("arbitrary",), vmem_limit_bytes=VMEM_LIMIT,
            disable_bounds_checks=True),
        name="dispatch",
    )(pad_start, pad_cnt, n_used, dest, h_ext)


def _expert_kernel(blk_g_ref, n_used_ref, xs_ref, wg_ref, wu_ref, wd_ref, ys_ref):
    i = pl.program_id(0)

    @pl.when(i < n_used_ref[0])
    def _():
        xb = xs_ref[:, 0:D_MODEL].astype(BF16)
        wgt = xs_ref[:, D_MODEL:D_EXT]
        hid = []
        for e in range(EXPERTS_PER_GROUP):
            gate = jnp.dot(xb, wg_ref[0, e], preferred_element_type=F32)
            up = jnp.dot(xb, wu_ref[0, e], preferred_element_type=F32)
            w_e = wgt[:, e:e + 1]
            hid.append(jnp.where(w_e != 0.0, gate * _sigmoid(gate) * up * w_e, 0.0).astype(BF16))
        ys_ref[...] = jnp.dot(jnp.concatenate(hid, axis=1), wd_ref[0], preferred_element_type=F32)

    @pl.when(i >= n_used_ref[0])
    def _():
        ys_ref[...] = jnp.zeros_like(ys_ref)


def _expert_call(blk_g, n_used, xs, wg, wu, wd):
    m_pad = xs.shape[0]
    n_blocks = m_pad // BM
    row_blk = lambda i, bg, nu: (jnp.minimum(i, nu[0] - 1), 0)
    out_blk = lambda i, bg, nu: (i, 0)
    grp = lambda i, bg, nu: bg[jnp.minimum(i, nu[0] - 1)]
    grid_spec = pltpu.PrefetchScalarGridSpec(
        num_scalar_prefetch=2,
        grid=(n_blocks,),
        in_specs=[pl.BlockSpec((BM, D_EXT), row_blk),
                  pl.BlockSpec((1, EXPERTS_PER_GROUP, D_MODEL, D_EXPERT),
                               lambda i, bg, nu: (grp(i, bg, nu), 0, 0, 0)),
                  pl.BlockSpec((1, EXPERTS_PER_GROUP, D_MODEL, D_EXPERT),
                               lambda i, bg, nu: (grp(i, bg, nu), 0, 0, 0)),
                  pl.BlockSpec((1, EXPERTS_PER_GROUP * D_EXPERT, D_MODEL),
                               lambda i, bg, nu: (grp(i, bg, nu), 0, 0))],
        out_specs=pl.BlockSpec((BM, D_MODEL), out_blk),
    )
    return pl.pallas_call(
        _expert_kernel,
        grid_spec=grid_spec,
        out_shape=jax.ShapeDtypeStruct((m_pad, D_MODEL), F32),
        compiler_params=pltpu.CompilerParams(
            dimension_semantics=("arbitrary",), vmem_limit_bytes=VMEM_LIMIT),
        name="experts",
    )(blk_g, n_used, xs, wg, wu, wd)


def _combine_kernel(dest_ref, dest_next_ref, h_ref, ys_ref, g_ref, b_ref, out_ref, y_ref, sem):
    i = pl.program_id(0)
    n_steps = pl.num_programs(0)
    ts = h_ref.shape[0]

    def gather(idx_ref, buf):
        _start_rows(ts, lambda t: pltpu.make_async_copy(
            ys_ref.at[pl.ds(idx_ref[0, 0, t], 1)], y_ref.at[buf, pl.ds(t, 1)], sem.at[buf]))

    cur = lax.rem(i, 2)

    @pl.when(i == 0)
    def _():
        gather(dest_ref, 0)

    @pl.when(i + 1 < n_steps)
    def _():
        gather(dest_next_ref, 1 - cur)

    pltpu.make_async_copy(ys_ref.at[pl.ds(0, ts)], y_ref.at[cur], sem.at[cur]).wait()
    out_ref[...] = _layer_norm(ALPHA * h_ref[...] + y_ref[cur], g_ref[...], b_ref[...])


def _combine_call(dest, h_ext, ys, ln2_g, ln2_b):
    n = h_ext.shape[0]
    full = lambda arr: pl.BlockSpec(arr.shape, lambda i: (0,) * arr.ndim)
    return pl.pallas_call(
        _combine_kernel,
        grid=(n // TS,),
        in_specs=[pl.BlockSpec((1, 1, TS), lambda i: (i, 0, 0), memory_space=pltpu.SMEM),
                  pl.BlockSpec((1, 1, TS), lambda i: (jnp.minimum(i + 1, n // TS - 1), 0, 0),
                               memory_space=pltpu.SMEM),
                  pl.BlockSpec((TS, D_MODEL), lambda i: (i, 0)),
                  pl.BlockSpec(memory_space=pl.ANY),
                  full(ln2_g), full(ln2_b)],
        out_specs=pl.BlockSpec((TS, D_MODEL), lambda i: (i, 0)),
        out_shape=jax.ShapeDtypeStruct((n, D_MODEL), F32),
        scratch_shapes=[pltpu.VMEM((2, TS, D_MODEL), F32), pltpu.SemaphoreType.DMA((2,))],
        compiler_params=pltpu.CompilerParams(
            dimension_semantics=("arbitrary",), vmem_limit_bytes=VMEM_LIMIT,
            disable_bounds_checks=True),
        name="combine",
    )(dest, dest, h_ext, ys, ln2_g, ln2_b)


def _block_diag(blocks):
    g, c, d = blocks.shape
    eye = jnp.eye(g, dtype=blocks.dtype)
    return (eye[:, None, :, None] * blocks[:, :, None, :]).reshape(g * c, g * d)


def _layer(x, w_in, pool_w, pool_scale, mu_shift, w0, w_up, a0, a_up, g_up, k_k, k_a, r_k,
           lnx_g, lnx_b, w_out, ln1_g, ln1_b, router_group, router_group_b, router_expert,
           router_expert_b, exp_gate, exp_up, exp_down, ln2_g, ln2_b):
    b, s, d = x.shape
    n = b * s
    row2 = lambda t: t.reshape(1, -1)

    ones_bd = _block_diag(jnp.ones((2, HEAD, HEAD), BF16))
    lora_w = _block_diag(jnp.stack([w_up, a_up])).astype(BF16)
    poolw_bd = _block_diag(pool_w).astype(BF16)
    rw = jnp.zeros((D_MODEL, LANES), F32)
    rw = rw.at[:, 0:N_GROUPS].set(router_group).at[:, EXPERT_LANE0:EXPERT_LANE0 + N_EXPERTS].set(router_expert)
    rw_hi = rw.astype(BF16)
    rw_lo = (rw - rw_hi.astype(F32)).astype(BF16)
    rb = jnp.zeros((1, LANES), F32)
    rb = rb.at[0, 0:N_GROUPS].set(router_group_b).at[0, EXPERT_LANE0:EXPERT_LANE0 + N_EXPERTS].set(router_expert_b)
    tri = (lax.broadcasted_iota(jnp.int32, (TS, TS), 0) > lax.broadcasted_iota(jnp.int32, (TS, TS), 1)).astype(BF16)

    pool_out, r, w, k, v, a_vec, b_vec, g = _prep_call(
        x, w_in.astype(BF16), row2(mu_shift), row2(w0), row2(a0), lora_w, g_up.astype(BF16),
        row2(k_k), row2(k_a), poolw_bd, row2(pool_scale), ones_bd)
    o = _rwkv_call(r, w, k, v, a_vec, b_vec)

    flat = lambda t: t.reshape(n, t.shape[-1])
    w_out_bf = w_out.astype(BF16)
    h_ext, meta_t, counts = _post_call(
        flat(x), flat(pool_out), flat(o), flat(r), flat(k), flat(v), flat(g),
        row2(lnx_g), row2(lnx_b), row2(r_k), ones_bd, w_out_bf[0:D_POOL], w_out_bf[D_POOL:],
        row2(ln1_g), row2(ln1_b), rw_hi, rw_lo, rb, tri)

    cnt = counts[0, 0:N_GROUPS].astype(jnp.int32)
    padded = (cnt + BM - 1) // BM * BM
    pend = jnp.cumsum(padded)
    pstart = pend - padded
    g_ids = meta_t[0].astype(jnp.int32)
    onehot = g_ids[:, None] == jnp.arange(N_GROUPS, dtype=jnp.int32)
    dest = jnp.sum(jnp.where(onehot, pstart, 0), axis=-1) + meta_t[1].astype(jnp.int32)
    dest = dest.reshape(n // TS, 1, TS)
    m_pad = n + N_GROUPS * BM
    n_blocks = m_pad // BM
    blk_start = jnp.arange(n_blocks, dtype=jnp.int32) * BM
    blk_g = jnp.minimum(jnp.sum(blk_start[:, None] >= pend[None, :], axis=1), N_GROUPS - 1).astype(jnp.int32)
    n_used = (pend[-1:] // BM).astype(jnp.int32)

    grouped = lambda wts: wts.astype(BF16).reshape(N_GROUPS, EXPERTS_PER_GROUP, *wts.shape[1:])
    wd = exp_down.astype(BF16).reshape(N_GROUPS, EXPERTS_PER_GROUP * D_EXPERT, D_MODEL)
    xs = _dispatch_call(pstart + cnt, padded - cnt, n_used, dest, h_ext, m_pad)
    ys = _expert_call(blk_g, n_used, xs, grouped(exp_gate), grouped(exp_up), wd)
    out = _combine_call(dest, h_ext, ys, row2(ln2_g), row2(ln2_b))
    return out.reshape(b, s, d)


def kernel(x, w_in, pool_w, pool_scale, mu_shift, w0, w_up, a0, a_up, g_up, k_k, k_a, r_k, lnx_g, lnx_b, w_out, ln1_g, ln1_b, router_group, router_group_b, router_expert, router_expert_b, exp_gate, exp_up, exp_down, ln2_g, ln2_b):
    depth = w_in.shape[0]
    for l in range(depth):
        x = _layer(x, w_in[l], pool_w[l], pool_scale[l], mu_shift[l], w0[l], w_up[l], a0[l],
                   a_up[l], g_up[l], k_k[l], k_a[l], r_k[l], lnx_g[l], lnx_b[l], w_out[l],
                   ln1_g[l], ln1_b[l], router_group[l], router_group_b[l], router_expert[l],
                   router_expert_b[l], exp_gate[l], exp_up[l], exp_down[l], ln2_g[l], ln2_b[l])
    return x
```

```python
import numpy as np
import jax
import jax.numpy as jnp
from jax import lax
from jax.experimental import pallas as pl
from jax.experimental.pallas import tpu as pltpu

F32 = jnp.float32
BF16 = jnp.bfloat16

D_MODEL = 1024
D_POOL = 256
POOL_WINDOWS = (2, 4, 8, 16)
POOL_GROUP = 64
POOL_HALO = 16
D_RWKV = 768
HEAD = 64
D_DECAY_LORA = 64
D_AAA_LORA = 64
D_GATE_LORA = 128
D_RWKV_IN = 3 * D_RWKV + D_DECAY_LORA + D_AAA_LORA + D_GATE_LORA
D_IN = D_POOL + D_RWKV_IN
N_GROUPS = 4
EXPERTS_PER_GROUP = 8
N_EXPERTS = 32
D_EXPERT = 256
LN_EPS = 1e-5
LNX_EPS = 64e-5
ALPHA = 2.0 ** 0.25
DECAY_SCALE = 0.6065306597126334
KK_NORM_FLOOR = 1e-24

LANES = 128
VMEM_LIMIT = 56 * 1024 * 1024

PAIR = 2 * HEAD
N_PAIRS = D_RWKV // PAIR
CHUNK = 64
RWKV_TILE = 4 * CHUNK
TS = 512
BM = 256
DMA_UNROLL = 8
EXPERT_LANE0 = 32
D_EXT = D_MODEL + LANES
META_CAT = 8
META_RANK = 9
PAIRS_PER_GROUP = EXPERTS_PER_GROUP * (EXPERTS_PER_GROUP - 1) // 2
N_CATS = N_GROUPS * PAIRS_PER_GROUP


def _pair_member():
    member = np.zeros((N_CATS, EXPERTS_PER_GROUP), bool)
    for g in range(N_GROUPS):
        pair = 0
        for lo in range(EXPERTS_PER_GROUP):
            for hi in range(lo + 1, EXPERTS_PER_GROUP):
                member[g * PAIRS_PER_GROUP + pair, [lo, hi]] = True
                pair += 1
    return member


_PAIR_MEMBER = _pair_member()


def _dot(a, b):
    return jnp.dot(a.astype(BF16), b.astype(BF16), preferred_element_type=F32)


def _dot_nt(a, b):
    return lax.dot_general(a.astype(BF16), b.astype(BF16), (((1,), (1,)), ((), ())),
                           preferred_element_type=F32)


def _dot_tn(a, b):
    return lax.dot_general(a.astype(BF16), b.astype(BF16), (((0,), (0,)), ((), ())),
                           preferred_element_type=F32)


def _head_sum(x, ones_bd):
    parts = [_dot(x[:, p * PAIR:(p + 1) * PAIR], ones_bd) for p in range(N_PAIRS)]
    return jnp.concatenate(parts, axis=1)


def _sigmoid(x):
    return 1.0 / (1.0 + jnp.exp(-x))


def _prep_kernel(x_ref, win_ref, mu_ref, w0_ref, a0_ref, lora_ref, gup_ref, kk_ref, ka_ref,
                 poolw_ref, pools_ref, ones_ref,
                 pool_o, r_o, w_o, k_o, v_o, a_o, b_o, g_o,
                 proj_ref, halo_ref):
    i = pl.program_id(1)
    ts = x_ref.shape[1]

    @pl.when(i == 0)
    def _():
        halo_ref[...] = jnp.zeros_like(halo_ref)

    proj_ref[...] = jnp.dot(x_ref[0].astype(BF16), win_ref[...], preferred_element_type=F32)

    row = lax.broadcasted_iota(jnp.int32, (ts, 1), 0)

    def shifted(off, width):
        z = proj_ref[:, off:off + width]
        prev = jnp.where(row == 0, halo_ref[POOL_HALO - 1:POOL_HALO, off:off + width],
                         pltpu.roll(z, 1, 0))
        return z + (prev - z) * mu_ref[:, off - D_POOL:off - D_POOL + width]

    p = proj_ref[:, 0:D_POOL]
    ext = jnp.concatenate([halo_ref[:, 0:D_POOL], p], axis=0)
    s2 = ext + pltpu.roll(ext, 1, 0)
    s4 = s2 + pltpu.roll(s2, 2, 0)
    s8 = s4 + pltpu.roll(s4, 4, 0)
    s16 = s8 + pltpu.roll(s8, 8, 0)
    lane = lax.broadcasted_iota(jnp.int32, (ts, D_POOL), 1)
    grp = lane // POOL_GROUP
    wsum = jnp.where(grp == 0, s2[POOL_HALO:], jnp.where(grp == 1, s4[POOL_HALO:],
                     jnp.where(grp == 2, s8[POOL_HALO:], s16[POOL_HALO:])))
    win = jnp.where(grp == 0, 2.0, jnp.where(grp == 1, 4.0, jnp.where(grp == 2, 8.0, 16.0)))
    pos = (i * ts + row + 1).astype(F32)
    diff = wsum / jnp.minimum(pos, win) - p
    pool_o[0] = _dot(diff, poolw_ref[...]) * pools_ref[...]

    o = D_POOL
    r = shifted(o, D_RWKV)
    k = shifted(o + D_RWKV, D_RWKV)
    v = shifted(o + 2 * D_RWKV, D_RWKV)
    lw = shifted(o + 3 * D_RWKV, D_DECAY_LORA + D_AAA_LORA)
    gd = shifted(o + 3 * D_RWKV + D_DECAY_LORA + D_AAA_LORA, D_GATE_LORA)

    lane128 = lax.broadcasted_iota(jnp.int32, (ts, LANES), 1)
    lora_in = jnp.where(lane128 < D_DECAY_LORA, jnp.tanh(lw), lw)
    lora = _dot(lora_in, lora_ref[...])
    w_o[0] = -DECAY_SCALE * _sigmoid(w0_ref[...] + lora[:, 0:D_RWKV])
    eta = _sigmoid(a0_ref[...] + lora[:, D_RWKV:2 * D_RWKV])
    g_o[0] = _dot(_sigmoid(gd), gup_ref[...])
    kk = k * kk_ref[...]
    ss = _head_sum(kk * kk, ones_ref[...])
    kkn = kk * lax.rsqrt(jnp.maximum(ss, KK_NORM_FLOOR))
    r_o[0] = r
    v_o[0] = v
    k_o[0] = k * (1.0 + (eta - 1.0) * ka_ref[...])
    a_o[0] = -kkn
    b_o[0] = kkn * eta

    halo_ref[...] = proj_ref[ts - POOL_HALO:ts, :]


def _prep_call(x, win_bf, mu, w0, a0, lora_w, gup_bf, k_k, k_a, poolw_bd, pool_scale, ones_bd):
    b, s, _ = x.shape
    grid = (b, s // TS)
    full = lambda arr: pl.BlockSpec(arr.shape, lambda bi, i: (0,) * arr.ndim)
    tok = lambda width: pl.BlockSpec((1, TS, width), lambda bi, i: (bi, i, 0))
    outs = [jax.ShapeDtypeStruct((b, s, D_POOL), F32)] + [jax.ShapeDtypeStruct((b, s, D_RWKV), F32)] * 7
    params = (win_bf, mu, w0, a0, lora_w, gup_bf, k_k, k_a, poolw_bd, pool_scale, ones_bd)
    return pl.pallas_call(
        _prep_kernel,
        grid=grid,
        in_specs=[tok(D_MODEL)] + [full(a) for a in params],
        out_specs=[tok(D_POOL)] + [tok(D_RWKV)] * 7,
        out_shape=outs,
        scratch_shapes=[pltpu.VMEM((TS, D_IN), F32), pltpu.VMEM((POOL_HALO, D_IN), F32)],
        compiler_params=pltpu.CompilerParams(
            dimension_semantics=("parallel", "arbitrary"), vmem_limit_bytes=VMEM_LIMIT),
        name="prep",
    )(x, *params)


def _rwkv_kernel(r_ref, w_ref, k_ref, v_ref, a_ref, b_ref, o_ref, h_ref):
    c = pl.program_id(1)

    @pl.when(c == 0)
    def _():
        h_ref[...] = jnp.zeros_like(h_ref)

    L = CHUNK
    n_chunks = r_ref.shape[1] // L
    row = lax.broadcasted_iota(jnp.int32, (L, PAIR), 0)
    lane = lax.broadcasted_iota(jnp.int32, (L, PAIR), 1)
    head0 = lane < HEAD
    strict = row > (lane & (L - 1))
    incl = row >= (lane & (L - 1))
    eye_wide = jnp.where(row == (lane & (L - 1)), 1.0, 0.0)

    def expand(x):
        xb = x.astype(BF16)
        zero = jnp.zeros_like(xb)
        return jnp.concatenate([jnp.where(head0, xb, zero), jnp.where(head0, zero, xb)], axis=0)

    def tile(ref, j, p):
        return ref[0, j * L:(j + 1) * L, p * PAIR:(p + 1) * PAIR]

    streams = [(j, p) for j in range(n_chunks) for p in range(N_PAIRS)]
    st = []
    for j, p in streams:
        w = tile(w_ref, j, p)
        cum = w
        for sh in (1, 2, 4, 8, 16, 32):
            cum = cum + jnp.where(row >= sh, pltpu.roll(cum, sh, 0), 0.0)
        tot = cum[L - 1:L, :]
        e_neg = jnp.exp(-cum)
        e_rem = jnp.exp(tot - cum)
        b = tile(b_ref, j, p)
        k = tile(k_ref, j, p)
        a_n = tile(a_ref, j, p) * jnp.exp(cum - w)
        st.append(dict(
            ar=jnp.concatenate([a_n, tile(r_ref, j, p) * jnp.exp(cum)], axis=0).astype(BF16),
            a_e=expand(a_n),
            v_e=expand(tile(v_ref, j, p)),
            bk_t=jnp.concatenate([expand(b * e_neg), expand(k * e_neg)], axis=0),
            bk_h=jnp.concatenate([expand(b * e_rem), expand(k * e_rem)], axis=0),
            w_tot=jnp.exp(tot)))
    for d in st:
        sc = _dot_nt(d['ar'], d['bk_t'])
        d['t'] = jnp.where(strict, sc[0:L, 0:PAIR], 0.0)
        d['t_ak'] = jnp.where(strict, sc[0:L, PAIR:2 * PAIR], 0.0)
        d['r_all'] = jnp.concatenate([d['ar'][L:2 * L],
                                      jnp.where(incl, sc[L:2 * L, 0:PAIR], 0.0).astype(BF16),
                                      jnp.where(incl, sc[L:2 * L, PAIR:2 * PAIR], 0.0).astype(BF16)],
                                     axis=1)
    for d in st:
        d['takv'] = _dot(d['t_ak'], d['v_e'])
        d['m'] = eye_wide + d['t']
        d['t'] = _dot(d['t'], expand(d['t']))
    for step in range(5):
        for d in st:
            if step < 4:
                both = _dot(d['t'], jnp.concatenate([expand(d['t']), expand(d['m'])], axis=1))
                d['t'] = both[:, 0:PAIR]
                d['m'] = d['m'] + both[:, PAIR:2 * PAIR]
            else:
                d['m'] = d['m'] + _dot(d['t'], expand(d['m']))
    for d in st:
        d['pq'] = _dot(d['m'], jnp.concatenate([d['a_e'], expand(d['takv'])], axis=1))

    h = [h_ref[p] for p in range(N_PAIRS)]
    for j in range(n_chunks):
        ds = st[j * N_PAIRS:(j + 1) * N_PAIRS]
        hb = [hp.astype(BF16) for hp in h]
        u_e = [expand(_dot(d['pq'][:, 0:PAIR], hb[p]) + d['pq'][:, PAIR:2 * PAIR])
               for p, d in enumerate(ds)]
        for p, d in enumerate(ds):
            o_ref[0, j * L:(j + 1) * L, p * PAIR:(p + 1) * PAIR] = _dot(
                d['r_all'], jnp.concatenate([hb[p], u_e[p], d['v_e']], axis=0))
        for p, d in enumerate(ds):
            h_add = _dot_tn(d['bk_h'], jnp.concatenate([u_e[p], d['v_e']], axis=0))
            w_col = jnp.transpose(jnp.broadcast_to(d['w_tot'], (PAIR, PAIR)))
            h[p] = h[p] * w_col + h_add
    for p in range(N_PAIRS):
        h_ref[p] = h[p]


def _rwkv_call(r, w, k, v, a, b):
    bsz, s, _ = r.shape
    spec = pl.BlockSpec((1, RWKV_TILE, D_RWKV), lambda bi, c: (bi, c, 0))
    return pl.pallas_call(
        _rwkv_kernel,
        grid=(bsz, s // RWKV_TILE),
        in_specs=[spec] * 6,
        out_specs=spec,
        out_shape=jax.ShapeDtypeStruct((bsz, s, D_RWKV), F32),
        scratch_shapes=[pltpu.VMEM((N_PAIRS, PAIR, PAIR), F32)],
        compiler_params=pltpu.CompilerParams(
            dimension_semantics=("parallel", "arbitrary"), vmem_limit_bytes=VMEM_LIMIT),
        name="rwkv",
    )(r, w, k, v, a, b)


def _layer_norm(x, g, b):
    mu = jnp.mean(x, axis=-1, keepdims=True)
    xc = x - mu
    var = jnp.mean(xc * xc, axis=-1, keepdims=True)
    return xc * lax.rsqrt(var + LN_EPS) * g + b


def _post_kernel(x_ref, pool_ref, o_ref, r_ref, k_ref, v_ref, g_ref,
                 lnxg_ref, lnxb_ref, rk_ref, ones_ref, wop_ref, wor_ref, ln1g_ref, ln1b_ref,
                 rw_hi_ref, rw_lo_ref, rb_ref, tri_ref,
                 h_o, meta_t_o, cnt_o, base_ref):
    i = pl.program_id(0)
    ts = x_ref.shape[0]

    @pl.when(i == 0)
    def _():
        base_ref[...] = jnp.zeros_like(base_ref)

    ones = ones_ref[...]
    o = o_ref[...]
    mu = _head_sum(o, ones) * (1.0 / HEAD)
    oc = o - mu
    var = _head_sum(oc * oc, ones) * (1.0 / HEAD)
    y = oc * lax.rsqrt(var + LNX_EPS) * lnxg_ref[...] + lnxb_ref[...]
    v = v_ref[...]
    bonus = _head_sum(r_ref[...] * k_ref[...] * rk_ref[...], ones) * v
    y = (y + bonus) * g_ref[...]
    mixed = _dot(pool_ref[...], wop_ref[...]) + _dot(y, wor_ref[...])
    h = _layer_norm(ALPHA * x_ref[...] + mixed, ln1g_ref[...], ln1b_ref[...])
    h_o[:, 0:D_MODEL] = h

    h_hi = h.astype(BF16)
    h_lo = (h - h_hi.astype(F32)).astype(BF16)
    logits = (jnp.dot(h_hi, rw_hi_ref[...], preferred_element_type=F32)
              + jnp.dot(h_lo, rw_hi_ref[...], preferred_element_type=F32)
              + jnp.dot(h_hi, rw_lo_ref[...], preferred_element_type=F32)) + rb_ref[...]
    lane = lax.broadcasted_iota(jnp.int32, (ts, LANES), 1)
    neg = -jnp.inf
    gl = jnp.where(lane < N_GROUPS, logits, neg)
    gmax = jnp.max(gl, axis=-1, keepdims=True)
    g_idx = jnp.min(jnp.where(gl == gmax, lane, LANES), axis=-1, keepdims=True)
    g_top_p = 1.0 / jnp.sum(jnp.exp(gl - gmax), axis=-1, keepdims=True)
    lo_lane = EXPERT_LANE0 + g_idx * EXPERTS_PER_GROUP
    el = jnp.where((lane >= lo_lane) & (lane < lo_lane + EXPERTS_PER_GROUP), logits, neg)
    m1 = jnp.max(el, axis=-1, keepdims=True)
    i1 = jnp.min(jnp.where(el == m1, lane, LANES), axis=-1, keepdims=True)
    el2 = jnp.where(lane == i1, neg, el)
    m2 = jnp.max(el2, axis=-1, keepdims=True)
    i2 = jnp.min(jnp.where(el2 == m2, lane, LANES), axis=-1, keepdims=True)
    e21 = jnp.exp(m2 - m1)
    wgt1 = g_top_p / (1.0 + e21)
    wgt2 = g_top_p * e21 / (1.0 + e21)

    l1 = i1 - lo_lane
    l2 = i2 - lo_lane
    lo = jnp.minimum(l1, l2)
    hi = jnp.maximum(l1, l2)
    cat = g_idx * PAIRS_PER_GROUP + jnp.right_shift(lo * (2 * EXPERTS_PER_GROUP - 1 - lo), 1) + (hi - lo - 1)
    in_cat = lane == cat
    onehot = jnp.where(in_cat, 1.0, 0.0)
    before = jnp.dot(tri_ref[...], onehot.astype(BF16), preferred_element_type=F32)
    posn = base_ref[0:1, :] + before
    rank = jnp.sum(jnp.where(in_cat, posn, 0.0), axis=-1, keepdims=True)
    new_base = base_ref[0:1, :] + jnp.sum(onehot, axis=0, keepdims=True)
    base_ref[...] = jnp.broadcast_to(new_base, base_ref.shape)
    cnt_o[...] = jnp.broadcast_to(new_base, cnt_o.shape)

    meta = jnp.where(lane == l1, wgt1, jnp.where(lane == l2, wgt2, 0.0))
    meta = jnp.where(lane == META_CAT, cat.astype(F32), jnp.where(lane == META_RANK, rank, meta))
    h_o[:, D_MODEL:D_EXT] = meta
    meta_t_o[...] = jnp.transpose(meta)[META_CAT:META_CAT + 8, :]


def _post_call(x2, pool2, o2, r2, k2, v2, g2, lnx_g, lnx_b, rk, ones_bd, wo_pool, wo_rwkv,
               ln1_g, ln1_b, rw_hi, rw_lo, rb, tri):
    n = x2.shape[0]
    full = lambda arr: pl.BlockSpec(arr.shape, lambda i: (0,) * arr.ndim)
    tok = lambda width: pl.BlockSpec((TS, width), lambda i: (i, 0))
    params = (lnx_g, lnx_b, rk, ones_bd, wo_pool, wo_rwkv, ln1_g, ln1_b, rw_hi, rw_lo, rb, tri)
    return pl.pallas_call(
        _post_kernel,
        grid=(n // TS,),
        in_specs=[tok(D_MODEL), tok(D_POOL)] + [tok(D_RWKV)] * 5 + [full(a) for a in params],
        out_specs=[tok(D_EXT), pl.BlockSpec((8, TS), lambda i: (0, i)),
                   pl.BlockSpec((8, LANES), lambda i: (0, 0))],
        out_shape=[jax.ShapeDtypeStruct((n, D_EXT), F32), jax.ShapeDtypeStruct((8, n), F32),
                   jax.ShapeDtypeStruct((8, LANES), F32)],
        scratch_shapes=[pltpu.VMEM((8, LANES), F32)],
        compiler_params=pltpu.CompilerParams(
            dimension_semantics=("arbitrary",), vmem_limit_bytes=VMEM_LIMIT),
        name="post",
    )(x2, pool2, o2, r2, k2, v2, g2, *params)


def _start_rows(ts, row_copy):
    def issue(g, carry):
        for u in range(DMA_UNROLL):
            row_copy(g * DMA_UNROLL + u).start(priority=u % 2)
        return carry

    lax.fori_loop(0, ts // DMA_UNROLL, issue, 0)


def _dispatch_kernel(pad_start_ref, pad_cnt_ref, n_used_ref, dest_ref, h_ref, xs_ref, zero_ref,
                     sem, zsem):
    i = pl.program_id(0)
    ts = h_ref.shape[0]
    n_blocks = xs_ref.shape[0] // BM

    def row_copy(t):
        return pltpu.make_async_copy(h_ref.at[pl.ds(t, 1)],
                                     xs_ref.at[pl.ds(dest_ref[0, 0, t], 1)], sem)

    def zero_row_copy(row):
        return pltpu.make_async_copy(zero_ref.at[pl.ds(0, 1)], xs_ref.at[pl.ds(row, 1)], zsem)

    def zero_block_copy(blk):
        return pltpu.make_async_copy(zero_ref, xs_ref.at[pl.ds(pl.multiple_of(blk * BM, BM), BM)], zsem)

    @pl.when(i == 0)
    def _():
        zero_ref[...] = jnp.zeros_like(zero_ref)

        def fill(e, carry):
            start = pad_start_ref[e]

            def one(j, c):
                zero_row_copy(start + j).start()
                return c

            return lax.fori_loop(0, pad_cnt_ref[e], one, carry)

        def fill_wait(e, carry):
            def one(j, c):
                zero_row_copy(0).wait()
                return c

            return lax.fori_loop(0, pad_cnt_ref[e], one, carry)

        def tail(blk, carry):
            zero_block_copy(blk).start()
            return carry

        def tail_wait(blk, carry):
            zero_block_copy(blk).wait()
            return carry

        lax.fori_loop(0, N_GROUPS, fill, 0)
        lax.fori_loop(n_used_ref[0], n_blocks, tail, 0)
        lax.fori_loop(0, N_GROUPS, fill_wait, 0)
        lax.fori_loop(n_used_ref[0], n_blocks, tail_wait, 0)

    _start_rows(ts, row_copy)
    pltpu.make_async_copy(h_ref, xs_ref.at[pl.ds(0, ts)], sem).wait()


def _dispatch_call(pad_start, pad_cnt, n_used, dest, h_ext, m_pad):
    n = h_ext.shape[0]
    grid_spec = pltpu.PrefetchScalarGridSpec(
        num_scalar_prefetch=3,
        grid=(n // TS,),
        in_specs=[pl.BlockSpec((1, 1, TS), lambda i, *_: (i, 0, 0), memory_space=pltpu.SMEM),
                  pl.BlockSpec((TS, D_EXT), lambda i, *_: (i, 0))],
        out_specs=pl.BlockSpec(memory_space=pl.ANY),
        scratch_shapes=[pltpu.VMEM((BM, D_EXT), F32), pltpu.SemaphoreType.DMA(()),
                        pltpu.SemaphoreType.DMA(())],
    )
    return pl.pallas_call(
        _dispatch_kernel,
        grid_spec=grid_spec,
        out_shape=jax.ShapeDtypeStruct((m_pad, D_EXT), F32),
        compiler_params=pltpu.CompilerParams(
            dimension_semantics=("arbitrary",), vmem_limit_bytes=VMEM_LIMIT,
            disable_bounds_checks=True),
        name="dispatch",
    )(pad_start, pad_cnt, n_used, dest, h_ext)


def _expert_kernel(blk_g_ref, n_used_ref, act_ref, xs_ref, wg_ref, wu_ref, wd_ref, ys_ref, xb_ref):
    i = pl.program_id(0)

    @pl.when(i < n_used_ref[0])
    def _():
        xb_ref[...] = xs_ref[:, 0:D_MODEL].astype(BF16)
        ys_ref[...] = jnp.zeros_like(ys_ref)
        for e in range(EXPERTS_PER_GROUP):
            @pl.when(act_ref[i * EXPERTS_PER_GROUP + e] != 0)
            def _(e=e):
                xb = xb_ref[...]
                gate = jnp.dot(xb, wg_ref[0, e], preferred_element_type=F32)
                up = jnp.dot(xb, wu_ref[0, e], preferred_element_type=F32)
                w_e = xs_ref[:, D_MODEL + e:D_MODEL + e + 1]
                hid = jnp.where(w_e != 0.0, gate * _sigmoid(gate) * up * w_e, 0.0)
                ys_ref[...] += jnp.dot(hid.astype(BF16), wd_ref[0, e * D_EXPERT:(e + 1) * D_EXPERT],
                                       preferred_element_type=F32)

    @pl.when(i >= n_used_ref[0])
    def _():
        ys_ref[...] = jnp.zeros_like(ys_ref)


def _expert_call(blk_g, n_used, act, xs, wg, wu, wd):
    m_pad = xs.shape[0]
    n_blocks = m_pad // BM
    row_blk = lambda i, bg, nu, ac: (jnp.minimum(i, nu[0] - 1), 0)
    out_blk = lambda i, bg, nu, ac: (i, 0)
    grp = lambda i, bg, nu: bg[jnp.minimum(i, nu[0] - 1)]
    grid_spec = pltpu.PrefetchScalarGridSpec(
        num_scalar_prefetch=3,
        grid=(n_blocks,),
        in_specs=[pl.BlockSpec((BM, D_EXT), row_blk),
                  pl.BlockSpec((1, EXPERTS_PER_GROUP, D_MODEL, D_EXPERT),
                               lambda i, bg, nu, ac: (grp(i, bg, nu), 0, 0, 0)),
                  pl.BlockSpec((1, EXPERTS_PER_GROUP, D_MODEL, D_EXPERT),
                               lambda i, bg, nu, ac: (grp(i, bg, nu), 0, 0, 0)),
                  pl.BlockSpec((1, EXPERTS_PER_GROUP * D_EXPERT, D_MODEL),
                               lambda i, bg, nu, ac: (grp(i, bg, nu), 0, 0))],
        out_specs=pl.BlockSpec((BM, D_MODEL), out_blk),
        scratch_shapes=[pltpu.VMEM((BM, D_MODEL), BF16)],
    )
    return pl.pallas_call(
        _expert_kernel,
        grid_spec=grid_spec,
        out_shape=jax.ShapeDtypeStruct((m_pad, D_MODEL), F32),
        compiler_params=pltpu.CompilerParams(
            dimension_semantics=("arbitrary",), vmem_limit_bytes=VMEM_LIMIT),
        name="experts",
    )(blk_g, n_used, act, xs, wg, wu, wd)


def _combine_kernel(dest_ref, dest_next_ref, h_ref, ys_ref, g_ref, b_ref, out_ref, y_ref, sem):
    i = pl.program_id(0)
    n_steps = pl.num_programs(0)
    ts = h_ref.shape[0]

    def gather(idx_ref, buf):
        _start_rows(ts, lambda t: pltpu.make_async_copy(
            ys_ref.at[pl.ds(idx_ref[0, 0, t], 1)], y_ref.at[buf, pl.ds(t, 1)], sem.at[buf]))

    cur = lax.rem(i, 2)

    @pl.when(i == 0)
    def _():
        gather(dest_ref, 0)

    @pl.when(i + 1 < n_steps)
    def _():
        gather(dest_next_ref, 1 - cur)

    pltpu.make_async_copy(ys_ref.at[pl.ds(0, ts)], y_ref.at[cur], sem.at[cur]).wait()
    out_ref[...] = _layer_norm(ALPHA * h_ref[...] + y_ref[cur], g_ref[...], b_ref[...])


def _combine_call(dest, h_ext, ys, ln2_g, ln2_b):
    n = h_ext.shape[0]
    full = lambda arr: pl.BlockSpec(arr.shape, lambda i: (0,) * arr.ndim)
    return pl.pallas_call(
        _combine_kernel,
        grid=(n // TS,),
        in_specs=[pl.BlockSpec((1, 1, TS), lambda i: (i, 0, 0), memory_space=pltpu.SMEM),
                  pl.BlockSpec((1, 1, TS), lambda i: (jnp.minimum(i + 1, n // TS - 1), 0, 0),
                               memory_space=pltpu.SMEM),
                  pl.BlockSpec((TS, D_MODEL), lambda i: (i, 0)),
                  pl.BlockSpec(memory_space=pl.ANY),
                  full(ln2_g), full(ln2_b)],
        out_specs=pl.BlockSpec((TS, D_MODEL), lambda i: (i, 0)),
        out_shape=jax.ShapeDtypeStruct((n, D_MODEL), F32),
        scratch_shapes=[pltpu.VMEM((2, TS, D_MODEL), F32), pltpu.SemaphoreType.DMA((2,))],
        compiler_params=pltpu.CompilerParams(
            dimension_semantics=("arbitrary",), vmem_limit_bytes=VMEM_LIMIT,
            disable_bounds_checks=True),
        name="combine",
    )(dest, dest, h_ext, ys, ln2_g, ln2_b)


def _block_diag(blocks):
    g, c, d = blocks.shape
    eye = jnp.eye(g, dtype=blocks.dtype)
    return (eye[:, None, :, None] * blocks[:, :, None, :]).reshape(g * c, g * d)


def _layer(x, w_in, pool_w, pool_scale, mu_shift, w0, w_up, a0, a_up, g_up, k_k, k_a, r_k,
           lnx_g, lnx_b, w_out, ln1_g, ln1_b, router_group, router_group_b, router_expert,
           router_expert_b, exp_gate, exp_up, exp_down, ln2_g, ln2_b):
    b, s, d = x.shape
    n = b * s
    row2 = lambda t: t.reshape(1, -1)

    ones_bd = _block_diag(jnp.ones((2, HEAD, HEAD), BF16))
    lora_w = _block_diag(jnp.stack([w_up, a_up])).astype(BF16)
    poolw_bd = _block_diag(pool_w).astype(BF16)
    rw = jnp.zeros((D_MODEL, LANES), F32)
    rw = rw.at[:, 0:N_GROUPS].set(router_group).at[:, EXPERT_LANE0:EXPERT_LANE0 + N_EXPERTS].set(router_expert)
    rw_hi = rw.astype(BF16)
    rw_lo = (rw - rw_hi.astype(F32)).astype(BF16)
    rb = jnp.zeros((1, LANES), F32)
    rb = rb.at[0, 0:N_GROUPS].set(router_group_b).at[0, EXPERT_LANE0:EXPERT_LANE0 + N_EXPERTS].set(router_expert_b)
    tri = (lax.broadcasted_iota(jnp.int32, (TS, TS), 0) > lax.broadcasted_iota(jnp.int32, (TS, TS), 1)).astype(BF16)

    pool_out, r, w, k, v, a_vec, b_vec, g = _prep_call(
        x, w_in.astype(BF16), row2(mu_shift), row2(w0), row2(a0), lora_w, g_up.astype(BF16),
        row2(k_k), row2(k_a), poolw_bd, row2(pool_scale), ones_bd)
    o = _rwkv_call(r, w, k, v, a_vec, b_vec)

    flat = lambda t: t.reshape(n, t.shape[-1])
    w_out_bf = w_out.astype(BF16)
    h_ext, meta_t, counts = _post_call(
        flat(x), flat(pool_out), flat(o), flat(r), flat(k), flat(v), flat(g),
        row2(lnx_g), row2(lnx_b), row2(r_k), ones_bd, w_out_bf[0:D_POOL], w_out_bf[D_POOL:],
        row2(ln1_g), row2(ln1_b), rw_hi, rw_lo, rb, tri)

    cnt_cat = counts[0, 0:N_CATS].astype(jnp.int32).reshape(N_GROUPS, PAIRS_PER_GROUP)
    cnt = jnp.sum(cnt_cat, axis=1)
    padded = (cnt + BM - 1) // BM * BM
    pend = jnp.cumsum(padded)
    pstart = pend - padded
    cat_start = (pstart[:, None] + jnp.cumsum(cnt_cat, axis=1) - cnt_cat).reshape(N_CATS)
    cnt_cat = cnt_cat.reshape(N_CATS)
    cats = meta_t[0].astype(jnp.int32)
    onehot = cats[:, None] == jnp.arange(N_CATS, dtype=jnp.int32)
    dest = jnp.sum(jnp.where(onehot, cat_start, 0), axis=-1) + meta_t[1].astype(jnp.int32)
    dest = dest.reshape(n // TS, 1, TS)
    m_pad = n + N_GROUPS * BM
    n_blocks = m_pad // BM
    blk_start = jnp.arange(n_blocks, dtype=jnp.int32) * BM
    blk_g = jnp.minimum(jnp.sum(blk_start[:, None] >= pend[None, :], axis=1), N_GROUPS - 1).astype(jnp.int32)
    n_used = (pend[-1:] // BM).astype(jnp.int32)
    overlap = ((cat_start[None, :] < blk_start[:, None] + BM)
               & (cat_start[None, :] + cnt_cat[None, :] > blk_start[:, None])
               & (cnt_cat[None, :] > 0))
    act = jnp.any(overlap[:, :, None] & _PAIR_MEMBER[None], axis=1).astype(jnp.int32).reshape(-1)

    grouped = lambda wts: wts.astype(BF16).reshape(N_GROUPS, EXPERTS_PER_GROUP, *wts.shape[1:])
    wd = exp_down.astype(BF16).reshape(N_GROUPS, EXPERTS_PER_GROUP * D_EXPERT, D_MODEL)
    xs = _dispatch_call(pstart + cnt, padded - cnt, n_used, dest, h_ext, m_pad)
    ys = _expert_call(blk_g, n_used, act, xs, grouped(exp_gate), grouped(exp_up), wd)
    out = _combine_call(dest, h_ext, ys, row2(ln2_g), row2(ln2_b))
    return out.reshape(b, s, d)


def kernel(x, w_in, pool_w, pool_scale, mu_shift, w0, w_up, a0, a_up, g_up, k_k, k_a, r_k, lnx_g, lnx_b, w_out, ln1_g, ln1_b, router_group, router_group_b, router_expert, router_expert_b, exp_gate, exp_up, exp_down, ln2_g, ln2_b):
    depth = w_in.shape[0]
    for l in range(depth):
        x = _layer(x, w_in[l], pool_w[l], pool_scale[l], mu_shift[l], w0[l], w_up[l], a0[l],
                   a_up[l], g_up[l], k_k[l], k_a[l], r_k[l], lnx_g[l], lnx_b[l], w_out[l],
                   ln1_g[l], ln1_b[l], router_group[l], router_group_b[l], router_expert[l],
                   router_expert_b[l], exp_gate[l], exp_up[l], exp_down[l], ln2_g[l], ln2_b[l])
    return x
```

```python
import numpy as np
import jax
import jax.numpy as jnp
from jax import lax
from jax.experimental import pallas as pl
from jax.experimental.pallas import tpu as pltpu

F32 = jnp.float32
BF16 = jnp.bfloat16

D_MODEL = 1024
D_POOL = 256
POOL_WINDOWS = (2, 4, 8, 16)
POOL_GROUP = 64
POOL_HALO = 16
D_RWKV = 768
HEAD = 64
D_DECAY_LORA = 64
D_AAA_LORA = 64
D_GATE_LORA = 128
D_RWKV_IN = 3 * D_RWKV + D_DECAY_LORA + D_AAA_LORA + D_GATE_LORA
D_IN = D_POOL + D_RWKV_IN
N_GROUPS = 4
EXPERTS_PER_GROUP = 8
N_EXPERTS = 32
D_EXPERT = 256
LN_EPS = 1e-5
LNX_EPS = 64e-5
ALPHA = 2.0 ** 0.25
DECAY_SCALE = 0.6065306597126334
KK_NORM_FLOOR = 1e-24

LANES = 128
VMEM_LIMIT = 56 * 1024 * 1024

PAIR = 2 * HEAD
N_PAIRS = D_RWKV // PAIR
CHUNK = 64
RWKV_TILE = 4 * CHUNK
TS = 512
BM = 256
DMA_UNROLL = 8
EXPERT_LANE0 = 32
D_EXT = D_MODEL + LANES
META_CAT = 8
META_RANK = 9
PAIRS_PER_GROUP = EXPERTS_PER_GROUP * (EXPERTS_PER_GROUP - 1) // 2
N_CATS = N_GROUPS * PAIRS_PER_GROUP


def _pair_member():
    member = np.zeros((N_CATS, EXPERTS_PER_GROUP), bool)
    for g in range(N_GROUPS):
        pair = 0
        for lo in range(EXPERTS_PER_GROUP):
            for hi in range(lo + 1, EXPERTS_PER_GROUP):
                member[g * PAIRS_PER_GROUP + pair, [lo, hi]] = True
                pair += 1
    return member


_PAIR_MEMBER = _pair_member()


def _dot(a, b):
    return jnp.dot(a.astype(BF16), b.astype(BF16), preferred_element_type=F32)


def _dot_nt(a, b):
    return lax.dot_general(a.astype(BF16), b.astype(BF16), (((1,), (1,)), ((), ())),
                           preferred_element_type=F32)


def _dot_tn(a, b):
    return lax.dot_general(a.astype(BF16), b.astype(BF16), (((0,), (0,)), ((), ())),
                           preferred_element_type=F32)


def _head_sum(x, ones_bd):
    parts = [_dot(x[:, p * PAIR:(p + 1) * PAIR], ones_bd) for p in range(N_PAIRS)]
    return jnp.concatenate(parts, axis=1)


def _sigmoid(x):
    return 1.0 / (1.0 + jnp.exp(-x))


def _prep_kernel(x_ref, win_ref, mu_ref, w0_ref, a0_ref, lora_ref, gup_ref, kk_ref, ka_ref,
                 poolw_ref, pools_ref, ones_ref,
                 pool_o, r_o, w_o, k_o, v_o, a_o, b_o, g_o,
                 proj_ref, halo_ref):
    i = pl.program_id(1)
    ts = x_ref.shape[1]

    @pl.when(i == 0)
    def _():
        halo_ref[...] = jnp.zeros_like(halo_ref)

    proj_ref[...] = jnp.dot(x_ref[0].astype(BF16), win_ref[...], preferred_element_type=F32)

    row = lax.broadcasted_iota(jnp.int32, (ts, 1), 0)

    def shifted(off, width):
        z = proj_ref[:, off:off + width]
        prev = jnp.where(row == 0, halo_ref[POOL_HALO - 1:POOL_HALO, off:off + width],
                         pltpu.roll(z, 1, 0))
        return z + (prev - z) * mu_ref[:, off - D_POOL:off - D_POOL + width]

    p = proj_ref[:, 0:D_POOL]
    ext = jnp.concatenate([halo_ref[:, 0:D_POOL], p], axis=0)
    s2 = ext + pltpu.roll(ext, 1, 0)
    s4 = s2 + pltpu.roll(s2, 2, 0)
    s8 = s4 + pltpu.roll(s4, 4, 0)
    s16 = s8 + pltpu.roll(s8, 8, 0)
    lane = lax.broadcasted_iota(jnp.int32, (ts, D_POOL), 1)
    grp = lane // POOL_GROUP
    wsum = jnp.where(grp == 0, s2[POOL_HALO:], jnp.where(grp == 1, s4[POOL_HALO:],
                     jnp.where(grp == 2, s8[POOL_HALO:], s16[POOL_HALO:])))
    win = jnp.where(grp == 0, 2.0, jnp.where(grp == 1, 4.0, jnp.where(grp == 2, 8.0, 16.0)))
    pos = (i * ts + row + 1).astype(F32)
    diff = wsum / jnp.minimum(pos, win) - p
    pool_o[0] = _dot(diff, poolw_ref[...]) * pools_ref[...]

    o = D_POOL
    r = shifted(o, D_RWKV)
    k = shifted(o + D_RWKV, D_RWKV)
    v = shifted(o + 2 * D_RWKV, D_RWKV)
    lw = shifted(o + 3 * D_RWKV, D_DECAY_LORA + D_AAA_LORA)
    gd = shifted(o + 3 * D_RWKV + D_DECAY_LORA + D_AAA_LORA, D_GATE_LORA)

    lane128 = lax.broadcasted_iota(jnp.int32, (ts, LANES), 1)
    lora_in = jnp.where(lane128 < D_DECAY_LORA, jnp.tanh(lw), lw)
    lora = _dot(lora_in, lora_ref[...])
    w_o[0] = -DECAY_SCALE * _sigmoid(w0_ref[...] + lora[:, 0:D_RWKV])
    eta = _sigmoid(a0_ref[...] + lora[:, D_RWKV:2 * D_RWKV])
    g_o[0] = _dot(_sigmoid(gd), gup_ref[...])
    kk = k * kk_ref[...]
    ss = _head_sum(kk * kk, ones_ref[...])
    kkn = kk * lax.rsqrt(jnp.maximum(ss, KK_NORM_FLOOR))
    r_o[0] = r
    v_o[0] = v
    k_o[0] = k * (1.0 + (eta - 1.0) * ka_ref[...])
    a_o[0] = -kkn
    b_o[0] = kkn * eta

    halo_ref[...] = proj_ref[ts - POOL_HALO:ts, :]


def _prep_call(x, win_bf, mu, w0, a0, lora_w, gup_bf, k_k, k_a, poolw_bd, pool_scale, ones_bd):
    b, s, _ = x.shape
    grid = (b, s // TS)
    full = lambda arr: pl.BlockSpec(arr.shape, lambda bi, i: (0,) * arr.ndim)
    tok = lambda width: pl.BlockSpec((1, TS, width), lambda bi, i: (bi, i, 0))
    outs = [jax.ShapeDtypeStruct((b, s, D_POOL), F32)] + [jax.ShapeDtypeStruct((b, s, D_RWKV), F32)] * 7
    params = (win_bf, mu, w0, a0, lora_w, gup_bf, k_k, k_a, poolw_bd, pool_scale, ones_bd)
    return pl.pallas_call(
        _prep_kernel,
        grid=grid,
        in_specs=[tok(D_MODEL)] + [full(a) for a in params],
        out_specs=[tok(D_POOL)] + [tok(D_RWKV)] * 7,
        out_shape=outs,
        scratch_shapes=[pltpu.VMEM((TS, D_IN), F32), pltpu.VMEM((POOL_HALO, D_IN), F32)],
        compiler_params=pltpu.CompilerParams(
            dimension_semantics=("parallel", "arbitrary"), vmem_limit_bytes=VMEM_LIMIT),
        name="prep",
    )(x, *params)


def _rwkv_kernel(r_ref, w_ref, k_ref, v_ref, a_ref, b_ref, o_ref, h_ref):
    c = pl.program_id(1)

    @pl.when(c == 0)
    def _():
        h_ref[...] = jnp.zeros_like(h_ref)

    L = CHUNK
    n_chunks = r_ref.shape[1] // L
    row = lax.broadcasted_iota(jnp.int32, (L, PAIR), 0)
    lane = lax.broadcasted_iota(jnp.int32, (L, PAIR), 1)
    head0 = lane < HEAD
    strict = row > (lane & (L - 1))
    incl = row >= (lane & (L - 1))
    eye_wide = jnp.where(row == (lane & (L - 1)), 1.0, 0.0)

    def expand(x):
        xb = x.astype(BF16)
        zero = jnp.zeros_like(xb)
        return jnp.concatenate([jnp.where(head0, xb, zero), jnp.where(head0, zero, xb)], axis=0)

    def tile(ref, j, p):
        return ref[0, j * L:(j + 1) * L, p * PAIR:(p + 1) * PAIR]

    streams = [(j, p) for j in range(n_chunks) for p in range(N_PAIRS)]
    st = []
    for j, p in streams:
        w = tile(w_ref, j, p)
        cum = w
        for sh in (1, 2, 4, 8, 16, 32):
            cum = cum + jnp.where(row >= sh, pltpu.roll(cum, sh, 0), 0.0)
        tot = cum[L - 1:L, :]
        e_neg = jnp.exp(-cum)
        e_rem = jnp.exp(tot - cum)
        b = tile(b_ref, j, p)
        k = tile(k_ref, j, p)
        a_n = tile(a_ref, j, p) * jnp.exp(cum - w)
        st.append(dict(
            ar=jnp.concatenate([a_n, tile(r_ref, j, p) * jnp.exp(cum)], axis=0).astype(BF16),
            v_e=expand(tile(v_ref, j, p)),
            bk_t=jnp.concatenate([expand(b * e_neg), expand(k * e_neg)], axis=0),
            bk_h=jnp.concatenate([expand(b * e_rem), expand(k * e_rem)], axis=0),
            w_tot=jnp.exp(tot)))
    for d in st:
        sc = _dot_nt(d['ar'], d['bk_t'])
        d['t'] = jnp.where(strict, sc[0:L, 0:PAIR], 0.0)
        d['t_k'] = jnp.concatenate([jnp.where(strict, sc[0:L, PAIR:2 * PAIR], 0.0),
                                    jnp.where(incl, sc[L:2 * L, PAIR:2 * PAIR], 0.0)], axis=0)
        d['r_all'] = jnp.concatenate([d['ar'][L:2 * L],
                                      jnp.where(incl, sc[L:2 * L, 0:PAIR], 0.0).astype(BF16)], axis=1)
    for d in st:
        kv = _dot(d['t_k'], d['v_e'])
        d['takv'] = kv[0:L]
        d['okv'] = kv[L:2 * L]
        d['m'] = eye_wide + d['t']
        d['t'] = _dot(d['t'], expand(d['t']))
    for step in range(5):
        for d in st:
            if step < 4:
                both = _dot(d['t'], jnp.concatenate([expand(d['t']), expand(d['m'])], axis=1))
                d['t'] = both[:, 0:PAIR]
                d['m'] = d['m'] + both[:, PAIR:2 * PAIR]
            else:
                d['m'] = d['m'] + _dot(d['t'], expand(d['m']))
    for d in st:
        d['pq'] = _dot(d['m'], jnp.concatenate([expand(d['ar'][0:L]), expand(d['takv'])], axis=1))

    h = [h_ref[p] for p in range(N_PAIRS)]
    for j in range(n_chunks):
        ds = st[j * N_PAIRS:(j + 1) * N_PAIRS]
        hb = [hp.astype(BF16) for hp in h]
        u_e = [expand(_dot(d['pq'][:, 0:PAIR], hb[p]) + d['pq'][:, PAIR:2 * PAIR])
               for p, d in enumerate(ds)]
        for p, d in enumerate(ds):
            o_ref[0, j * L:(j + 1) * L, p * PAIR:(p + 1) * PAIR] = d['okv'] + _dot(
                d['r_all'], jnp.concatenate([hb[p], u_e[p]], axis=0))
        for p, d in enumerate(ds):
            h_add = _dot_tn(d['bk_h'], jnp.concatenate([u_e[p], d['v_e']], axis=0))
            w_col = jnp.transpose(jnp.broadcast_to(d['w_tot'], (PAIR, PAIR)))
            h[p] = h[p] * w_col + h_add
    for p in range(N_PAIRS):
        h_ref[p] = h[p]


def _rwkv_call(r, w, k, v, a, b):
    bsz, s, _ = r.shape
    spec = pl.BlockSpec((1, RWKV_TILE, D_RWKV), lambda bi, c: (bi, c, 0))
    return pl.pallas_call(
        _rwkv_kernel,
        grid=(bsz, s // RWKV_TILE),
        in_specs=[spec] * 6,
        out_specs=spec,
        out_shape=jax.ShapeDtypeStruct((bsz, s, D_RWKV), F32),
        scratch_shapes=[pltpu.VMEM((N_PAIRS, PAIR, PAIR), F32)],
        compiler_params=pltpu.CompilerParams(
            dimension_semantics=("parallel", "arbitrary"), vmem_limit_bytes=VMEM_LIMIT),
        name="rwkv",
    )(r, w, k, v, a, b)


def _layer_norm(x, g, b):
    mu = jnp.mean(x, axis=-1, keepdims=True)
    xc = x - mu
    var = jnp.mean(xc * xc, axis=-1, keepdims=True)
    return xc * lax.rsqrt(var + LN_EPS) * g + b


def _post_kernel(x_ref, pool_ref, o_ref, r_ref, k_ref, v_ref, g_ref,
                 lnxg_ref, lnxb_ref, rk_ref, ones_ref, wop_ref, wor_ref, ln1g_ref, ln1b_ref,
                 rw_hi_ref, rw_lo_ref, rb_ref, tri_ref, eg_ref, eu_ref, ed_ref,
                 h_o, meta_t_o, cnt_o, eg_o, eu_o, ed_o, base_ref):
    i = pl.program_id(0)
    ts = x_ref.shape[0]

    @pl.when(i == 0)
    def _():
        base_ref[...] = jnp.zeros_like(base_ref)

    eg_o[...] = eg_ref[...].astype(BF16)
    eu_o[...] = eu_ref[...].astype(BF16)
    ed_o[...] = ed_ref[...].astype(BF16)

    ones = ones_ref[...]
    o = o_ref[...]
    mu = _head_sum(o, ones) * (1.0 / HEAD)
    oc = o - mu
    var = _head_sum(oc * oc, ones) * (1.0 / HEAD)
    y = oc * lax.rsqrt(var + LNX_EPS) * lnxg_ref[...] + lnxb_ref[...]
    v = v_ref[...]
    bonus = _head_sum(r_ref[...] * k_ref[...] * rk_ref[...], ones) * v
    y = (y + bonus) * g_ref[...]
    mixed = _dot(pool_ref[...], wop_ref[...]) + _dot(y, wor_ref[...])
    h = _layer_norm(ALPHA * x_ref[...] + mixed, ln1g_ref[...], ln1b_ref[...])
    h_o[:, 0:D_MODEL] = h

    h_hi = h.astype(BF16)
    h_lo = (h - h_hi.astype(F32)).astype(BF16)
    logits = (jnp.dot(h_hi, rw_hi_ref[...], preferred_element_type=F32)
              + jnp.dot(h_lo, rw_hi_ref[...], preferred_element_type=F32)
              + jnp.dot(h_hi, rw_lo_ref[...], preferred_element_type=F32)) + rb_ref[...]
    lane = lax.broadcasted_iota(jnp.int32, (ts, LANES), 1)
    neg = -jnp.inf
    gl = jnp.where(lane < N_GROUPS, logits, neg)
    gmax = jnp.max(gl, axis=-1, keepdims=True)
    g_idx = jnp.min(jnp.where(gl == gmax, lane, LANES), axis=-1, keepdims=True)
    g_top_p = 1.0 / jnp.sum(jnp.exp(gl - gmax), axis=-1, keepdims=True)
    lo_lane = EXPERT_LANE0 + g_idx * EXPERTS_PER_GROUP
    el = jnp.where((lane >= lo_lane) & (lane < lo_lane + EXPERTS_PER_GROUP), logits, neg)
    m1 = jnp.max(el, axis=-1, keepdims=True)
    i1 = jnp.min(jnp.where(el == m1, lane, LANES), axis=-1, keepdims=True)
    el2 = jnp.where(lane == i1, neg, el)
    m2 = jnp.max(el2, axis=-1, keepdims=True)
    i2 = jnp.min(jnp.where(el2 == m2, lane, LANES), axis=-1, keepdims=True)
    e21 = jnp.exp(m2 - m1)
    wgt1 = g_top_p / (1.0 + e21)
    wgt2 = g_top_p * e21 / (1.0 + e21)

    l1 = i1 - lo_lane
    l2 = i2 - lo_lane
    lo = jnp.minimum(l1, l2)
    hi = jnp.maximum(l1, l2)
    cat = g_idx * PAIRS_PER_GROUP + jnp.right_shift(lo * (2 * EXPERTS_PER_GROUP - 1 - lo), 1) + (hi - lo - 1)
    in_cat = lane == cat
    onehot = jnp.where(in_cat, 1.0, 0.0)
    before = jnp.dot(tri_ref[...], onehot.astype(BF16), preferred_element_type=F32)
    posn = base_ref[0:1, :] + before
    rank = jnp.sum(jnp.where(in_cat, posn, 0.0), axis=-1, keepdims=True)
    new_base = base_ref[0:1, :] + jnp.sum(onehot, axis=0, keepdims=True)
    base_ref[...] = jnp.broadcast_to(new_base, base_ref.shape)
    cnt_o[...] = jnp.broadcast_to(new_base, cnt_o.shape)

    meta = jnp.where(lane == l1, wgt1, jnp.where(lane == l2, wgt2, 0.0))
    meta = jnp.where(lane == META_CAT, cat.astype(F32), jnp.where(lane == META_RANK, rank, meta))
    h_o[:, D_MODEL:D_EXT] = meta
    meta_t_o[...] = jnp.transpose(meta)[META_CAT:META_CAT + 8, :]


def _post_call(x2, pool2, o2, r2, k2, v2, g2, lnx_g, lnx_b, rk, ones_bd, wo_pool, wo_rwkv,
               ln1_g, ln1_b, rw_hi, rw_lo, rb, tri, exp_gate, exp_up, exp_down):
    n = x2.shape[0]
    n_steps = n // TS
    assert N_EXPERTS % n_steps == 0
    epb = N_EXPERTS // n_steps
    full = lambda arr: pl.BlockSpec(arr.shape, lambda i: (0,) * arr.ndim)
    tok = lambda width: pl.BlockSpec((TS, width), lambda i: (i, 0))
    wblk = lambda arr: pl.BlockSpec((epb,) + arr.shape[1:], lambda i: (i, 0, 0))
    wout = lambda arr: jax.ShapeDtypeStruct(arr.shape, BF16)
    params = (lnx_g, lnx_b, rk, ones_bd, wo_pool, wo_rwkv, ln1_g, ln1_b, rw_hi, rw_lo, rb, tri)
    experts = (exp_gate, exp_up, exp_down)
    return pl.pallas_call(
        _post_kernel,
        grid=(n_steps,),
        in_specs=([tok(D_MODEL), tok(D_POOL)] + [tok(D_RWKV)] * 5 + [full(a) for a in params]
                  + [wblk(a) for a in experts]),
        out_specs=[tok(D_EXT), pl.BlockSpec((8, TS), lambda i: (0, i)),
                   pl.BlockSpec((8, LANES), lambda i: (0, 0))] + [wblk(a) for a in experts],
        out_shape=[jax.ShapeDtypeStruct((n, D_EXT), F32), jax.ShapeDtypeStruct((8, n), F32),
                   jax.ShapeDtypeStruct((8, LANES), F32)] + [wout(a) for a in experts],
        scratch_shapes=[pltpu.VMEM((8, LANES), F32)],
        compiler_params=pltpu.CompilerParams(
            dimension_semantics=("arbitrary",), vmem_limit_bytes=VMEM_LIMIT),
        name="post",
    )(x2, pool2, o2, r2, k2, v2, g2, *params, *experts)


def _start_rows(ts, row_copy):
    def issue(g, carry):
        for u in range(DMA_UNROLL):
            row_copy(g * DMA_UNROLL + u).start(priority=u % 2)
        return carry

    lax.fori_loop(0, ts // DMA_UNROLL, issue, 0)


def _dispatch_kernel(pad_start_ref, pad_cnt_ref, n_used_ref, dest_ref, h_ref, xs_ref, zero_ref,
                     sem, zsem):
    i = pl.program_id(0)
    ts = h_ref.shape[0]
    n_blocks = xs_ref.shape[0] // BM

    def row_copy(t):
        return pltpu.make_async_copy(h_ref.at[pl.ds(t, 1)],
                                     xs_ref.at[pl.ds(dest_ref[0, 0, t], 1)], sem)

    def zero_row_copy(row):
        return pltpu.make_async_copy(zero_ref.at[pl.ds(0, 1)], xs_ref.at[pl.ds(row, 1)], zsem)

    def zero_block_copy(blk):
        return pltpu.make_async_copy(zero_ref, xs_ref.at[pl.ds(pl.multiple_of(blk * BM, BM), BM)], zsem)

    @pl.when(i == 0)
    def _():
        zero_ref[...] = jnp.zeros_like(zero_ref)

        def fill(e, carry):
            start = pad_start_ref[e]

            def one(j, c):
                zero_row_copy(start + j).start()
                return c

            return lax.fori_loop(0, pad_cnt_ref[e], one, carry)

        def fill_wait(e, carry):
            def one(j, c):
                zero_row_copy(0).wait()
                return c

            return lax.fori_loop(0, pad_cnt_ref[e], one, carry)

        def tail(blk, carry):
            zero_block_copy(blk).start()
            return carry

        def tail_wait(blk, carry):
            zero_block_copy(blk).wait()
            return carry

        lax.fori_loop(0, N_GROUPS, fill, 0)
        lax.fori_loop(n_used_ref[0], n_blocks, tail, 0)
        lax.fori_loop(0, N_GROUPS, fill_wait, 0)
        lax.fori_loop(n_used_ref[0], n_blocks, tail_wait, 0)

    _start_rows(ts, row_copy)
    pltpu.make_async_copy(h_ref, xs_ref.at[pl.ds(0, ts)], sem).wait()


def _dispatch_call(pad_start, pad_cnt, n_used, dest, h_ext, m_pad):
    n = h_ext.shape[0]
    grid_spec = pltpu.PrefetchScalarGridSpec(
        num_scalar_prefetch=3,
        grid=(n // TS,),
        in_specs=[pl.BlockSpec((1, 1, TS), lambda i, *_: (i, 0, 0), memory_space=pltpu.SMEM),
                  pl.BlockSpec((TS, D_EXT), lambda i, *_: (i, 0))],
        out_specs=pl.BlockSpec(memory_space=pl.ANY),
        scratch_shapes=[pltpu.VMEM((BM, D_EXT), F32), pltpu.SemaphoreType.DMA(()),
                        pltpu.SemaphoreType.DMA(())],
    )
    return pl.pallas_call(
        _dispatch_kernel,
        grid_spec=grid_spec,
        out_shape=jax.ShapeDtypeStruct((m_pad, D_EXT), F32),
        compiler_params=pltpu.CompilerParams(
            dimension_semantics=("arbitrary",), vmem_limit_bytes=VMEM_LIMIT,
            disable_bounds_checks=True),
        name="dispatch",
    )(pad_start, pad_cnt, n_used, dest, h_ext)


def _expert_kernel(blk_g_ref, n_used_ref, act_ref, xs_ref, wg_ref, wu_ref, wd_ref, ys_ref, xb_ref):
    i = pl.program_id(0)

    @pl.when(i < n_used_ref[0])
    def _():
        xb_ref[...] = xs_ref[:, 0:D_MODEL].astype(BF16)
        ys_ref[...] = jnp.zeros_like(ys_ref)
        for e in range(EXPERTS_PER_GROUP):
            @pl.when(act_ref[i * EXPERTS_PER_GROUP + e] != 0)
            def _(e=e):
                xb = xb_ref[...]
                gate = jnp.dot(xb, wg_ref[0, e], preferred_element_type=F32)
                up = jnp.dot(xb, wu_ref[0, e], preferred_element_type=F32)
                w_e = xs_ref[:, D_MODEL + e:D_MODEL + e + 1]
                hid = jnp.where(w_e != 0.0, gate * _sigmoid(gate) * up * w_e, 0.0)
                ys_ref[...] += jnp.dot(hid.astype(BF16), wd_ref[0, e * D_EXPERT:(e + 1) * D_EXPERT],
                                       preferred_element_type=F32)

    @pl.when(i >= n_used_ref[0])
    def _():
        ys_ref[...] = jnp.zeros_like(ys_ref)


def _expert_call(blk_g, n_used, act, xs, wg, wu, wd):
    m_pad = xs.shape[0]
    n_blocks = m_pad // BM
    row_blk = lambda i, bg, nu, ac: (jnp.minimum(i, nu[0] - 1), 0)
    out_blk = lambda i, bg, nu, ac: (i, 0)
    grp = lambda i, bg, nu: bg[jnp.minimum(i, nu[0] - 1)]
    grid_spec = pltpu.PrefetchScalarGridSpec(
        num_scalar_prefetch=3,
        grid=(n_blocks,),
        in_specs=[pl.BlockSpec((BM, D_EXT), row_blk),
                  pl.BlockSpec((1, EXPERTS_PER_GROUP, D_MODEL, D_EXPERT),
                               lambda i, bg, nu, ac: (grp(i, bg, nu), 0, 0, 0)),
                  pl.BlockSpec((1, EXPERTS_PER_GROUP, D_MODEL, D_EXPERT),
                               lambda i, bg, nu, ac: (grp(i, bg, nu), 0, 0, 0)),
                  pl.BlockSpec((1, EXPERTS_PER_GROUP * D_EXPERT, D_MODEL),
                               lambda i, bg, nu, ac: (grp(i, bg, nu), 0, 0))],
        out_specs=pl.BlockSpec((BM, D_MODEL), out_blk),
        scratch_shapes=[pltpu.VMEM((BM, D_MODEL), BF16)],
    )
    return pl.pallas_call(
        _expert_kernel,
        grid_spec=grid_spec,
        out_shape=jax.ShapeDtypeStruct((m_pad, D_MODEL), F32),
        compiler_params=pltpu.CompilerParams(
            dimension_semantics=("arbitrary",), vmem_limit_bytes=VMEM_LIMIT),
        name="experts",
    )(blk_g, n_used, act, xs, wg, wu, wd)


def _combine_kernel(dest_ref, dest_next_ref, h_ref, ys_ref, g_ref, b_ref, out_ref, y_ref, sem):
    i = pl.program_id(0)
    n_steps = pl.num_programs(0)
    ts = h_ref.shape[0]

    def gather(idx_ref, buf):
        _start_rows(ts, lambda t: pltpu.make_async_copy(
            ys_ref.at[pl.ds(idx_ref[0, 0, t], 1)], y_ref.at[buf, pl.ds(t, 1)], sem.at[buf]))

    cur = lax.rem(i, 2)

    @pl.when(i == 0)
    def _():
        gather(dest_ref, 0)

    @pl.when(i + 1 < n_steps)
    def _():
        gather(dest_next_ref, 1 - cur)

    pltpu.make_async_copy(ys_ref.at[pl.ds(0, ts)], y_ref.at[cur], sem.at[cur]).wait()
    out_ref[...] = _layer_norm(ALPHA * h_ref[...] + y_ref[cur], g_ref[...], b_ref[...])


def _combine_call(dest, h_ext, ys, ln2_g, ln2_b):
    n = h_ext.shape[0]
    full = lambda arr: pl.BlockSpec(arr.shape, lambda i: (0,) * arr.ndim)
    return pl.pallas_call(
        _combine_kernel,
        grid=(n // TS,),
        in_specs=[pl.BlockSpec((1, 1, TS), lambda i: (i, 0, 0), memory_space=pltpu.SMEM),
                  pl.BlockSpec((1, 1, TS), lambda i: (jnp.minimum(i + 1, n // TS - 1), 0, 0),
                               memory_space=pltpu.SMEM),
                  pl.BlockSpec((TS, D_MODEL), lambda i: (i, 0)),
                  pl.BlockSpec(memory_space=pl.ANY),
                  full(ln2_g), full(ln2_b)],
        out_specs=pl.BlockSpec((TS, D_MODEL), lambda i: (i, 0)),
        out_shape=jax.ShapeDtypeStruct((n, D_MODEL), F32),
        scratch_shapes=[pltpu.VMEM((2, TS, D_MODEL), F32), pltpu.SemaphoreType.DMA((2,))],
        compiler_params=pltpu.CompilerParams(
            dimension_semantics=("arbitrary",), vmem_limit_bytes=VMEM_LIMIT,
            disable_bounds_checks=True),
        name="combine",
    )(dest, dest, h_ext, ys, ln2_g, ln2_b)


def _block_diag(blocks):
    g, c, d = blocks.shape
    eye = jnp.eye(g, dtype=blocks.dtype)
    return (eye[:, None, :, None] * blocks[:, :, None, :]).reshape(g * c, g * d)


def _layer(x, w_in, pool_w, pool_scale, mu_shift, w0, w_up, a0, a_up, g_up, k_k, k_a, r_k,
           lnx_g, lnx_b, w_out, ln1_g, ln1_b, router_group, router_group_b, router_expert,
           router_expert_b, exp_gate, exp_up, exp_down, ln2_g, ln2_b):
    b, s, d = x.shape
    n = b * s
    row2 = lambda t: t.reshape(1, -1)

    ones_bd = _block_diag(jnp.ones((2, HEAD, HEAD), BF16))
    lora_w = _block_diag(jnp.stack([w_up, a_up])).astype(BF16)
    poolw_bd = _block_diag(pool_w).astype(BF16)
    rw = jnp.zeros((D_MODEL, LANES), F32)
    rw = rw.at[:, 0:N_GROUPS].set(router_group).at[:, EXPERT_LANE0:EXPERT_LANE0 + N_EXPERTS].set(router_expert)
    rw_hi = rw.astype(BF16)
    rw_lo = (rw - rw_hi.astype(F32)).astype(BF16)
    rb = jnp.zeros((1, LANES), F32)
    rb = rb.at[0, 0:N_GROUPS].set(router_group_b).at[0, EXPERT_LANE0:EXPERT_LANE0 + N_EXPERTS].set(router_expert_b)
    tri = (lax.broadcasted_iota(jnp.int32, (TS, TS), 0) > lax.broadcasted_iota(jnp.int32, (TS, TS), 1)).astype(BF16)

    pool_out, r, w, k, v, a_vec, b_vec, g = _prep_call(
        x, w_in.astype(BF16), row2(mu_shift), row2(w0), row2(a0), lora_w, g_up.astype(BF16),
        row2(k_k), row2(k_a), poolw_bd, row2(pool_scale), ones_bd)
    o = _rwkv_call(r, w, k, v, a_vec, b_vec)

    flat = lambda t: t.reshape(n, t.shape[-1])
    w_out_bf = w_out.astype(BF16)
    h_ext, meta_t, counts, eg_bf, eu_bf, ed_bf = _post_call(
        flat(x), flat(pool_out), flat(o), flat(r), flat(k), flat(v), flat(g),
        row2(lnx_g), row2(lnx_b), row2(r_k), ones_bd, w_out_bf[0:D_POOL], w_out_bf[D_POOL:],
        row2(ln1_g), row2(ln1_b), rw_hi, rw_lo, rb, tri, exp_gate, exp_up, exp_down)

    cnt_cat = counts[0, 0:N_CATS].astype(jnp.int32).reshape(N_GROUPS, PAIRS_PER_GROUP)
    cnt = jnp.sum(cnt_cat, axis=1)
    padded = (cnt + BM - 1) // BM * BM
    pend = jnp.cumsum(padded)
    pstart = pend - padded
    cat_start = (pstart[:, None] + jnp.cumsum(cnt_cat, axis=1) - cnt_cat).reshape(N_CATS)
    cnt_cat = cnt_cat.reshape(N_CATS)
    cats = meta_t[0].astype(jnp.int32)
    onehot = cats[:, None] == jnp.arange(N_CATS, dtype=jnp.int32)
    dest = jnp.sum(jnp.where(onehot, cat_start, 0), axis=-1) + meta_t[1].astype(jnp.int32)
    dest = dest.reshape(n // TS, 1, TS)
    m_pad = n + N_GROUPS * BM
    n_blocks = m_pad // BM
    blk_start = jnp.arange(n_blocks, dtype=jnp.int32) * BM
    blk_g = jnp.minimum(jnp.sum(blk_start[:, None] >= pend[None, :], axis=1), N_GROUPS - 1).astype(jnp.int32)
    n_used = (pend[-1:] // BM).astype(jnp.int32)
    overlap = ((cat_start[None, :] < blk_start[:, None] + BM)
               & (cat_start[None, :] + cnt_cat[None, :] > blk_start[:, None])
               & (cnt_cat[None, :] > 0))
    act = jnp.any(overlap[:, :, None] & _PAIR_MEMBER[None], axis=1).astype(jnp.int32).reshape(-1)

    grouped = lambda wts: wts.reshape(N_GROUPS, EXPERTS_PER_GROUP, *wts.shape[1:])
    wd = ed_bf.reshape(N_GROUPS, EXPERTS_PER_GROUP * D_EXPERT, D_MODEL)
    xs = _dispatch_call(pstart + cnt, padded - cnt, n_used, dest, h_ext, m_pad)
    ys = _expert_call(blk_g, n_used, act, xs, grouped(eg_bf), grouped(eu_bf), wd)
    out = _combine_call(dest, h_ext, ys, row2(ln2_g), row2(ln2_b))
    return out.reshape(b, s, d)


def kernel(x, w_in, pool_w, pool_scale, mu_shift, w0, w_up, a0, a_up, g_up, k_k, k_a, r_k, lnx_g, lnx_b, w_out, ln1_g, ln1_b, router_group, router_group_b, router_expert, router_expert_b, exp_gate, exp_up, exp_down, ln2_g, ln2_b):
    depth = w_in.shape[0]
    for l in range(depth):
        x = _layer(x, w_in[l], pool_w[l], pool_scale[l], mu_shift[l], w0[l], w_up[l], a0[l],
                   a_up[l], g_up[l], k_k[l], k_a[l], r_k[l], lnx_g[l], lnx_b[l], w_out[l],
                   ln1_g[l], ln1_b[l], router_group[l], router_group_b[l], router_expert[l],
                   router_expert_b[l], exp_gate[l], exp_up[l], exp_down[l], ln2_g[l], ln2_b[l])
    return x
```

```python
import numpy as np
import jax
import jax.numpy as jnp
from jax import lax
from jax.experimental import pallas as pl
from jax.experimental.pallas import tpu as pltpu

F32 = jnp.float32
BF16 = jnp.bfloat16

D_MODEL = 1024
D_POOL = 256
POOL_WINDOWS = (2, 4, 8, 16)
POOL_GROUP = 64
POOL_HALO = 16
D_RWKV = 768
HEAD = 64
D_DECAY_LORA = 64
D_AAA_LORA = 64
D_GATE_LORA = 128
D_RWKV_IN = 3 * D_RWKV + D_DECAY_LORA + D_AAA_LORA + D_GATE_LORA
D_IN = D_POOL + D_RWKV_IN
N_GROUPS = 4
EXPERTS_PER_GROUP = 8
N_EXPERTS = 32
D_EXPERT = 256
LN_EPS = 1e-5
LNX_EPS = 64e-5
ALPHA = 2.0 ** 0.25
DECAY_SCALE = 0.6065306597126334
KK_NORM_FLOOR = 1e-24

LANES = 128
VMEM_LIMIT = 56 * 1024 * 1024

PAIR = 2 * HEAD
N_PAIRS = D_RWKV // PAIR
CHUNK = 64
RWKV_TILE = 4 * CHUNK
TS = 512
BM = 256
DMA_UNROLL = 8
EXPERT_LANE0 = 32
D_EXT = D_MODEL + LANES
META_CAT = 8
META_RANK = 9
PAIRS_PER_GROUP = EXPERTS_PER_GROUP * (EXPERTS_PER_GROUP - 1) // 2
N_CATS = N_GROUPS * PAIRS_PER_GROUP


def _pair_member():
    member = np.zeros((N_CATS, EXPERTS_PER_GROUP), bool)
    for g in range(N_GROUPS):
        pair = 0
        for lo in range(EXPERTS_PER_GROUP):
            for hi in range(lo + 1, EXPERTS_PER_GROUP):
                member[g * PAIRS_PER_GROUP + pair, [lo, hi]] = True
                pair += 1
    return member


_PAIR_MEMBER = _pair_member()


def _dot(a, b):
    return jnp.dot(a.astype(BF16), b.astype(BF16), preferred_element_type=F32)


def _dot_nt(a, b):
    return lax.dot_general(a.astype(BF16), b.astype(BF16), (((1,), (1,)), ((), ())),
                           preferred_element_type=F32)


def _dot_tn(a, b):
    return lax.dot_general(a.astype(BF16), b.astype(BF16), (((0,), (0,)), ((), ())),
                           preferred_element_type=F32)


def _head_sum(x, ones_bd):
    parts = [_dot(x[:, p * PAIR:(p + 1) * PAIR], ones_bd) for p in range(N_PAIRS)]
    return jnp.concatenate(parts, axis=1)


def _sigmoid(x):
    return 1.0 / (1.0 + jnp.exp(-x))


def _prep_kernel(x_ref, win_ref, mu_ref, w0_ref, a0_ref, lora_ref, gup_ref, kk_ref, ka_ref,
                 poolw_ref, pools_ref, ones_ref, rk_ref,
                 pool_o, r_o, w_o, k_o, v_o, a_o, b_o, g_o, bv_o,
                 proj_ref, halo_ref):
    i = pl.program_id(1)
    ts = x_ref.shape[1]

    @pl.when(i == 0)
    def _():
        halo_ref[...] = jnp.zeros_like(halo_ref)

    proj_ref[...] = jnp.dot(x_ref[0].astype(BF16), win_ref[...], preferred_element_type=F32)

    row = lax.broadcasted_iota(jnp.int32, (ts, 1), 0)

    def shifted(off, width):
        z = proj_ref[:, off:off + width]
        prev = jnp.where(row == 0, halo_ref[POOL_HALO - 1:POOL_HALO, off:off + width],
                         pltpu.roll(z, 1, 0))
        return z + (prev - z) * mu_ref[:, off - D_POOL:off - D_POOL + width]

    p = proj_ref[:, 0:D_POOL]
    ext = jnp.concatenate([halo_ref[:, 0:D_POOL], p], axis=0)
    s2 = ext + pltpu.roll(ext, 1, 0)
    s4 = s2 + pltpu.roll(s2, 2, 0)
    s8 = s4 + pltpu.roll(s4, 4, 0)
    s16 = s8 + pltpu.roll(s8, 8, 0)
    lane = lax.broadcasted_iota(jnp.int32, (ts, D_POOL), 1)
    grp = lane // POOL_GROUP
    wsum = jnp.where(grp == 0, s2[POOL_HALO:], jnp.where(grp == 1, s4[POOL_HALO:],
                     jnp.where(grp == 2, s8[POOL_HALO:], s16[POOL_HALO:])))
    win = jnp.where(grp == 0, 2.0, jnp.where(grp == 1, 4.0, jnp.where(grp == 2, 8.0, 16.0)))
    pos = (i * ts + row + 1).astype(F32)
    diff = wsum / jnp.minimum(pos, win) - p
    pool_o[0] = _dot(diff, poolw_ref[...]) * pools_ref[...]

    o = D_POOL
    r = shifted(o, D_RWKV)
    k = shifted(o + D_RWKV, D_RWKV)
    v = shifted(o + 2 * D_RWKV, D_RWKV)
    lw = shifted(o + 3 * D_RWKV, D_DECAY_LORA + D_AAA_LORA)
    gd = shifted(o + 3 * D_RWKV + D_DECAY_LORA + D_AAA_LORA, D_GATE_LORA)

    lane128 = lax.broadcasted_iota(jnp.int32, (ts, LANES), 1)
    lora_in = jnp.where(lane128 < D_DECAY_LORA, jnp.tanh(lw), lw)
    lora = _dot(lora_in, lora_ref[...])
    w_o[0] = -DECAY_SCALE * _sigmoid(w0_ref[...] + lora[:, 0:D_RWKV])
    eta = _sigmoid(a0_ref[...] + lora[:, D_RWKV:2 * D_RWKV])
    g_o[0] = _dot(_sigmoid(gd), gup_ref[...]).astype(g_o.dtype)
    kk = k * kk_ref[...]
    ss = _head_sum(kk * kk, ones_ref[...])
    kkn = kk * lax.rsqrt(jnp.maximum(ss, KK_NORM_FLOOR))
    k_mod = k * (1.0 + (eta - 1.0) * ka_ref[...])
    r_o[0] = r
    v_o[0] = v.astype(v_o.dtype)
    k_o[0] = k_mod
    bv_o[0] = (_head_sum(r * k_mod * rk_ref[...], ones_ref[...]) * v).astype(bv_o.dtype)
    a_o[0] = -kkn
    b_o[0] = kkn * eta

    halo_ref[...] = proj_ref[ts - POOL_HALO:ts, :]


def _prep_call(x, win_bf, mu, w0, a0, lora_w, gup_bf, k_k, k_a, poolw_bd, pool_scale, ones_bd, rk):
    b, s, _ = x.shape
    grid = (b, s // TS)
    full = lambda arr: pl.BlockSpec(arr.shape, lambda bi, i: (0,) * arr.ndim)
    tok = lambda width: pl.BlockSpec((1, TS, width), lambda bi, i: (bi, i, 0))
    dtypes = (F32, F32, F32, BF16, F32, F32, BF16, BF16)
    outs = ([jax.ShapeDtypeStruct((b, s, D_POOL), F32)]
            + [jax.ShapeDtypeStruct((b, s, D_RWKV), dt) for dt in dtypes])
    params = (win_bf, mu, w0, a0, lora_w, gup_bf, k_k, k_a, poolw_bd, pool_scale, ones_bd, rk)
    return pl.pallas_call(
        _prep_kernel,
        grid=grid,
        in_specs=[tok(D_MODEL)] + [full(a) for a in params],
        out_specs=[tok(D_POOL)] + [tok(D_RWKV)] * 8,
        out_shape=outs,
        scratch_shapes=[pltpu.VMEM((TS, D_IN), F32), pltpu.VMEM((POOL_HALO, D_IN), F32)],
        compiler_params=pltpu.CompilerParams(
            dimension_semantics=("parallel", "arbitrary"), vmem_limit_bytes=VMEM_LIMIT),
        name="prep",
    )(x, *params)


def _rwkv_kernel(r_ref, w_ref, k_ref, v_ref, a_ref, b_ref, o_ref, h_ref):
    c = pl.program_id(1)

    @pl.when(c == 0)
    def _():
        h_ref[...] = jnp.zeros_like(h_ref)

    L = CHUNK
    n_chunks = r_ref.shape[1] // L
    row = lax.broadcasted_iota(jnp.int32, (L, PAIR), 0)
    lane = lax.broadcasted_iota(jnp.int32, (L, PAIR), 1)
    head0 = lane < HEAD
    strict = row > (lane & (L - 1))
    incl = row >= (lane & (L - 1))
    eye_wide = jnp.where(row == (lane & (L - 1)), 1.0, 0.0)

    def expand(x):
        xb = x.astype(BF16)
        zero = jnp.zeros_like(xb)
        return jnp.concatenate([jnp.where(head0, xb, zero), jnp.where(head0, zero, xb)], axis=0)

    def tile(ref, j, p):
        return ref[0, j * L:(j + 1) * L, p * PAIR:(p + 1) * PAIR]

    streams = [(j, p) for j in range(n_chunks) for p in range(N_PAIRS)]
    st = []
    for j, p in streams:
        w = tile(w_ref, j, p)
        cum = w
        for sh in (1, 2, 4, 8, 16, 32):
            cum = cum + jnp.where(row >= sh, pltpu.roll(cum, sh, 0), 0.0)
        tot = cum[L - 1:L, :]
        e_neg = jnp.exp(-cum)
        e_rem = jnp.exp(tot - cum)
        b = tile(b_ref, j, p)
        k = tile(k_ref, j, p)
        a_n = tile(a_ref, j, p) * jnp.exp(cum - w)
        st.append(dict(
            ar=jnp.concatenate([a_n, tile(r_ref, j, p) * jnp.exp(cum)], axis=0).astype(BF16),
            v_e=expand(tile(v_ref, j, p)),
            bk_t=jnp.concatenate([expand(b * e_neg), expand(k * e_neg)], axis=0),
            bk_h=jnp.concatenate([expand(b * e_rem), expand(k * e_rem)], axis=0),
            w_tot=jnp.exp(tot)))
    for d in st:
        sc = _dot_nt(d['ar'], d['bk_t'])
        d['t'] = jnp.where(strict, sc[0:L, 0:PAIR], 0.0)
        d['t_ak'] = jnp.where(strict, sc[0:L, PAIR:2 * PAIR], 0.0)
        d['r_all'] = jnp.concatenate([d['ar'][L:2 * L],
                                      jnp.where(incl, sc[L:2 * L, 0:PAIR], 0.0).astype(BF16),
                                      jnp.where(incl, sc[L:2 * L, PAIR:2 * PAIR], 0.0).astype(BF16)],
                                     axis=1)
    for d in st:
        d['takv'] = _dot(d['t_ak'], d['v_e'])
        d['m'] = eye_wide + d['t']
        d['t'] = _dot(d['t'], expand(d['t']))
    for step in range(5):
        for d in st:
            if step < 4:
                both = _dot(d['t'], jnp.concatenate([expand(d['t']), expand(d['m'])], axis=1))
                d['t'] = both[:, 0:PAIR]
                d['m'] = d['m'] + both[:, PAIR:2 * PAIR]
            else:
                d['m'] = d['m'] + _dot(d['t'], expand(d['m']))
    for d in st:
        d['pq'] = _dot(d['m'], jnp.concatenate([expand(d['ar'][0:L]), expand(d['takv'])], axis=1))

    h = [h_ref[p] for p in range(N_PAIRS)]
    for j in range(n_chunks):
        ds = st[j * N_PAIRS:(j + 1) * N_PAIRS]
        hb = [hp.astype(BF16) for hp in h]
        u_e = [expand(_dot(d['pq'][:, 0:PAIR], hb[p]) + d['pq'][:, PAIR:2 * PAIR])
               for p, d in enumerate(ds)]
        for p, d in enumerate(ds):
            o_ref[0, j * L:(j + 1) * L, p * PAIR:(p + 1) * PAIR] = _dot(
                d['r_all'], jnp.concatenate([hb[p], u_e[p], d['v_e']], axis=0))
        for p, d in enumerate(ds):
            h_add = _dot_tn(d['bk_h'], jnp.concatenate([u_e[p], d['v_e']], axis=0))
            w_col = jnp.transpose(jnp.broadcast_to(d['w_tot'], (PAIR, PAIR)))
            h[p] = h[p] * w_col + h_add
    for p in range(N_PAIRS):
        h_ref[p] = h[p]


def _rwkv_call(r, w, k, v, a, b):
    bsz, s, _ = r.shape
    spec = pl.BlockSpec((1, RWKV_TILE, D_RWKV), lambda bi, c: (bi, c, 0))
    return pl.pallas_call(
        _rwkv_kernel,
        grid=(bsz, s // RWKV_TILE),
        in_specs=[spec] * 6,
        out_specs=spec,
        out_shape=jax.ShapeDtypeStruct((bsz, s, D_RWKV), F32),
        scratch_shapes=[pltpu.VMEM((N_PAIRS, PAIR, PAIR), F32)],
        compiler_params=pltpu.CompilerParams(
            dimension_semantics=("parallel", "arbitrary"), vmem_limit_bytes=VMEM_LIMIT),
        name="rwkv",
    )(r, w, k, v, a, b)


def _layer_norm(x, g, b):
    mu = jnp.mean(x, axis=-1, keepdims=True)
    xc = x - mu
    var = jnp.mean(xc * xc, axis=-1, keepdims=True)
    return xc * lax.rsqrt(var + LN_EPS) * g + b


def _post_kernel(x_ref, pool_ref, o_ref, bv_ref, g_ref,
                 lnxg_ref, lnxb_ref, ones_ref, wop_ref, wor_ref, ln1g_ref, ln1b_ref,
                 rw_hi_ref, rw_lo_ref, rb_ref, tri_ref, eg_ref, eu_ref, ed_ref,
                 h_o, meta_t_o, cnt_o, eg_o, eu_o, ed_o, base_ref):
    i = pl.program_id(0)
    ts = x_ref.shape[0]

    @pl.when(i == 0)
    def _():
        base_ref[...] = jnp.zeros_like(base_ref)

    eg_o[...] = eg_ref[...].astype(BF16)
    eu_o[...] = eu_ref[...].astype(BF16)
    ed_o[...] = ed_ref[...].astype(BF16)

    ones = ones_ref[...]
    o = o_ref[...]
    mu = _head_sum(o, ones) * (1.0 / HEAD)
    oc = o - mu
    var = _head_sum(oc * oc, ones) * (1.0 / HEAD)
    y = oc * lax.rsqrt(var + LNX_EPS) * lnxg_ref[...] + lnxb_ref[...]
    y = (y + bv_ref[...]) * g_ref[...]
    mixed = _dot(pool_ref[...], wop_ref[...]) + _dot(y, wor_ref[...])
    h = _layer_norm(ALPHA * x_ref[...] + mixed, ln1g_ref[...], ln1b_ref[...])
    h_o[:, 0:D_MODEL] = h

    h_hi = h.astype(BF16)
    h_lo = (h - h_hi.astype(F32)).astype(BF16)
    logits = (jnp.dot(h_hi, rw_hi_ref[...], preferred_element_type=F32)
              + jnp.dot(h_lo, rw_hi_ref[...], preferred_element_type=F32)
              + jnp.dot(h_hi, rw_lo_ref[...], preferred_element_type=F32)) + rb_ref[...]
    lane = lax.broadcasted_iota(jnp.int32, (ts, LANES), 1)
    neg = -jnp.inf
    gl = jnp.where(lane < N_GROUPS, logits, neg)
    gmax = jnp.max(gl, axis=-1, keepdims=True)
    g_idx = jnp.min(jnp.where(gl == gmax, lane, LANES), axis=-1, keepdims=True)
    g_top_p = 1.0 / jnp.sum(jnp.exp(gl - gmax), axis=-1, keepdims=True)
    lo_lane = EXPERT_LANE0 + g_idx * EXPERTS_PER_GROUP
    el = jnp.where((lane >= lo_lane) & (lane < lo_lane + EXPERTS_PER_GROUP), logits, neg)
    m1 = jnp.max(el, axis=-1, keepdims=True)
    i1 = jnp.min(jnp.where(el == m1, lane, LANES), axis=-1, keepdims=True)
    el2 = jnp.where(lane == i1, neg, el)
    m2 = jnp.max(el2, axis=-1, keepdims=True)
    i2 = jnp.min(jnp.where(el2 == m2, lane, LANES), axis=-1, keepdims=True)
    e21 = jnp.exp(m2 - m1)
    wgt1 = g_top_p / (1.0 + e21)
    wgt2 = g_top_p * e21 / (1.0 + e21)

    l1 = i1 - lo_lane
    l2 = i2 - lo_lane
    lo = jnp.minimum(l1, l2)
    hi = jnp.maximum(l1, l2)
    cat = g_idx * PAIRS_PER_GROUP + jnp.right_shift(lo * (2 * EXPERTS_PER_GROUP - 1 - lo), 1) + (hi - lo - 1)
    in_cat = lane == cat
    onehot = jnp.where(in_cat, 1.0, 0.0)
    before = jnp.dot(tri_ref[...], onehot.astype(BF16), preferred_element_type=F32)
    posn = base_ref[0:1, :] + before
    rank = jnp.sum(jnp.where(in_cat, posn, 0.0), axis=-1, keepdims=True)
    new_base = base_ref[0:1, :] + jnp.sum(onehot, axis=0, keepdims=True)
    base_ref[...] = jnp.broadcast_to(new_base, base_ref.shape)
    cnt_o[...] = jnp.broadcast_to(new_base, cnt_o.shape)

    meta = jnp.where(lane == l1, wgt1, jnp.where(lane == l2, wgt2, 0.0))
    meta = jnp.where(lane == META_CAT, cat.astype(F32), jnp.where(lane == META_RANK, rank, meta))
    h_o[:, D_MODEL:D_EXT] = meta
    meta_t_o[...] = jnp.transpose(meta)[META_CAT:META_CAT + 8, :]


def _post_call(x2, pool2, o2, bv2, g2, lnx_g, lnx_b, ones_bd, wo_pool, wo_rwkv,
               ln1_g, ln1_b, rw_hi, rw_lo, rb, tri, exp_gate, exp_up, exp_down):
    n = x2.shape[0]
    n_steps = n // TS
    assert N_EXPERTS % n_steps == 0
    epb = N_EXPERTS // n_steps
    full = lambda arr: pl.BlockSpec(arr.shape, lambda i: (0,) * arr.ndim)
    tok = lambda width: pl.BlockSpec((TS, width), lambda i: (i, 0))
    wblk = lambda arr: pl.BlockSpec((epb,) + arr.shape[1:], lambda i: (i, 0, 0))
    wout = lambda arr: jax.ShapeDtypeStruct(arr.shape, BF16)
    params = (lnx_g, lnx_b, ones_bd, wo_pool, wo_rwkv, ln1_g, ln1_b, rw_hi, rw_lo, rb, tri)
    experts = (exp_gate, exp_up, exp_down)
    return pl.pallas_call(
        _post_kernel,
        grid=(n_steps,),
        in_specs=([tok(D_MODEL), tok(D_POOL)] + [tok(D_RWKV)] * 3 + [full(a) for a in params]
                  + [wblk(a) for a in experts]),
        out_specs=[tok(D_EXT), pl.BlockSpec((8, TS), lambda i: (0, i)),
                   pl.BlockSpec((8, LANES), lambda i: (0, 0))] + [wblk(a) for a in experts],
        out_shape=[jax.ShapeDtypeStruct((n, D_EXT), F32), jax.ShapeDtypeStruct((8, n), F32),
                   jax.ShapeDtypeStruct((8, LANES), F32)] + [wout(a) for a in experts],
        scratch_shapes=[pltpu.VMEM((8, LANES), F32)],
        compiler_params=pltpu.CompilerParams(
            dimension_semantics=("arbitrary",), vmem_limit_bytes=VMEM_LIMIT),
        name="post",
    )(x2, pool2, o2, bv2, g2, *params, *experts)


def _start_rows(ts, row_copy):
    def issue(g, carry):
        for u in range(DMA_UNROLL):
            row_copy(g * DMA_UNROLL + u).start(priority=u % 2)
        return carry

    lax.fori_loop(0, ts // DMA_UNROLL, issue, 0)


def _dispatch_kernel(pad_start_ref, pad_cnt_ref, n_used_ref, dest_ref, h_ref, xs_ref, zero_ref,
                     sem, zsem):
    i = pl.program_id(0)
    ts = h_ref.shape[0]
    n_blocks = xs_ref.shape[0] // BM

    def row_copy(t):
        return pltpu.make_async_copy(h_ref.at[pl.ds(t, 1)],
                                     xs_ref.at[pl.ds(dest_ref[0, 0, t], 1)], sem)

    def zero_row_copy(row):
        return pltpu.make_async_copy(zero_ref.at[pl.ds(0, 1)], xs_ref.at[pl.ds(row, 1)], zsem)

    def zero_block_copy(blk):
        return pltpu.make_async_copy(zero_ref, xs_ref.at[pl.ds(pl.multiple_of(blk * BM, BM), BM)], zsem)

    @pl.when(i == 0)
    def _():
        zero_ref[...] = jnp.zeros_like(zero_ref)

        def fill(e, carry):
            start = pad_start_ref[e]

            def one(j, c):
                zero_row_copy(start + j).start()
                return c

            return lax.fori_loop(0, pad_cnt_ref[e], one, carry)

        def fill_wait(e, carry):
            def one(j, c):
                zero_row_copy(0).wait()
                return c

            return lax.fori_loop(0, pad_cnt_ref[e], one, carry)

        def tail(blk, carry):
            zero_block_copy(blk).start()
            return carry

        def tail_wait(blk, carry):
            zero_block_copy(blk).wait()
            return carry

        lax.fori_loop(0, N_GROUPS, fill, 0)
        lax.fori_loop(n_used_ref[0], n_blocks, tail, 0)
        lax.fori_loop(0, N_GROUPS, fill_wait, 0)
        lax.fori_loop(n_used_ref[0], n_blocks, tail_wait, 0)

    _start_rows(ts, row_copy)
    pltpu.make_async_copy(h_ref, xs_ref.at[pl.ds(0, ts)], sem).wait()


def _dispatch_call(pad_start, pad_cnt, n_used, dest, h_ext, m_pad):
    n = h_ext.shape[0]
    grid_spec = pltpu.PrefetchScalarGridSpec(
        num_scalar_prefetch=3,
        grid=(n // TS,),
        in_specs=[pl.BlockSpec((1, 1, TS), lambda i, *_: (i, 0, 0), memory_space=pltpu.SMEM),
                  pl.BlockSpec((TS, D_EXT), lambda i, *_: (i, 0))],
        out_specs=pl.BlockSpec(memory_space=pl.ANY),
        scratch_shapes=[pltpu.VMEM((BM, D_EXT), F32), pltpu.SemaphoreType.DMA(()),
                        pltpu.SemaphoreType.DMA(())],
    )
    return pl.pallas_call(
        _dispatch_kernel,
        grid_spec=grid_spec,
        out_shape=jax.ShapeDtypeStruct((m_pad, D_EXT), F32),
        compiler_params=pltpu.CompilerParams(
            dimension_semantics=("arbitrary",), vmem_limit_bytes=VMEM_LIMIT,
            disable_bounds_checks=True),
        name="dispatch",
    )(pad_start, pad_cnt, n_used, dest, h_ext)


def _expert_kernel(blk_g_ref, n_used_ref, act_ref, xs_ref, wg_ref, wu_ref, wd_ref, ys_ref, xb_ref):
    i = pl.program_id(0)

    @pl.when(i < n_used_ref[0])
    def _():
        xb_ref[...] = xs_ref[:, 0:D_MODEL].astype(BF16)
        ys_ref[...] = jnp.zeros_like(ys_ref)
        for e in range(EXPERTS_PER_GROUP):
            @pl.when(act_ref[i * EXPERTS_PER_GROUP + e] != 0)
            def _(e=e):
                xb = xb_ref[...]
                gate = jnp.dot(xb, wg_ref[0, e], preferred_element_type=F32)
                up = jnp.dot(xb, wu_ref[0, e], preferred_element_type=F32)
                w_e = xs_ref[:, D_MODEL + e:D_MODEL + e + 1]
                hid = jnp.where(w_e != 0.0, gate * _sigmoid(gate) * up * w_e, 0.0)
                ys_ref[...] += jnp.dot(hid.astype(BF16), wd_ref[0, e * D_EXPERT:(e + 1) * D_EXPERT],
                                       preferred_element_type=F32)

    @pl.when(i >= n_used_ref[0])
    def _():
        ys_ref[...] = jnp.zeros_like(ys_ref)


def _expert_call(blk_g, n_used, act, xs, wg, wu, wd):
    m_pad = xs.shape[0]
    n_blocks = m_pad // BM
    row_blk = lambda i, bg, nu, ac: (jnp.minimum(i, nu[0] - 1), 0)
    out_blk = lambda i, bg, nu, ac: (i, 0)
    grp = lambda i, bg, nu: bg[jnp.minimum(i, nu[0] - 1)]
    grid_spec = pltpu.PrefetchScalarGridSpec(
        num_scalar_prefetch=3,
        grid=(n_blocks,),
        in_specs=[pl.BlockSpec((BM, D_EXT), row_blk),
                  pl.BlockSpec((1, EXPERTS_PER_GROUP, D_MODEL, D_EXPERT),
                               lambda i, bg, nu, ac: (grp(i, bg, nu), 0, 0, 0)),
                  pl.BlockSpec((1, EXPERTS_PER_GROUP, D_MODEL, D_EXPERT),
                               lambda i, bg, nu, ac: (grp(i, bg, nu), 0, 0, 0)),
                  pl.BlockSpec((1, EXPERTS_PER_GROUP * D_EXPERT, D_MODEL),
                               lambda i, bg, nu, ac: (grp(i, bg, nu), 0, 0))],
        out_specs=pl.BlockSpec((BM, D_MODEL), out_blk),
        scratch_shapes=[pltpu.VMEM((BM, D_MODEL), BF16)],
    )
    return pl.pallas_call(
        _expert_kernel,
        grid_spec=grid_spec,
        out_shape=jax.ShapeDtypeStruct((m_pad, D_MODEL), F32),
        compiler_params=pltpu.CompilerParams(
            dimension_semantics=("arbitrary",), vmem_limit_bytes=VMEM_LIMIT),
        name="experts",
    )(blk_g, n_used, act, xs, wg, wu, wd)


def _combine_kernel(dest_ref, dest_next_ref, h_ref, ys_ref, g_ref, b_ref, out_ref, y_ref, sem):
    i = pl.program_id(0)
    n_steps = pl.num_programs(0)
    ts = h_ref.shape[0]

    def gather(idx_ref, buf):
        _start_rows(ts, lambda t: pltpu.make_async_copy(
            ys_ref.at[pl.ds(idx_ref[0, 0, t], 1)], y_ref.at[buf, pl.ds(t, 1)], sem.at[buf]))

    cur = lax.rem(i, 2)

    @pl.when(i == 0)
    def _():
        gather(dest_ref, 0)

    @pl.when(i + 1 < n_steps)
    def _():
        gather(dest_next_ref, 1 - cur)

    pltpu.make_async_copy(ys_ref.at[pl.ds(0, ts)], y_ref.at[cur], sem.at[cur]).wait()
    out_ref[...] = _layer_norm(ALPHA * h_ref[...] + y_ref[cur], g_ref[...], b_ref[...])


def _combine_call(dest, h_ext, ys, ln2_g, ln2_b):
    n = h_ext.shape[0]
    full = lambda arr: pl.BlockSpec(arr.shape, lambda i: (0,) * arr.ndim)
    return pl.pallas_call(
        _combine_kernel,
        grid=(n // TS,),
        in_specs=[pl.BlockSpec((1, 1, TS), lambda i: (i, 0, 0), memory_space=pltpu.SMEM),
                  pl.BlockSpec((1, 1, TS), lambda i: (jnp.minimum(i + 1, n // TS - 1), 0, 0),
                               memory_space=pltpu.SMEM),
                  pl.BlockSpec((TS, D_MODEL), lambda i: (i, 0)),
                  pl.BlockSpec(memory_space=pl.ANY),
                  full(ln2_g), full(ln2_b)],
        out_specs=pl.BlockSpec((TS, D_MODEL), lambda i: (i, 0)),
        out_shape=jax.ShapeDtypeStruct((n, D_MODEL), F32),
        scratch_shapes=[pltpu.VMEM((2, TS, D_MODEL), F32), pltpu.SemaphoreType.DMA((2,))],
        compiler_params=pltpu.CompilerParams(
            dimension_semantics=("arbitrary",), vmem_limit_bytes=VMEM_LIMIT,
            disable_bounds_checks=True),
        name="combine",
    )(dest, dest, h_ext, ys, ln2_g, ln2_b)


def _block_diag(blocks):
    g, c, d = blocks.shape
    eye = jnp.eye(g, dtype=blocks.dtype)
    return (eye[:, None, :, None] * blocks[:, :, None, :]).reshape(g * c, g * d)


def _layer(x, w_in, pool_w, pool_scale, mu_shift, w0, w_up, a0, a_up, g_up, k_k, k_a, r_k,
           lnx_g, lnx_b, w_out, ln1_g, ln1_b, router_group, router_group_b, router_expert,
           router_expert_b, exp_gate, exp_up, exp_down, ln2_g, ln2_b):
    b, s, d = x.shape
    n = b * s
    row2 = lambda t: t.reshape(1, -1)

    ones_bd = _block_diag(jnp.ones((2, HEAD, HEAD), BF16))
    lora_w = _block_diag(jnp.stack([w_up, a_up])).astype(BF16)
    poolw_bd = _block_diag(pool_w).astype(BF16)
    rw = jnp.zeros((D_MODEL, LANES), F32)
    rw = rw.at[:, 0:N_GROUPS].set(router_group).at[:, EXPERT_LANE0:EXPERT_LANE0 + N_EXPERTS].set(router_expert)
    rw_hi = rw.astype(BF16)
    rw_lo = (rw - rw_hi.astype(F32)).astype(BF16)
    rb = jnp.zeros((1, LANES), F32)
    rb = rb.at[0, 0:N_GROUPS].set(router_group_b).at[0, EXPERT_LANE0:EXPERT_LANE0 + N_EXPERTS].set(router_expert_b)
    tri = (lax.broadcasted_iota(jnp.int32, (TS, TS), 0) > lax.broadcasted_iota(jnp.int32, (TS, TS), 1)).astype(BF16)

    pool_out, r, w, k, v, a_vec, b_vec, g, bv = _prep_call(
        x, w_in.astype(BF16), row2(mu_shift), row2(w0), row2(a0), lora_w, g_up.astype(BF16),
        row2(k_k), row2(k_a), poolw_bd, row2(pool_scale), ones_bd, row2(r_k))
    o = _rwkv_call(r, w, k, v, a_vec, b_vec)

    flat = lambda t: t.reshape(n, t.shape[-1])
    w_out_bf = w_out.astype(BF16)
    h_ext, meta_t, counts, eg_bf, eu_bf, ed_bf = _post_call(
        flat(x), flat(pool_out), flat(o), flat(bv), flat(g),
        row2(lnx_g), row2(lnx_b), ones_bd, w_out_bf[0:D_POOL], w_out_bf[D_POOL:],
        row2(ln1_g), row2(ln1_b), rw_hi, rw_lo, rb, tri, exp_gate, exp_up, exp_down)

    cnt_cat = counts[0, 0:N_CATS].astype(jnp.int32).reshape(N_GROUPS, PAIRS_PER_GROUP)
    cnt = jnp.sum(cnt_cat, axis=1)
    padded = (cnt + BM - 1) // BM * BM
    pend = jnp.cumsum(padded)
    pstart = pend - padded
    cat_start = (pstart[:, None] + jnp.cumsum(cnt_cat, axis=1) - cnt_cat).reshape(N_CATS)
    cnt_cat = cnt_cat.reshape(N_CATS)
    cats = meta_t[0].astype(jnp.int32)
    onehot = cats[:, None] == jnp.arange(N_CATS, dtype=jnp.int32)
    dest = jnp.sum(jnp.where(onehot, cat_start, 0), axis=-1) + meta_t[1].astype(jnp.int32)
    dest = dest.reshape(n // TS, 1, TS)
    m_pad = n + N_GROUPS * BM
    n_blocks = m_pad // BM
    blk_start = jnp.arange(n_blocks, dtype=jnp.int32) * BM
    blk_g = jnp.minimum(jnp.sum(blk_start[:, None] >= pend[None, :], axis=1), N_GROUPS - 1).astype(jnp.int32)
    n_used = (pend[-1:] // BM).astype(jnp.int32)
    overlap = ((cat_start[None, :] < blk_start[:, None] + BM)
               & (cat_start[None, :] + cnt_cat[None, :] > blk_start[:, None])
               & (cnt_cat[None, :] > 0))
    act = jnp.any(overlap[:, :, None] & _PAIR_MEMBER[None], axis=1).astype(jnp.int32).reshape(-1)

    grouped = lambda wts: wts.reshape(N_GROUPS, EXPERTS_PER_GROUP, *wts.shape[1:])
    wd = ed_bf.reshape(N_GROUPS, EXPERTS_PER_GROUP * D_EXPERT, D_MODEL)
    xs = _dispatch_call(pstart + cnt, padded - cnt, n_used, dest, h_ext, m_pad)
    ys = _expert_call(blk_g, n_used, act, xs, grouped(eg_bf), grouped(eu_bf), wd)
    out = _combine_call(dest, h_ext, ys, row2(ln2_g), row2(ln2_b))
    return out.reshape(b, s, d)


def kernel(x, w_in, pool_w, pool_scale, mu_shift, w0, w_up, a0, a_up, g_up, k_k, k_a, r_k, lnx_g, lnx_b, w_out, ln1_g, ln1_b, router_group, router_group_b, router_expert, router_expert_b, exp_gate, exp_up, exp_down, ln2_g, ln2_b):
    depth = w_in.shape[0]
    for l in range(depth):
        x = _layer(x, w_in[l], pool_w[l], pool_scale[l], mu_shift[l], w0[l], w_up[l], a0[l],
                   a_up[l], g_up[l], k_k[l], k_a[l], r_k[l], lnx_g[l], lnx_b[l], w_out[l],
                   ln1_g[l], ln1_b[l], router_group[l], router_group_b[l], router_expert[l],
                   router_expert_b[l], exp_gate[l], exp_up[l], exp_down[l], ln2_g[l], ln2_b[l])
    return x
```

```python
import numpy as np
import jax
import jax.numpy as jnp
from jax import lax
from jax.experimental import pallas as pl
from jax.experimental.pallas import tpu as pltpu

F32 = jnp.float32
BF16 = jnp.bfloat16

D_MODEL = 1024
D_POOL = 256
POOL_WINDOWS = (2, 4, 8, 16)
POOL_GROUP = 64
POOL_HALO = 16
D_RWKV = 768
HEAD = 64
D_DECAY_LORA = 64
D_AAA_LORA = 64
D_GATE_LORA = 128
D_RWKV_IN = 3 * D_RWKV + D_DECAY_LORA + D_AAA_LORA + D_GATE_LORA
D_IN = D_POOL + D_RWKV_IN
N_GROUPS = 4
EXPERTS_PER_GROUP = 8
N_EXPERTS = 32
D_EXPERT = 256
LN_EPS = 1e-5
LNX_EPS = 64e-5
ALPHA = 2.0 ** 0.25
DECAY_SCALE = 0.6065306597126334
KK_NORM_FLOOR = 1e-24

LANES = 128
VMEM_LIMIT = 56 * 1024 * 1024

PAIR = 2 * HEAD
N_PAIRS = D_RWKV // PAIR
CHUNK = 64
RWKV_TILE = 8 * CHUNK
TS = 512
TM = 1024
BM = 256
DMA_UNROLL = 8
EXPERT_LANE0 = 32
D_EXT = D_MODEL + LANES
META_CAT = 8
META_RANK = 9
PAIRS_PER_GROUP = EXPERTS_PER_GROUP * (EXPERTS_PER_GROUP - 1) // 2
N_CATS = N_GROUPS * PAIRS_PER_GROUP


def _pair_member():
    member = np.zeros((N_CATS, EXPERTS_PER_GROUP), bool)
    for g in range(N_GROUPS):
        pair = 0
        for lo in range(EXPERTS_PER_GROUP):
            for hi in range(lo + 1, EXPERTS_PER_GROUP):
                member[g * PAIRS_PER_GROUP + pair, [lo, hi]] = True
                pair += 1
    return member


_PAIR_MEMBER = _pair_member()


def _dot(a, b):
    return jnp.dot(a.astype(BF16), b.astype(BF16), preferred_element_type=F32)


def _dot_nt(a, b):
    return lax.dot_general(a.astype(BF16), b.astype(BF16), (((1,), (1,)), ((), ())),
                           preferred_element_type=F32)


def _dot_tn(a, b):
    return lax.dot_general(a.astype(BF16), b.astype(BF16), (((0,), (0,)), ((), ())),
                           preferred_element_type=F32)


def _head_sum(x, ones_bd):
    parts = [_dot(x[:, p * PAIR:(p + 1) * PAIR], ones_bd) for p in range(N_PAIRS)]
    return jnp.concatenate(parts, axis=1)


def _sigmoid(x):
    return 1.0 / (1.0 + jnp.exp(-x))


def _prep_kernel(x_ref, win_ref, mu_ref, w0_ref, a0_ref, lora_ref, gup_ref, kk_ref, ka_ref,
                 poolw_ref, pools_ref, ones_ref, rk_ref,
                 pool_o, r_o, w_o, k_o, v_o, a_o, b_o, g_o, bv_o,
                 proj_ref, halo_ref):
    i = pl.program_id(1)
    ts = x_ref.shape[1]

    @pl.when(i == 0)
    def _():
        halo_ref[...] = jnp.zeros_like(halo_ref)

    proj_ref[...] = jnp.dot(x_ref[0].astype(BF16), win_ref[...], preferred_element_type=F32)

    row = lax.broadcasted_iota(jnp.int32, (ts, 1), 0)

    def shifted(off, width):
        z = proj_ref[:, off:off + width]
        prev = jnp.where(row == 0, halo_ref[POOL_HALO - 1:POOL_HALO, off:off + width],
                         pltpu.roll(z, 1, 0))
        return z + (prev - z) * mu_ref[:, off - D_POOL:off - D_POOL + width]

    p = proj_ref[:, 0:D_POOL]
    ext = jnp.concatenate([halo_ref[:, 0:D_POOL], p], axis=0)
    s2 = ext + pltpu.roll(ext, 1, 0)
    s4 = s2 + pltpu.roll(s2, 2, 0)
    s8 = s4 + pltpu.roll(s4, 4, 0)
    s16 = s8 + pltpu.roll(s8, 8, 0)
    lane = lax.broadcasted_iota(jnp.int32, (ts, D_POOL), 1)
    grp = lane // POOL_GROUP
    wsum = jnp.where(grp == 0, s2[POOL_HALO:], jnp.where(grp == 1, s4[POOL_HALO:],
                     jnp.where(grp == 2, s8[POOL_HALO:], s16[POOL_HALO:])))
    win = jnp.where(grp == 0, 2.0, jnp.where(grp == 1, 4.0, jnp.where(grp == 2, 8.0, 16.0)))
    pos = (i * ts + row + 1).astype(F32)
    diff = wsum / jnp.minimum(pos, win) - p
    pool_o[0] = _dot(diff, poolw_ref[...]) * pools_ref[...]

    o = D_POOL
    r = shifted(o, D_RWKV)
    k = shifted(o + D_RWKV, D_RWKV)
    v = shifted(o + 2 * D_RWKV, D_RWKV)
    lw = shifted(o + 3 * D_RWKV, D_DECAY_LORA + D_AAA_LORA)
    gd = shifted(o + 3 * D_RWKV + D_DECAY_LORA + D_AAA_LORA, D_GATE_LORA)

    lane128 = lax.broadcasted_iota(jnp.int32, (ts, LANES), 1)
    lora_in = jnp.where(lane128 < D_DECAY_LORA, jnp.tanh(lw), lw)
    lora = _dot(lora_in, lora_ref[...])
    w_o[0] = -DECAY_SCALE * _sigmoid(w0_ref[...] + lora[:, 0:D_RWKV])
    eta = _sigmoid(a0_ref[...] + lora[:, D_RWKV:2 * D_RWKV])
    g_o[0] = _dot(_sigmoid(gd), gup_ref[...]).astype(g_o.dtype)
    kk = k * kk_ref[...]
    ss = _head_sum(kk * kk, ones_ref[...])
    kkn = kk * lax.rsqrt(jnp.maximum(ss, KK_NORM_FLOOR))
    k_mod = k * (1.0 + (eta - 1.0) * ka_ref[...])
    r_o[0] = r
    v_o[0] = v.astype(v_o.dtype)
    k_o[0] = k_mod
    bv_o[0] = (_head_sum(r * k_mod * rk_ref[...], ones_ref[...]) * v).astype(bv_o.dtype)
    a_o[0] = -kkn
    b_o[0] = kkn * eta

    halo_ref[...] = proj_ref[ts - POOL_HALO:ts, :]


def _prep_call(x, win_bf, mu, w0, a0, lora_w, gup_bf, k_k, k_a, poolw_bd, pool_scale, ones_bd, rk):
    b, s, _ = x.shape
    grid = (b, s // TS)
    full = lambda arr: pl.BlockSpec(arr.shape, lambda bi, i: (0,) * arr.ndim)
    tok = lambda width: pl.BlockSpec((1, TS, width), lambda bi, i: (bi, i, 0))
    dtypes = (F32, F32, F32, BF16, F32, F32, F32, F32)
    outs = ([jax.ShapeDtypeStruct((b, s, D_POOL), F32)]
            + [jax.ShapeDtypeStruct((b, s, D_RWKV), dt) for dt in dtypes])
    params = (win_bf, mu, w0, a0, lora_w, gup_bf, k_k, k_a, poolw_bd, pool_scale, ones_bd, rk)
    return pl.pallas_call(
        _prep_kernel,
        grid=grid,
        in_specs=[tok(D_MODEL)] + [full(a) for a in params],
        out_specs=[tok(D_POOL)] + [tok(D_RWKV)] * 8,
        out_shape=outs,
        scratch_shapes=[pltpu.VMEM((TS, D_IN), F32), pltpu.VMEM((POOL_HALO, D_IN), F32)],
        compiler_params=pltpu.CompilerParams(
            dimension_semantics=("parallel", "arbitrary"), vmem_limit_bytes=VMEM_LIMIT),
        name="prep",
    )(x, *params)


def _rwkv_kernel(r_ref, w_ref, k_ref, v_ref, a_ref, b_ref, o_ref, h_ref):
    c = pl.program_id(1)

    @pl.when(c == 0)
    def _():
        h_ref[...] = jnp.zeros_like(h_ref)

    L = CHUNK
    n_chunks = r_ref.shape[1] // L
    row = lax.broadcasted_iota(jnp.int32, (L, PAIR), 0)
    lane = lax.broadcasted_iota(jnp.int32, (L, PAIR), 1)
    head0 = lane < HEAD
    strict = row > (lane & (L - 1))
    incl = row >= (lane & (L - 1))
    eye_wide = jnp.where(row == (lane & (L - 1)), 1.0, 0.0)

    def expand(x):
        xb = x.astype(BF16)
        zero = jnp.zeros_like(xb)
        return jnp.concatenate([jnp.where(head0, xb, zero), jnp.where(head0, zero, xb)], axis=0)

    def tile(ref, j, p):
        return ref[0, j * L:(j + 1) * L, p * PAIR:(p + 1) * PAIR]

    streams = [(j, p) for j in range(n_chunks) for p in range(N_PAIRS)]
    st = []
    for j, p in streams:
        w = tile(w_ref, j, p)
        cum = w
        for sh in (1, 2, 4, 8, 16, 32):
            cum = cum + jnp.where(row >= sh, pltpu.roll(cum, sh, 0), 0.0)
        tot = cum[L - 1:L, :]
        e_neg = jnp.exp(-cum)
        e_rem = jnp.exp(tot - cum)
        b = tile(b_ref, j, p)
        k = tile(k_ref, j, p)
        a_n = tile(a_ref, j, p) * jnp.exp(cum - w)
        st.append(dict(
            ar=jnp.concatenate([a_n, tile(r_ref, j, p) * jnp.exp(cum)], axis=0).astype(BF16),
            v_e=expand(tile(v_ref, j, p)),
            bk_t=jnp.concatenate([expand(b * e_neg), expand(k * e_neg)], axis=0),
            bk_h=jnp.concatenate([expand(b * e_rem), expand(k * e_rem)], axis=0),
            w_tot=jnp.exp(tot)))
    for d in st:
        sc = _dot_nt(d['ar'], d['bk_t'])
        d['t'] = jnp.where(strict, sc[0:L, 0:PAIR], 0.0)
        d['t_ak'] = jnp.where(strict, sc[0:L, PAIR:2 * PAIR], 0.0)
        d['r_all'] = jnp.concatenate([d['ar'][L:2 * L],
                                      jnp.where(incl, sc[L:2 * L, 0:PAIR], 0.0).astype(BF16),
                                      jnp.where(incl, sc[L:2 * L, PAIR:2 * PAIR], 0.0).astype(BF16)],
                                     axis=1)
    for d in st:
        d['takv'] = _dot(d['t_ak'], d['v_e'])
        d['m'] = eye_wide + d['t']
        d['t'] = _dot(d['t'], expand(d['t']))
    for step in range(5):
        for d in st:
            if step < 4:
                both = _dot(d['t'], jnp.concatenate([expand(d['t']), expand(d['m'])], axis=1))
                d['t'] = both[:, 0:PAIR]
                d['m'] = d['m'] + both[:, PAIR:2 * PAIR]
            else:
                d['m'] = d['m'] + _dot(d['t'], expand(d['m']))
    for d in st:
        d['pq'] = _dot(d['m'], jnp.concatenate([expand(d['ar'][0:L]), expand(d['takv'])], axis=1))

    h = [h_ref[p] for p in range(N_PAIRS)]
    for j in range(n_chunks):
        ds = st[j * N_PAIRS:(j + 1) * N_PAIRS]
        hb = [hp.astype(BF16) for hp in h]
        u_e = [expand(_dot(d['pq'][:, 0:PAIR], hb[p]) + d['pq'][:, PAIR:2 * PAIR])
               for p, d in enumerate(ds)]
        for p, d in enumerate(ds):
            o_ref[0, j * L:(j + 1) * L, p * PAIR:(p + 1) * PAIR] = _dot(
                d['r_all'], jnp.concatenate([hb[p], u_e[p], d['v_e']], axis=0))
        for p, d in enumerate(ds):
            h_add = _dot_tn(d['bk_h'], jnp.concatenate([u_e[p], d['v_e']], axis=0))
            w_col = jnp.transpose(jnp.broadcast_to(d['w_tot'], (PAIR, PAIR)))
            h[p] = h[p] * w_col + h_add
    for p in range(N_PAIRS):
        h_ref[p] = h[p]


def _rwkv_call(r, w, k, v, a, b):
    bsz, s, _ = r.shape
    spec = pl.BlockSpec((1, RWKV_TILE, D_RWKV), lambda bi, c: (bi, c, 0))
    return pl.pallas_call(
        _rwkv_kernel,
        grid=(bsz, s // RWKV_TILE),
        in_specs=[spec] * 6,
        out_specs=spec,
        out_shape=jax.ShapeDtypeStruct((bsz, s, D_RWKV), F32),
        scratch_shapes=[pltpu.VMEM((N_PAIRS, PAIR, PAIR), F32)],
        compiler_params=pltpu.CompilerParams(
            dimension_semantics=("parallel", "arbitrary"), vmem_limit_bytes=VMEM_LIMIT),
        name="rwkv",
    )(r, w, k, v, a, b)


def _layer_norm(x, g, b):
    mu = jnp.mean(x, axis=-1, keepdims=True)
    xc = x - mu
    var = jnp.mean(xc * xc, axis=-1, keepdims=True)
    return xc * lax.rsqrt(var + LN_EPS) * g + b


def _post_kernel(x_ref, pool_ref, o_ref, bv_ref, g_ref,
                 lnxg_ref, lnxb_ref, ones_ref, wop_ref, wor_ref, ln1g_ref, ln1b_ref,
                 rw_hi_ref, rw_lo_ref, rb_ref, tri_ref, eg_ref, eu_ref, ed_ref,
                 h_o, meta_t_o, cnt_o, eg_o, eu_o, ed_o, base_ref):
    i = pl.program_id(0)
    ts = x_ref.shape[0]

    @pl.when(i == 0)
    def _():
        base_ref[...] = jnp.zeros_like(base_ref)

    eg_o[...] = eg_ref[...].astype(BF16)
    eu_o[...] = eu_ref[...].astype(BF16)
    ed_o[...] = ed_ref[...].astype(BF16)

    ones = ones_ref[...]
    o = o_ref[...]
    mu = _head_sum(o, ones) * (1.0 / HEAD)
    oc = o - mu
    var = _head_sum(oc * oc, ones) * (1.0 / HEAD)
    y = oc * lax.rsqrt(var + LNX_EPS) * lnxg_ref[...] + lnxb_ref[...]
    y = (y + bv_ref[...]) * g_ref[...]
    mixed = _dot(pool_ref[...], wop_ref[...]) + _dot(y, wor_ref[...])
    h = _layer_norm(ALPHA * x_ref[...] + mixed, ln1g_ref[...], ln1b_ref[...])
    h_o[:, 0:D_MODEL] = h

    h_hi = h.astype(BF16)
    h_lo = (h - h_hi.astype(F32)).astype(BF16)
    logits = (jnp.dot(h_hi, rw_hi_ref[...], preferred_element_type=F32)
              + jnp.dot(h_lo, rw_hi_ref[...], preferred_element_type=F32)
              + jnp.dot(h_hi, rw_lo_ref[...], preferred_element_type=F32)) + rb_ref[...]
    lane = lax.broadcasted_iota(jnp.int32, (ts, LANES), 1)
    neg = -jnp.inf
    gl = jnp.where(lane < N_GROUPS, logits, neg)
    gmax = jnp.max(gl, axis=-1, keepdims=True)
    g_idx = jnp.min(jnp.where(gl == gmax, lane, LANES), axis=-1, keepdims=True)
    g_top_p = 1.0 / jnp.sum(jnp.exp(gl - gmax), axis=-1, keepdims=True)
    lo_lane = EXPERT_LANE0 + g_idx * EXPERTS_PER_GROUP
    el = jnp.where((lane >= lo_lane) & (lane < lo_lane + EXPERTS_PER_GROUP), logits, neg)
    m1 = jnp.max(el, axis=-1, keepdims=True)
    i1 = jnp.min(jnp.where(el == m1, lane, LANES), axis=-1, keepdims=True)
    el2 = jnp.where(lane == i1, neg, el)
    m2 = jnp.max(el2, axis=-1, keepdims=True)
    i2 = jnp.min(jnp.where(el2 == m2, lane, LANES), axis=-1, keepdims=True)
    e21 = jnp.exp(m2 - m1)
    wgt1 = g_top_p / (1.0 + e21)
    wgt2 = g_top_p * e21 / (1.0 + e21)

    l1 = i1 - lo_lane
    l2 = i2 - lo_lane
    lo = jnp.minimum(l1, l2)
    hi = jnp.maximum(l1, l2)
    cat = g_idx * PAIRS_PER_GROUP + jnp.right_shift(lo * (2 * EXPERTS_PER_GROUP - 1 - lo), 1) + (hi - lo - 1)
    in_cat = lane == cat
    onehot = jnp.where(in_cat, 1.0, 0.0)
    before = jnp.dot(tri_ref[...], onehot.astype(BF16), preferred_element_type=F32)
    posn = base_ref[0:1, :] + before
    rank = jnp.sum(jnp.where(in_cat, posn, 0.0), axis=-1, keepdims=True)
    new_base = base_ref[0:1, :] + jnp.sum(onehot, axis=0, keepdims=True)
    base_ref[...] = jnp.broadcast_to(new_base, base_ref.shape)
    cnt_o[...] = jnp.broadcast_to(new_base, cnt_o.shape)

    meta = jnp.where(lane == l1, wgt1, jnp.where(lane == l2, wgt2, 0.0))
    meta = jnp.where(lane == META_CAT, cat.astype(F32), jnp.where(lane == META_RANK, rank, meta))
    h_o[:, D_MODEL:D_EXT] = meta
    meta_t_o[...] = jnp.transpose(meta)[META_CAT:META_CAT + 8, :]


def _post_call(x2, pool2, o2, bv2, g2, lnx_g, lnx_b, ones_bd, wo_pool, wo_rwkv,
               ln1_g, ln1_b, rw_hi, rw_lo, rb, tri, exp_gate, exp_up, exp_down):
    n = x2.shape[0]
    n_steps = n // TS
    assert N_EXPERTS % n_steps == 0
    epb = N_EXPERTS // n_steps
    full = lambda arr: pl.BlockSpec(arr.shape, lambda i: (0,) * arr.ndim)
    tok = lambda width: pl.BlockSpec((TS, width), lambda i: (i, 0))
    wblk = lambda arr: pl.BlockSpec((epb,) + arr.shape[1:], lambda i: (i, 0, 0))
    wout = lambda arr: jax.ShapeDtypeStruct(arr.shape, BF16)
    params = (lnx_g, lnx_b, ones_bd, wo_pool, wo_rwkv, ln1_g, ln1_b, rw_hi, rw_lo, rb, tri)
    experts = (exp_gate, exp_up, exp_down)
    return pl.pallas_call(
        _post_kernel,
        grid=(n_steps,),
        in_specs=([tok(D_MODEL), tok(D_POOL)] + [tok(D_RWKV)] * 3 + [full(a) for a in params]
                  + [wblk(a) for a in experts]),
        out_specs=[tok(D_EXT), pl.BlockSpec((8, TS), lambda i: (0, i)),
                   pl.BlockSpec((8, LANES), lambda i: (0, 0))] + [wblk(a) for a in experts],
        out_shape=[jax.ShapeDtypeStruct((n, D_EXT), F32), jax.ShapeDtypeStruct((8, n), F32),
                   jax.ShapeDtypeStruct((8, LANES), F32)] + [wout(a) for a in experts],
        scratch_shapes=[pltpu.VMEM((8, LANES), F32)],
        compiler_params=pltpu.CompilerParams(
            dimension_semantics=("arbitrary",), vmem_limit_bytes=VMEM_LIMIT),
        name="post",
    )(x2, pool2, o2, bv2, g2, *params, *experts)


def _start_rows(ts, row_copy):
    def issue(g, carry):
        for u in range(DMA_UNROLL):
            row_copy(g * DMA_UNROLL + u).start(priority=u % 2)
        return carry

    lax.fori_loop(0, ts // DMA_UNROLL, issue, 0)


def _dispatch_kernel(pad_start_ref, pad_cnt_ref, n_used_ref, dest_ref, h_ref, xs_ref, zero_ref,
                     stage_ref, sem, zsem):
    i = pl.program_id(0)
    ts = h_ref.shape[0]
    n_blocks = xs_ref.shape[0] // BM
    slot = lax.rem(i, 2)

    def row_copy(t):
        return pltpu.make_async_copy(stage_ref.at[slot, pl.ds(t, 1)],
                                     xs_ref.at[pl.ds(dest_ref[0, 0, t], 1)], sem.at[slot])

    def wait_tile(s):
        pltpu.make_async_copy(stage_ref.at[s], xs_ref.at[pl.ds(0, ts)], sem.at[s]).wait()

    def zero_row_copy(row):
        return pltpu.make_async_copy(zero_ref.at[pl.ds(0, 1)], xs_ref.at[pl.ds(row, 1)], zsem)

    def zero_block_copy(blk):
        return pltpu.make_async_copy(zero_ref, xs_ref.at[pl.ds(pl.multiple_of(blk * BM, BM), BM)], zsem)

    @pl.when(i == 0)
    def _():
        zero_ref[...] = jnp.zeros_like(zero_ref)

        def fill(e, carry):
            start = pad_start_ref[e]

            def one(j, c):
                zero_row_copy(start + j).start()
                return c

            return lax.fori_loop(0, pad_cnt_ref[e], one, carry)

        def fill_wait(e, carry):
            def one(j, c):
                zero_row_copy(0).wait()
                return c

            return lax.fori_loop(0, pad_cnt_ref[e], one, carry)

        def tail(blk, carry):
            zero_block_copy(blk).start()
            return carry

        def tail_wait(blk, carry):
            zero_block_copy(blk).wait()
            return carry

        lax.fori_loop(0, N_GROUPS, fill, 0)
        lax.fori_loop(n_used_ref[0], n_blocks, tail, 0)
        lax.fori_loop(0, N_GROUPS, fill_wait, 0)
        lax.fori_loop(n_used_ref[0], n_blocks, tail_wait, 0)

    stage_ref[slot] = h_ref[...]
    _start_rows(ts, row_copy)

    @pl.when(i > 0)
    def _():
        wait_tile(1 - slot)

    @pl.when(i == pl.num_programs(0) - 1)
    def _():
        wait_tile(slot)


def _dispatch_call(pad_start, pad_cnt, n_used, dest, h_ext, m_pad):
    n = h_ext.shape[0]
    grid_spec = pltpu.PrefetchScalarGridSpec(
        num_scalar_prefetch=3,
        grid=(n // TM,),
        in_specs=[pl.BlockSpec((1, 1, TM), lambda i, *_: (i, 0, 0), memory_space=pltpu.SMEM),
                  pl.BlockSpec((TM, D_EXT), lambda i, *_: (i, 0))],
        out_specs=pl.BlockSpec(memory_space=pl.ANY),
        scratch_shapes=[pltpu.VMEM((BM, D_EXT), F32), pltpu.VMEM((2, TM, D_EXT), F32),
                        pltpu.SemaphoreType.DMA((2,)), pltpu.SemaphoreType.DMA(())],
    )
    return pl.pallas_call(
        _dispatch_kernel,
        grid_spec=grid_spec,
        out_shape=jax.ShapeDtypeStruct((m_pad, D_EXT), F32),
        compiler_params=pltpu.CompilerParams(
            dimension_semantics=("arbitrary",), vmem_limit_bytes=VMEM_LIMIT,
            disable_bounds_checks=True),
        name="dispatch",
    )(pad_start, pad_cnt, n_used, dest, h_ext)


def _expert_kernel(blk_g_ref, n_used_ref, act_ref, xs_ref, wg_ref, wu_ref, wd_ref, ys_ref, xb_ref):
    i = pl.program_id(0)

    @pl.when(i < n_used_ref[0])
    def _():
        xb_ref[...] = xs_ref[:, 0:D_MODEL].astype(BF16)

        def expert_out(e):
            xb = xb_ref[...]
            gate = jnp.dot(xb, wg_ref[0, e], preferred_element_type=F32)
            up = jnp.dot(xb, wu_ref[0, e], preferred_element_type=F32)
            w_e = xs_ref[:, D_MODEL + e:D_MODEL + e + 1]
            hid = jnp.where(w_e != 0.0, gate * _sigmoid(gate) * up * w_e, 0.0)
            return jnp.dot(hid.astype(BF16), wd_ref[0, e * D_EXPERT:(e + 1) * D_EXPERT],
                           preferred_element_type=F32)

        for e in range(EXPERTS_PER_GROUP):
            flag = act_ref[i * EXPERTS_PER_GROUP + e]

            @pl.when(flag == 2)
            def _(e=e):
                ys_ref[...] = expert_out(e)

            @pl.when(flag == 1)
            def _(e=e):
                ys_ref[...] += expert_out(e)

    @pl.when(i >= n_used_ref[0])
    def _():
        ys_ref[...] = jnp.zeros_like(ys_ref)


def _expert_call(blk_g, n_used, act, xs, wg, wu, wd):
    m_pad = xs.shape[0]
    n_blocks = m_pad // BM
    row_blk = lambda i, bg, nu, ac: (jnp.minimum(i, nu[0] - 1), 0)
    out_blk = lambda i, bg, nu, ac: (i, 0)
    grp = lambda i, bg, nu: bg[jnp.minimum(i, nu[0] - 1)]
    grid_spec = pltpu.PrefetchScalarGridSpec(
        num_scalar_prefetch=3,
        grid=(n_blocks,),
        in_specs=[pl.BlockSpec((BM, D_EXT), row_blk),
                  pl.BlockSpec((1, EXPERTS_PER_GROUP, D_MODEL, D_EXPERT),
                               lambda i, bg, nu, ac: (grp(i, bg, nu), 0, 0, 0)),
                  pl.BlockSpec((1, EXPERTS_PER_GROUP, D_MODEL, D_EXPERT),
                               lambda i, bg, nu, ac: (grp(i, bg, nu), 0, 0, 0)),
                  pl.BlockSpec((1, EXPERTS_PER_GROUP * D_EXPERT, D_MODEL),
                               lambda i, bg, nu, ac: (grp(i, bg, nu), 0, 0))],
        out_specs=pl.BlockSpec((BM, D_MODEL), out_blk),
        scratch_shapes=[pltpu.VMEM((BM, D_MODEL), BF16)],
    )
    return pl.pallas_call(
        _expert_kernel,
        grid_spec=grid_spec,
        out_shape=jax.ShapeDtypeStruct((m_pad, D_MODEL), F32),
        compiler_params=pltpu.CompilerParams(
            dimension_semantics=("arbitrary",), vmem_limit_bytes=VMEM_LIMIT),
        name="experts",
    )(blk_g, n_used, act, xs, wg, wu, wd)


def _combine_kernel(dest_ref, dest_next_ref, h_ref, ys_ref, g_ref, b_ref, out_ref, y_ref, sem):
    i = pl.program_id(0)
    n_steps = pl.num_programs(0)
    ts = h_ref.shape[0]

    def gather(idx_ref, buf):
        _start_rows(ts, lambda t: pltpu.make_async_copy(
            ys_ref.at[pl.ds(idx_ref[0, 0, t], 1)], y_ref.at[buf, pl.ds(t, 1)], sem.at[buf]))

    cur = lax.rem(i, 2)

    @pl.when(i == 0)
    def _():
        gather(dest_ref, 0)

    @pl.when(i + 1 < n_steps)
    def _():
        gather(dest_next_ref, 1 - cur)

    pltpu.make_async_copy(ys_ref.at[pl.ds(0, ts)], y_ref.at[cur], sem.at[cur]).wait()
    out_ref[...] = _layer_norm(ALPHA * h_ref[...] + y_ref[cur], g_ref[...], b_ref[...])


def _combine_call(dest, h_ext, ys, ln2_g, ln2_b):
    n = h_ext.shape[0]
    full = lambda arr: pl.BlockSpec(arr.shape, lambda i: (0,) * arr.ndim)
    return pl.pallas_call(
        _combine_kernel,
        grid=(n // TM,),
        in_specs=[pl.BlockSpec((1, 1, TM), lambda i: (i, 0, 0), memory_space=pltpu.SMEM),
                  pl.BlockSpec((1, 1, TM), lambda i: (jnp.minimum(i + 1, n // TM - 1), 0, 0),
                               memory_space=pltpu.SMEM),
                  pl.BlockSpec((TM, D_MODEL), lambda i: (i, 0)),
                  pl.BlockSpec(memory_space=pl.ANY),
                  full(ln2_g), full(ln2_b)],
        out_specs=pl.BlockSpec((TM, D_MODEL), lambda i: (i, 0)),
        out_shape=jax.ShapeDtypeStruct((n, D_MODEL), F32),
        scratch_shapes=[pltpu.VMEM((2, TM, D_MODEL), F32), pltpu.SemaphoreType.DMA((2,))],
        compiler_params=pltpu.CompilerParams(
            dimension_semantics=("arbitrary",), vmem_limit_bytes=VMEM_LIMIT,
            disable_bounds_checks=True),
        name="combine",
    )(dest, dest, h_ext, ys, ln2_g, ln2_b)


def _block_diag(blocks):
    g, c, d = blocks.shape
    eye = jnp.eye(g, dtype=blocks.dtype)
    return (eye[:, None, :, None] * blocks[:, :, None, :]).reshape(g * c, g * d)


def _layer(x, w_in, pool_w, pool_scale, mu_shift, w0, w_up, a0, a_up, g_up, k_k, k_a, r_k,
           lnx_g, lnx_b, w_out, ln1_g, ln1_b, router_group, router_group_b, router_expert,
           router_expert_b, exp_gate, exp_up, exp_down, ln2_g, ln2_b):
    b, s, d = x.shape
    n = b * s
    row2 = lambda t: t.reshape(1, -1)

    ones_bd = _block_diag(jnp.ones((2, HEAD, HEAD), BF16))
    lora_w = _block_diag(jnp.stack([w_up, a_up])).astype(BF16)
    poolw_bd = _block_diag(pool_w).astype(BF16)
    rw = jnp.zeros((D_MODEL, LANES), F32)
    rw = rw.at[:, 0:N_GROUPS].set(router_group).at[:, EXPERT_LANE0:EXPERT_LANE0 + N_EXPERTS].set(router_expert)
    rw_hi = rw.astype(BF16)
    rw_lo = (rw - rw_hi.astype(F32)).astype(BF16)
    rb = jnp.zeros((1, LANES), F32)
    rb = rb.at[0, 0:N_GROUPS].set(router_group_b).at[0, EXPERT_LANE0:EXPERT_LANE0 + N_EXPERTS].set(router_expert_b)
    tri = (lax.broadcasted_iota(jnp.int32, (TS, TS), 0) > lax.broadcasted_iota(jnp.int32, (TS, TS), 1)).astype(BF16)

    pool_out, r, w, k, v, a_vec, b_vec, g, bv = _prep_call(
        x, w_in.astype(BF16), row2(mu_shift), row2(w0), row2(a0), lora_w, g_up.astype(BF16),
        row2(k_k), row2(k_a), poolw_bd, row2(pool_scale), ones_bd, row2(r_k))
    o = _rwkv_call(r, w, k, v, a_vec, b_vec)

    flat = lambda t: t.reshape(n, t.shape[-1])
    w_out_bf = w_out.astype(BF16)
    h_ext, meta_t, counts, eg_bf, eu_bf, ed_bf = _post_call(
        flat(x), flat(pool_out), flat(o), flat(bv), flat(g),
        row2(lnx_g), row2(lnx_b), ones_bd, w_out_bf[0:D_POOL], w_out_bf[D_POOL:],
        row2(ln1_g), row2(ln1_b), rw_hi, rw_lo, rb, tri, exp_gate, exp_up, exp_down)

    cnt_cat = counts[0, 0:N_CATS].astype(jnp.int32).reshape(N_GROUPS, PAIRS_PER_GROUP)
    cnt = jnp.sum(cnt_cat, axis=1)
    padded = (cnt + BM - 1) // BM * BM
    pend = jnp.cumsum(padded)
    pstart = pend - padded
    cat_start = (pstart[:, None] + jnp.cumsum(cnt_cat, axis=1) - cnt_cat).reshape(N_CATS)
    cnt_cat = cnt_cat.reshape(N_CATS)
    cats = meta_t[0].astype(jnp.int32)
    onehot = cats[:, None] == jnp.arange(N_CATS, dtype=jnp.int32)
    dest = jnp.sum(jnp.where(onehot, cat_start, 0), axis=-1) + meta_t[1].astype(jnp.int32)
    dest = dest.reshape(n // TM, 1, TM)
    m_pad = n + N_GROUPS * BM
    n_blocks = m_pad // BM
    blk_start = jnp.arange(n_blocks, dtype=jnp.int32) * BM
    blk_g = jnp.minimum(jnp.sum(blk_start[:, None] >= pend[None, :], axis=1), N_GROUPS - 1).astype(jnp.int32)
    n_used = (pend[-1:] // BM).astype(jnp.int32)
    overlap = ((cat_start[None, :] < blk_start[:, None] + BM)
               & (cat_start[None, :] + cnt_cat[None, :] > blk_start[:, None])
               & (cnt_cat[None, :] > 0))
    act = jnp.any(overlap[:, :, None] & _PAIR_MEMBER[None], axis=1).astype(jnp.int32)
    first = jnp.cumsum(act, axis=1) == 1
    act = (act + (first & (act == 1))).reshape(-1)

    grouped = lambda wts: wts.reshape(N_GROUPS, EXPERTS_PER_GROUP, *wts.shape[1:])
    wd = ed_bf.reshape(N_GROUPS, EXPERTS_PER_GROUP * D_EXPERT, D_MODEL)
    xs = _dispatch_call(pstart + cnt, padded - cnt, n_used, dest, h_ext, m_pad)
    ys = _expert_call(blk_g, n_used, act, xs, grouped(eg_bf), grouped(eu_bf), wd)
    out = _combine_call(dest, h_ext, ys, row2(ln2_g), row2(ln2_b))
    return out.reshape(b, s, d)


def kernel(x, w_in, pool_w, pool_scale, mu_shift, w0, w_up, a0, a_up, g_up, k_k, k_a, r_k, lnx_g, lnx_b, w_out, ln1_g, ln1_b, router_group, router_group_b, router_expert, router_expert_b, exp_gate, exp_up, exp_down, ln2_g, ln2_b):
    depth = w_in.shape[0]
    for l in range(depth):
        x = _layer(x, w_in[l], pool_w[l], pool_scale[l], mu_shift[l], w0[l], w_up[l], a0[l],
                   a_up[l], g_up[l], k_k[l], k_a[l], r_k[l], lnx_g[l], lnx_b[l], w_out[l],
                   ln1_g[l], ln1_b[l], router_group[l], router_group_b[l], router_expert[l],
                   router_expert_b[l], exp_gate[l], exp_up[l], exp_down[l], ln2_g[l], ln2_b[l])
    return x
```

```python
import numpy as np
import jax
import jax.numpy as jnp
from jax import lax
from jax.experimental import pallas as pl
from jax.experimental.pallas import tpu as pltpu

F32 = jnp.float32
BF16 = jnp.bfloat16

D_MODEL = 1024
D_POOL = 256
POOL_WINDOWS = (2, 4, 8, 16)
POOL_GROUP = 64
POOL_HALO = 16
D_RWKV = 768
HEAD = 64
D_DECAY_LORA = 64
D_AAA_LORA = 64
D_GATE_LORA = 128
D_RWKV_IN = 3 * D_RWKV + D_DECAY_LORA + D_AAA_LORA + D_GATE_LORA
D_IN = D_POOL + D_RWKV_IN
N_GROUPS = 4
EXPERTS_PER_GROUP = 8
N_EXPERTS = 32
D_EXPERT = 256
LN_EPS = 1e-5
LNX_EPS = 64e-5
ALPHA = 2.0 ** 0.25
DECAY_SCALE = 0.6065306597126334
KK_NORM_FLOOR = 1e-24

LANES = 128
VMEM_LIMIT = 56 * 1024 * 1024

PAIR = 2 * HEAD
N_PAIRS = D_RWKV // PAIR
CHUNK = 64
RWKV_TILE = 8 * CHUNK
TS = 512
TM = 1024
BM = 256
DMA_UNROLL = 8
EXPERT_LANE0 = 32
D_EXT = D_MODEL + LANES
META_CAT = 8
META_RANK = 9
PAIRS_PER_GROUP = EXPERTS_PER_GROUP * (EXPERTS_PER_GROUP - 1) // 2
N_CATS = N_GROUPS * PAIRS_PER_GROUP


def _pair_member():
    member = np.zeros((N_CATS, EXPERTS_PER_GROUP), bool)
    for g in range(N_GROUPS):
        pair = 0
        for lo in range(EXPERTS_PER_GROUP):
            for hi in range(lo + 1, EXPERTS_PER_GROUP):
                member[g * PAIRS_PER_GROUP + pair, [lo, hi]] = True
                pair += 1
    return member


_PAIR_MEMBER = _pair_member()


def _dot(a, b):
    return jnp.dot(a.astype(BF16), b.astype(BF16), preferred_element_type=F32)


def _dot_nt(a, b):
    return lax.dot_general(a.astype(BF16), b.astype(BF16), (((1,), (1,)), ((), ())),
                           preferred_element_type=F32)


def _dot_tn(a, b):
    return lax.dot_general(a.astype(BF16), b.astype(BF16), (((0,), (0,)), ((), ())),
                           preferred_element_type=F32)


def _head_sum(x, ones_bd):
    parts = [_dot(x[:, p * PAIR:(p + 1) * PAIR], ones_bd) for p in range(N_PAIRS)]
    return jnp.concatenate(parts, axis=1)


def _sigmoid(x):
    return 1.0 / (1.0 + jnp.exp(-x))


def _prep_kernel(x_ref, win_ref, mu_ref, w0_ref, a0_ref, lora_ref, gup_ref, kk_ref, ka_ref,
                 poolw_ref, pools_ref, ones_ref, rk_ref,
                 pool_o, r_o, w_o, k_o, v_o, a_o, b_o, g_o, bv_o,
                 proj_ref, halo_ref):
    i = pl.program_id(1)
    ts = x_ref.shape[1]

    @pl.when(i == 0)
    def _():
        halo_ref[...] = jnp.zeros_like(halo_ref)

    proj_ref[...] = jnp.dot(x_ref[0].astype(BF16), win_ref[...], preferred_element_type=F32)

    row = lax.broadcasted_iota(jnp.int32, (ts, 1), 0)

    def shifted(off, width):
        z = proj_ref[:, off:off + width]
        prev = jnp.where(row == 0, halo_ref[POOL_HALO - 1:POOL_HALO, off:off + width],
                         pltpu.roll(z, 1, 0))
        return z + (prev - z) * mu_ref[:, off - D_POOL:off - D_POOL + width]

    p = proj_ref[:, 0:D_POOL]
    ext = jnp.concatenate([halo_ref[:, 0:D_POOL], p], axis=0)
    s2 = ext + pltpu.roll(ext, 1, 0)
    s4 = s2 + pltpu.roll(s2, 2, 0)
    s8 = s4 + pltpu.roll(s4, 4, 0)
    s16 = s8 + pltpu.roll(s8, 8, 0)
    lane = lax.broadcasted_iota(jnp.int32, (ts, D_POOL), 1)
    grp = lane // POOL_GROUP
    wsum = jnp.where(grp == 0, s2[POOL_HALO:], jnp.where(grp == 1, s4[POOL_HALO:],
                     jnp.where(grp == 2, s8[POOL_HALO:], s16[POOL_HALO:])))
    win = jnp.where(grp == 0, 2.0, jnp.where(grp == 1, 4.0, jnp.where(grp == 2, 8.0, 16.0)))
    pos = (i * ts + row + 1).astype(F32)
    diff = wsum / jnp.minimum(pos, win) - p
    pool_o[0] = _dot(diff, poolw_ref[...]) * pools_ref[...]

    o = D_POOL
    r = shifted(o, D_RWKV)
    k = shifted(o + D_RWKV, D_RWKV)
    v = shifted(o + 2 * D_RWKV, D_RWKV)
    lw = shifted(o + 3 * D_RWKV, D_DECAY_LORA + D_AAA_LORA)
    gd = shifted(o + 3 * D_RWKV + D_DECAY_LORA + D_AAA_LORA, D_GATE_LORA)

    lane128 = lax.broadcasted_iota(jnp.int32, (ts, LANES), 1)
    lora_in = jnp.where(lane128 < D_DECAY_LORA, jnp.tanh(lw), lw)
    lora = _dot(lora_in, lora_ref[...])
    w_o[0] = -DECAY_SCALE * _sigmoid(w0_ref[...] + lora[:, 0:D_RWKV])
    eta = _sigmoid(a0_ref[...] + lora[:, D_RWKV:2 * D_RWKV])
    g_o[0] = _dot(_sigmoid(gd), gup_ref[...]).astype(g_o.dtype)
    kk = k * kk_ref[...]
    ss = _head_sum(kk * kk, ones_ref[...])
    kkn = kk * lax.rsqrt(jnp.maximum(ss, KK_NORM_FLOOR))
    k_mod = k * (1.0 + (eta - 1.0) * ka_ref[...])
    r_o[0] = r
    v_o[0] = v.astype(v_o.dtype)
    k_o[0] = k_mod
    bv_o[0] = (_head_sum(r * k_mod * rk_ref[...], ones_ref[...]) * v).astype(bv_o.dtype)
    a_o[0] = -kkn
    b_o[0] = kkn * eta

    halo_ref[...] = proj_ref[ts - POOL_HALO:ts, :]


def _prep_call(x, win_bf, mu, w0, a0, lora_w, gup_bf, k_k, k_a, poolw_bd, pool_scale, ones_bd, rk):
    b, s, _ = x.shape
    grid = (b, s // TS)
    full = lambda arr: pl.BlockSpec(arr.shape, lambda bi, i: (0,) * arr.ndim)
    tok = lambda width: pl.BlockSpec((1, TS, width), lambda bi, i: (bi, i, 0))
    dtypes = (F32, F32, F32, BF16, F32, F32, F32, F32)
    outs = ([jax.ShapeDtypeStruct((b, s, D_POOL), F32)]
            + [jax.ShapeDtypeStruct((b, s, D_RWKV), dt) for dt in dtypes])
    params = (win_bf, mu, w0, a0, lora_w, gup_bf, k_k, k_a, poolw_bd, pool_scale, ones_bd, rk)
    return pl.pallas_call(
        _prep_kernel,
        grid=grid,
        in_specs=[tok(D_MODEL)] + [full(a) for a in params],
        out_specs=[tok(D_POOL)] + [tok(D_RWKV)] * 8,
        out_shape=outs,
        scratch_shapes=[pltpu.VMEM((TS, D_IN), F32), pltpu.VMEM((POOL_HALO, D_IN), F32)],
        compiler_params=pltpu.CompilerParams(
            dimension_semantics=("parallel", "arbitrary"), vmem_limit_bytes=VMEM_LIMIT),
        name="prep",
    )(x, *params)


def _rwkv_kernel(r_ref, w_ref, k_ref, v_ref, a_ref, b_ref, o_ref, h_ref):
    c = pl.program_id(1)

    @pl.when(c == 0)
    def _():
        h_ref[...] = jnp.zeros_like(h_ref)

    L = CHUNK
    n_chunks = r_ref.shape[1] // L
    row = lax.broadcasted_iota(jnp.int32, (L, PAIR), 0)
    lane = lax.broadcasted_iota(jnp.int32, (L, PAIR), 1)
    head0 = lane < HEAD
    strict = row > (lane & (L - 1))
    incl = row >= (lane & (L - 1))
    eye_wide = jnp.where(row == (lane & (L - 1)), 1.0, 0.0)

    def expand(x):
        xb = x.astype(BF16)
        zero = jnp.zeros_like(xb)
        return jnp.concatenate([jnp.where(head0, xb, zero), jnp.where(head0, zero, xb)], axis=0)

    def tile(ref, j, p):
        return ref[0, j * L:(j + 1) * L, p * PAIR:(p + 1) * PAIR]

    streams = [(j, p) for j in range(n_chunks) for p in range(N_PAIRS)]
    st = []
    for j, p in streams:
        w = tile(w_ref, j, p)
        cum = w
        for sh in (1, 2, 4, 8, 16, 32):
            cum = cum + jnp.where(row >= sh, pltpu.roll(cum, sh, 0), 0.0)
        tot = cum[L - 1:L, :]
        e_neg = jnp.exp(-cum)
        e_rem = jnp.exp(tot - cum)
        b = tile(b_ref, j, p)
        k = tile(k_ref, j, p)
        a_n = tile(a_ref, j, p) * jnp.exp(cum - w)
        st.append(dict(
            ar=jnp.concatenate([a_n, tile(r_ref, j, p) * jnp.exp(cum)], axis=0).astype(BF16),
            v_e=expand(tile(v_ref, j, p)),
            bk_t=jnp.concatenate([expand(b * e_neg), expand(k * e_neg)], axis=0),
            bk_h=jnp.concatenate([expand(b * e_rem), expand(k * e_rem)], axis=0),
            w_tot=jnp.exp(tot)))
    for d in st:
        sc = _dot_nt(d['ar'], d['bk_t'])
        d['t'] = jnp.where(strict, sc[0:L, 0:PAIR], 0.0)
        d['t_ak'] = jnp.where(strict, sc[0:L, PAIR:2 * PAIR], 0.0)
        d['r_all'] = jnp.concatenate([d['ar'][L:2 * L],
                                      jnp.where(incl, sc[L:2 * L, 0:PAIR], 0.0).astype(BF16),
                                      jnp.where(incl, sc[L:2 * L, PAIR:2 * PAIR], 0.0).astype(BF16)],
                                     axis=1)
    for d in st:
        d['takv'] = _dot(d['t_ak'], d['v_e'])
        d['m'] = eye_wide + d['t']
        d['t'] = _dot(d['t'], expand(d['t']))
    for step in range(5):
        for d in st:
            if step < 4:
                both = _dot(d['t'], jnp.concatenate([expand(d['t']), expand(d['m'])], axis=1))
                d['t'] = both[:, 0:PAIR]
                d['m'] = d['m'] + both[:, PAIR:2 * PAIR]
            else:
                d['m'] = d['m'] + _dot(d['t'], expand(d['m']))
    for d in st:
        d['pq'] = _dot(d['m'], jnp.concatenate([expand(d['ar'][0:L]), expand(d['takv'])], axis=1))

    h = [h_ref[p] for p in range(N_PAIRS)]
    for j in range(n_chunks):
        ds = st[j * N_PAIRS:(j + 1) * N_PAIRS]
        hb = [hp.astype(BF16) for hp in h]
        u_e = [expand(_dot(d['pq'][:, 0:PAIR], hb[p]) + d['pq'][:, PAIR:2 * PAIR])
               for p, d in enumerate(ds)]
        for p, d in enumerate(ds):
            o_ref[0, j * L:(j + 1) * L, p * PAIR:(p + 1) * PAIR] = _dot(
                d['r_all'], jnp.concatenate([hb[p], u_e[p], d['v_e']], axis=0))
        for p, d in enumerate(ds):
            h_add = _dot_tn(d['bk_h'], jnp.concatenate([u_e[p], d['v_e']], axis=0))
            w_col = jnp.transpose(jnp.broadcast_to(d['w_tot'], (PAIR, PAIR)))
            h[p] = h[p] * w_col + h_add
    for p in range(N_PAIRS):
        h_ref[p] = h[p]


def _rwkv_call(r, w, k, v, a, b):
    bsz, s, _ = r.shape
    spec = pl.BlockSpec((1, RWKV_TILE, D_RWKV), lambda bi, c: (bi, c, 0))
    return pl.pallas_call(
        _rwkv_kernel,
        grid=(bsz, s // RWKV_TILE),
        in_specs=[spec] * 6,
        out_specs=spec,
        out_shape=jax.ShapeDtypeStruct((bsz, s, D_RWKV), F32),
        scratch_shapes=[pltpu.VMEM((N_PAIRS, PAIR, PAIR), F32)],
        compiler_params=pltpu.CompilerParams(
            dimension_semantics=("parallel", "arbitrary"), vmem_limit_bytes=VMEM_LIMIT),
        name="rwkv",
    )(r, w, k, v, a, b)


def _layer_norm(x, g, b):
    mu = jnp.mean(x, axis=-1, keepdims=True)
    xc = x - mu
    var = jnp.mean(xc * xc, axis=-1, keepdims=True)
    return xc * lax.rsqrt(var + LN_EPS) * g + b


def _post_kernel(x_ref, pool_ref, o_ref, bv_ref, g_ref,
                 lnxg_ref, lnxb_ref, ones_ref, wop_ref, wor_ref, ln1g_ref, ln1b_ref,
                 rw_hi_ref, rw_lo_ref, rb_ref, tri_ref, eg_ref, eu_ref, ed_ref,
                 h_o, meta_t_o, cnt_o, eg_o, eu_o, ed_o, base_ref):
    i = pl.program_id(0)
    ts = x_ref.shape[0]

    @pl.when(i == 0)
    def _():
        base_ref[...] = jnp.zeros_like(base_ref)

    eg_o[...] = eg_ref[...].astype(BF16)
    eu_o[...] = eu_ref[...].astype(BF16)
    ed_o[...] = ed_ref[...].astype(BF16)

    ones = ones_ref[...]
    o = o_ref[...]
    mu = _head_sum(o, ones) * (1.0 / HEAD)
    oc = o - mu
    var = _head_sum(oc * oc, ones) * (1.0 / HEAD)
    y = oc * lax.rsqrt(var + LNX_EPS) * lnxg_ref[...] + lnxb_ref[...]
    y = (y + bv_ref[...]) * g_ref[...]
    mixed = _dot(pool_ref[...], wop_ref[...]) + _dot(y, wor_ref[...])
    h = _layer_norm(ALPHA * x_ref[...] + mixed, ln1g_ref[...], ln1b_ref[...])
    h_o[:, 0:D_MODEL] = h

    h_hi = h.astype(BF16)
    h_lo = (h - h_hi.astype(F32)).astype(BF16)
    logits = (jnp.dot(h_hi, rw_hi_ref[...], preferred_element_type=F32)
              + jnp.dot(h_lo, rw_hi_ref[...], preferred_element_type=F32)
              + jnp.dot(h_hi, rw_lo_ref[...], preferred_element_type=F32)) + rb_ref[...]
    lane = lax.broadcasted_iota(jnp.int32, (ts, LANES), 1)
    neg = -jnp.inf
    gl = jnp.where(lane < N_GROUPS, logits, neg)
    gmax = jnp.max(gl, axis=-1, keepdims=True)
    g_idx = jnp.min(jnp.where(gl == gmax, lane, LANES), axis=-1, keepdims=True)
    g_top_p = 1.0 / jnp.sum(jnp.exp(gl - gmax), axis=-1, keepdims=True)
    lo_lane = EXPERT_LANE0 + g_idx * EXPERTS_PER_GROUP
    el = jnp.where((lane >= lo_lane) & (lane < lo_lane + EXPERTS_PER_GROUP), logits, neg)
    m1 = jnp.max(el, axis=-1, keepdims=True)
    i1 = jnp.min(jnp.where(el == m1, lane, LANES), axis=-1, keepdims=True)
    el2 = jnp.where(lane == i1, neg, el)
    m2 = jnp.max(el2, axis=-1, keepdims=True)
    i2 = jnp.min(jnp.where(el2 == m2, lane, LANES), axis=-1, keepdims=True)
    e21 = jnp.exp(m2 - m1)
    wgt1 = g_top_p / (1.0 + e21)
    wgt2 = g_top_p * e21 / (1.0 + e21)

    l1 = i1 - lo_lane
    l2 = i2 - lo_lane
    lo = jnp.minimum(l1, l2)
    hi = jnp.maximum(l1, l2)
    cat = g_idx * PAIRS_PER_GROUP + jnp.right_shift(lo * (2 * EXPERTS_PER_GROUP - 1 - lo), 1) + (hi - lo - 1)
    in_cat = lane == cat
    onehot = jnp.where(in_cat, 1.0, 0.0)
    before = jnp.dot(tri_ref[...], onehot.astype(BF16), preferred_element_type=F32)
    posn = base_ref[0:1, :] + before
    rank = jnp.sum(jnp.where(in_cat, posn, 0.0), axis=-1, keepdims=True)
    new_base = base_ref[0:1, :] + jnp.sum(onehot, axis=0, keepdims=True)
    base_ref[...] = jnp.broadcast_to(new_base, base_ref.shape)
    cnt_o[...] = jnp.broadcast_to(new_base, cnt_o.shape)

    meta = jnp.where(lane == l1, wgt1, jnp.where(lane == l2, wgt2, 0.0))
    meta = jnp.where(lane == META_CAT, cat.astype(F32), jnp.where(lane == META_RANK, rank, meta))
    h_o[:, D_MODEL:D_EXT] = meta
    meta_t_o[...] = jnp.transpose(meta)[META_CAT:META_CAT + 8, :]


def _post_call(x2, pool2, o2, bv2, g2, lnx_g, lnx_b, ones_bd, wo_pool, wo_rwkv,
               ln1_g, ln1_b, rw_hi, rw_lo, rb, tri, exp_gate, exp_up, exp_down):
    n = x2.shape[0]
    n_steps = n // TS
    assert N_EXPERTS % n_steps == 0
    epb = N_EXPERTS // n_steps
    full = lambda arr: pl.BlockSpec(arr.shape, lambda i: (0,) * arr.ndim)
    tok = lambda width: pl.BlockSpec((TS, width), lambda i: (i, 0))
    wblk = lambda arr: pl.BlockSpec((epb,) + arr.shape[1:], lambda i: (i, 0, 0))
    wout = lambda arr: jax.ShapeDtypeStruct(arr.shape, BF16)
    params = (lnx_g, lnx_b, ones_bd, wo_pool, wo_rwkv, ln1_g, ln1_b, rw_hi, rw_lo, rb, tri)
    experts = (exp_gate, exp_up, exp_down)
    return pl.pallas_call(
        _post_kernel,
        grid=(n_steps,),
        in_specs=([tok(D_MODEL), tok(D_POOL)] + [tok(D_RWKV)] * 3 + [full(a) for a in params]
                  + [wblk(a) for a in experts]),
        out_specs=[tok(D_EXT), pl.BlockSpec((8, TS), lambda i: (0, i)),
                   pl.BlockSpec((8, LANES), lambda i: (0, 0))] + [wblk(a) for a in experts],
        out_shape=[jax.ShapeDtypeStruct((n, D_EXT), F32), jax.ShapeDtypeStruct((8, n), F32),
                   jax.ShapeDtypeStruct((8, LANES), F32)] + [wout(a) for a in experts],
        scratch_shapes=[pltpu.VMEM((8, LANES), F32)],
        compiler_params=pltpu.CompilerParams(
            dimension_semantics=("arbitrary",), vmem_limit_bytes=VMEM_LIMIT),
        name="post",
    )(x2, pool2, o2, bv2, g2, *params, *experts)


SLAB = 8


def _start_rows(ts, row_copy):
    def issue(g, carry):
        for u in range(SLAB):
            row_copy(g, u).start(priority=u % 2)
        return carry

    lax.fori_loop(0, ts // SLAB, issue, 0)


def _dispatch_kernel(pad_start_ref, pad_cnt_ref, n_used_ref, dest_ref, h_ref, xs_ref, zero_ref,
                     stage_ref, sem, zsem):
    i = pl.program_id(0)
    ts = h_ref.shape[0]
    n_blocks = xs_ref.shape[0] * SLAB // BM
    slot = lax.rem(i, 2)

    def wait_tile(s):
        pltpu.make_async_copy(stage_ref.at[s], xs_ref.at[pl.ds(0, ts // SLAB)], sem.at[s]).wait()

    def zero_row_copy(row):
        return pltpu.make_async_copy(zero_ref.at[0, pl.ds(0, 1)],
                                     xs_ref.at[row // SLAB, pl.ds(lax.rem(row, SLAB), 1)], zsem)

    def zero_block_copy(blk):
        return pltpu.make_async_copy(zero_ref, xs_ref.at[pl.ds(blk * (BM // SLAB), BM // SLAB)], zsem)

    @pl.when(i == 0)
    def _():
        zero_ref[...] = jnp.zeros_like(zero_ref)

        def fill(e, carry):
            start = pad_start_ref[e]

            def one(j, c):
                zero_row_copy(start + j).start()
                return c

            return lax.fori_loop(0, pad_cnt_ref[e], one, carry)

        def fill_wait(e, carry):
            def one(j, c):
                zero_row_copy(0).wait()
                return c

            return lax.fori_loop(0, pad_cnt_ref[e], one, carry)

        def tail(blk, carry):
            zero_block_copy(blk).start()
            return carry

        def tail_wait(blk, carry):
            zero_block_copy(blk).wait()
            return carry

        lax.fori_loop(0, N_GROUPS, fill, 0)
        lax.fori_loop(n_used_ref[0], n_blocks, tail, 0)
        lax.fori_loop(0, N_GROUPS, fill_wait, 0)
        lax.fori_loop(n_used_ref[0], n_blocks, tail_wait, 0)

    for s in range(2):
        @pl.when(slot == s)
        def _(s=s):
            stage_ref[s] = h_ref[...].reshape(ts // SLAB, SLAB, D_EXT)
            _start_rows(ts, lambda g, u: pltpu.make_async_copy(
                stage_ref.at[s, g, pl.ds(u, 1)],
                xs_ref.at[dest_ref[0, 0, g * SLAB + u], pl.ds(dest_ref[0, 0, ts + g * SLAB + u], 1)],
                sem.at[s]))

    @pl.when(i > 0)
    def _():
        wait_tile(1 - slot)

    @pl.when(i == pl.num_programs(0) - 1)
    def _():
        wait_tile(slot)


def _dispatch_call(pad_start, pad_cnt, n_used, dest, h_ext, m_pad):
    n = h_ext.shape[0]
    grid_spec = pltpu.PrefetchScalarGridSpec(
        num_scalar_prefetch=3,
        grid=(n // TM,),
        in_specs=[pl.BlockSpec((1, 1, 2 * TM), lambda i, *_: (i, 0, 0), memory_space=pltpu.SMEM),
                  pl.BlockSpec((TM, D_EXT), lambda i, *_: (i, 0))],
        out_specs=pl.BlockSpec(memory_space=pl.ANY),
        scratch_shapes=[pltpu.VMEM((BM // SLAB, SLAB, D_EXT), F32),
                        pltpu.VMEM((2, TM // SLAB, SLAB, D_EXT), F32),
                        pltpu.SemaphoreType.DMA((2,)), pltpu.SemaphoreType.DMA(())],
    )
    return pl.pallas_call(
        _dispatch_kernel,
        grid_spec=grid_spec,
        out_shape=jax.ShapeDtypeStruct((m_pad // SLAB, SLAB, D_EXT), F32),
        compiler_params=pltpu.CompilerParams(
            dimension_semantics=("arbitrary",), vmem_limit_bytes=VMEM_LIMIT,
            disable_bounds_checks=True),
        name="dispatch",
    )(pad_start, pad_cnt, n_used, dest, h_ext)


def _expert_kernel(blk_g_ref, n_used_ref, act_ref, xs_ref, wg_ref, wu_ref, wd_ref, ys_ref, xb_ref):
    i = pl.program_id(0)

    @pl.when(i < n_used_ref[0])
    def _():
        xb_ref[...] = xs_ref[:, 0:D_MODEL].astype(BF16)

        def expert_out(e):
            xb = xb_ref[...]
            gate = jnp.dot(xb, wg_ref[0, e], preferred_element_type=F32)
            up = jnp.dot(xb, wu_ref[0, e], preferred_element_type=F32)
            w_e = xs_ref[:, D_MODEL + e:D_MODEL + e + 1]
            hid = jnp.where(w_e != 0.0, gate * _sigmoid(gate) * up * w_e, 0.0)
            return jnp.dot(hid.astype(BF16), wd_ref[0, e * D_EXPERT:(e + 1) * D_EXPERT],
                           preferred_element_type=F32)

        for e in range(EXPERTS_PER_GROUP):
            flag = act_ref[i * EXPERTS_PER_GROUP + e]

            @pl.when(flag == 2)
            def _(e=e):
                ys_ref[...] = expert_out(e)

            @pl.when(flag == 1)
            def _(e=e):
                ys_ref[...] += expert_out(e)

    @pl.when(i >= n_used_ref[0])
    def _():
        ys_ref[...] = jnp.zeros_like(ys_ref)


def _expert_call(blk_g, n_used, act, xs, wg, wu, wd):
    m_pad = xs.shape[0]
    n_blocks = m_pad // BM
    row_blk = lambda i, bg, nu, ac: (jnp.minimum(i, nu[0] - 1), 0)
    out_blk = lambda i, bg, nu, ac: (i, 0)
    grp = lambda i, bg, nu: bg[jnp.minimum(i, nu[0] - 1)]
    grid_spec = pltpu.PrefetchScalarGridSpec(
        num_scalar_prefetch=3,
        grid=(n_blocks,),
        in_specs=[pl.BlockSpec((BM, D_EXT), row_blk),
                  pl.BlockSpec((1, EXPERTS_PER_GROUP, D_MODEL, D_EXPERT),
                               lambda i, bg, nu, ac: (grp(i, bg, nu), 0, 0, 0)),
                  pl.BlockSpec((1, EXPERTS_PER_GROUP, D_MODEL, D_EXPERT),
                               lambda i, bg, nu, ac: (grp(i, bg, nu), 0, 0, 0)),
                  pl.BlockSpec((1, EXPERTS_PER_GROUP * D_EXPERT, D_MODEL),
                               lambda i, bg, nu, ac: (grp(i, bg, nu), 0, 0))],
        out_specs=pl.BlockSpec((BM, D_MODEL), out_blk),
        scratch_shapes=[pltpu.VMEM((BM, D_MODEL), BF16)],
    )
    return pl.pallas_call(
        _expert_kernel,
        grid_spec=grid_spec,
        out_shape=jax.ShapeDtypeStruct((m_pad, D_MODEL), F32),
        compiler_params=pltpu.CompilerParams(
            dimension_semantics=("arbitrary",), vmem_limit_bytes=VMEM_LIMIT),
        name="experts",
    )(blk_g, n_used, act, xs, wg, wu, wd)


def _combine_kernel(dest_ref, dest_next_ref, h_ref, ys_ref, g_ref, b_ref, out_ref, y_ref, sem):
    i = pl.program_id(0)
    n_steps = pl.num_programs(0)
    ts = h_ref.shape[0]

    def gather(idx_ref, buf):
        _start_rows(ts, lambda g, u: pltpu.make_async_copy(
            ys_ref.at[idx_ref[0, 0, g * SLAB + u], pl.ds(idx_ref[0, 0, ts + g * SLAB + u], 1)],
            y_ref.at[buf, g, pl.ds(u, 1)], sem.at[buf]))

    cur = lax.rem(i, 2)

    @pl.when(i == 0)
    def _():
        gather(dest_ref, 0)

    for buf in range(2):
        @pl.when((i + 1 < n_steps) & (cur != buf))
        def _(buf=buf):
            gather(dest_next_ref, buf)

    pltpu.make_async_copy(ys_ref.at[pl.ds(0, ts // SLAB)], y_ref.at[cur], sem.at[cur]).wait()
    y = y_ref[cur].reshape(ts, D_MODEL)
    out_ref[...] = _layer_norm(ALPHA * h_ref[...] + y, g_ref[...], b_ref[...])


def _combine_call(dest, h_ext, ys, ln2_g, ln2_b):
    n = h_ext.shape[0]
    full = lambda arr: pl.BlockSpec(arr.shape, lambda i: (0,) * arr.ndim)
    return pl.pallas_call(
        _combine_kernel,
        grid=(n // TM,),
        in_specs=[pl.BlockSpec((1, 1, 2 * TM), lambda i: (i, 0, 0), memory_space=pltpu.SMEM),
                  pl.BlockSpec((1, 1, 2 * TM), lambda i: (jnp.minimum(i + 1, n // TM - 1), 0, 0),
                               memory_space=pltpu.SMEM),
                  pl.BlockSpec((TM, D_MODEL), lambda i: (i, 0)),
                  pl.BlockSpec(memory_space=pl.ANY),
                  full(ln2_g), full(ln2_b)],
        out_specs=pl.BlockSpec((TM, D_MODEL), lambda i: (i, 0)),
        out_shape=jax.ShapeDtypeStruct((n, D_MODEL), F32),
        scratch_shapes=[pltpu.VMEM((2, TM // SLAB, SLAB, D_MODEL), F32),
                        pltpu.SemaphoreType.DMA((2,))],
        compiler_params=pltpu.CompilerParams(
            dimension_semantics=("arbitrary",), vmem_limit_bytes=VMEM_LIMIT,
            disable_bounds_checks=True),
        name="combine",
    )(dest, dest, h_ext, ys, ln2_g, ln2_b)


def _block_diag(blocks):
    g, c, d = blocks.shape
    eye = jnp.eye(g, dtype=blocks.dtype)
    return (eye[:, None, :, None] * blocks[:, :, None, :]).reshape(g * c, g * d)


def _layer(x, w_in, pool_w, pool_scale, mu_shift, w0, w_up, a0, a_up, g_up, k_k, k_a, r_k,
           lnx_g, lnx_b, w_out, ln1_g, ln1_b, router_group, router_group_b, router_expert,
           router_expert_b, exp_gate, exp_up, exp_down, ln2_g, ln2_b):
    b, s, d = x.shape
    n = b * s
    row2 = lambda t: t.reshape(1, -1)

    ones_bd = _block_diag(jnp.ones((2, HEAD, HEAD), BF16))
    lora_w = _block_diag(jnp.stack([w_up, a_up])).astype(BF16)
    poolw_bd = _block_diag(pool_w).astype(BF16)
    rw = jnp.zeros((D_MODEL, LANES), F32)
    rw = rw.at[:, 0:N_GROUPS].set(router_group).at[:, EXPERT_LANE0:EXPERT_LANE0 + N_EXPERTS].set(router_expert)
    rw_hi = rw.astype(BF16)
    rw_lo = (rw - rw_hi.astype(F32)).astype(BF16)
    rb = jnp.zeros((1, LANES), F32)
    rb = rb.at[0, 0:N_GROUPS].set(router_group_b).at[0, EXPERT_LANE0:EXPERT_LANE0 + N_EXPERTS].set(router_expert_b)
    tri = (lax.broadcasted_iota(jnp.int32, (TS, TS), 0) > lax.broadcasted_iota(jnp.int32, (TS, TS), 1)).astype(BF16)

    pool_out, r, w, k, v, a_vec, b_vec, g, bv = _prep_call(
        x, w_in.astype(BF16), row2(mu_shift), row2(w0), row2(a0), lora_w, g_up.astype(BF16),
        row2(k_k), row2(k_a), poolw_bd, row2(pool_scale), ones_bd, row2(r_k))
    o = _rwkv_call(r, w, k, v, a_vec, b_vec)

    flat = lambda t: t.reshape(n, t.shape[-1])
    w_out_bf = w_out.astype(BF16)
    h_ext, meta_t, counts, eg_bf, eu_bf, ed_bf = _post_call(
        flat(x), flat(pool_out), flat(o), flat(bv), flat(g),
        row2(lnx_g), row2(lnx_b), ones_bd, w_out_bf[0:D_POOL], w_out_bf[D_POOL:],
        row2(ln1_g), row2(ln1_b), rw_hi, rw_lo, rb, tri, exp_gate, exp_up, exp_down)

    cnt_cat = counts[0, 0:N_CATS].astype(jnp.int32).reshape(N_GROUPS, PAIRS_PER_GROUP)
    cnt = jnp.sum(cnt_cat, axis=1)
    padded = (cnt + BM - 1) // BM * BM
    pend = jnp.cumsum(padded)
    pstart = pend - padded
    cat_start = (pstart[:, None] + jnp.cumsum(cnt_cat, axis=1) - cnt_cat).reshape(N_CATS)
    cnt_cat = cnt_cat.reshape(N_CATS)
    cats = meta_t[0].astype(jnp.int32)
    onehot = cats[:, None] == jnp.arange(N_CATS, dtype=jnp.int32)
    dest = jnp.sum(jnp.where(onehot, cat_start, 0), axis=-1) + meta_t[1].astype(jnp.int32)
    dest = jnp.concatenate([(dest // SLAB).reshape(n // TM, 1, TM),
                            (dest % SLAB).reshape(n // TM, 1, TM)], axis=2)
    m_pad = n + N_GROUPS * BM
    n_blocks = m_pad // BM
    blk_start = jnp.arange(n_blocks, dtype=jnp.int32) * BM
    blk_g = jnp.minimum(jnp.sum(blk_start[:, None] >= pend[None, :], axis=1), N_GROUPS - 1).astype(jnp.int32)
    n_used = (pend[-1:] // BM).astype(jnp.int32)
    overlap = ((cat_start[None, :] < blk_start[:, None] + BM)
               & (cat_start[None, :] + cnt_cat[None, :] > blk_start[:, None])
               & (cnt_cat[None, :] > 0))
    act = jnp.any(overlap[:, :, None] & _PAIR_MEMBER[None], axis=1).astype(jnp.int32)
    first = jnp.cumsum(act, axis=1) == 1
    act = (act + (first & (act == 1))).reshape(-1)

    grouped = lambda wts: wts.reshape(N_GROUPS, EXPERTS_PER_GROUP, *wts.shape[1:])
    wd = ed_bf.reshape(N_GROUPS, EXPERTS_PER_GROUP * D_EXPERT, D_MODEL)
    xs = _dispatch_call(pstart + cnt, padded - cnt, n_used, dest, h_ext, m_pad)
    xs = xs.reshape(m_pad, D_EXT)
    ys = _expert_call(blk_g, n_used, act, xs, grouped(eg_bf), grouped(eu_bf), wd)
    out = _combine_call(dest, h_ext, ys.reshape(m_pad // SLAB, SLAB, D_MODEL),
                        row2(ln2_g), row2(ln2_b))
    return out.reshape(b, s, d)


def kernel(x, w_in, pool_w, pool_scale, mu_shift, w0, w_up, a0, a_up, g_up, k_k, k_a, r_k, lnx_g, lnx_b, w_out, ln1_g, ln1_b, router_group, router_group_b, router_expert, router_expert_b, exp_gate, exp_up, exp_down, ln2_g, ln2_b):
    depth = w_in.shape[0]
    for l in range(depth):
        x = _layer(x, w_in[l], pool_w[l], pool_scale[l], mu_shift[l], w0[l], w_up[l], a0[l],
                   a_up[l], g_up[l], k_k[l], k_a[l], r_k[l], lnx_g[l], lnx_b[l], w_out[l],
                   ln1_g[l], ln1_b[l], router_group[l], router_group_b[l], router_expert[l],
                   router_expert_b[l], exp_gate[l], exp_up[l], exp_down[l], ln2_g[l], ln2_b[l])
    return x
```

```python
import numpy as np
import jax
import jax.numpy as jnp
from jax import lax
from jax.experimental import pallas as pl
from jax.experimental.pallas import tpu as pltpu

F32 = jnp.float32
BF16 = jnp.bfloat16

D_MODEL = 1024
D_POOL = 256
POOL_WINDOWS = (2, 4, 8, 16)
POOL_GROUP = 64
POOL_HALO = 16
D_RWKV = 768
HEAD = 64
D_DECAY_LORA = 64
D_AAA_LORA = 64
D_GATE_LORA = 128
D_RWKV_IN = 3 * D_RWKV + D_DECAY_LORA + D_AAA_LORA + D_GATE_LORA
D_IN = D_POOL + D_RWKV_IN
N_GROUPS = 4
EXPERTS_PER_GROUP = 8
N_EXPERTS = 32
D_EXPERT = 256
LN_EPS = 1e-5
LNX_EPS = 64e-5
ALPHA = 2.0 ** 0.25
DECAY_SCALE = 0.6065306597126334
KK_NORM_FLOOR = 1e-24

LANES = 128
VMEM_LIMIT = 56 * 1024 * 1024

PAIR = 2 * HEAD
N_PAIRS = D_RWKV // PAIR
CHUNK = 64
RWKV_TILE = 8 * CHUNK
TS = 512
TM = 1024
BM = 256
DMA_UNROLL = 8
EXPERT_LANE0 = 32
D_EXT = D_MODEL + LANES
META_CAT = 8
META_RANK = 9
PAIRS_PER_GROUP = EXPERTS_PER_GROUP * (EXPERTS_PER_GROUP - 1) // 2
N_CATS = N_GROUPS * PAIRS_PER_GROUP


def _pair_member():
    member = np.zeros((N_CATS, EXPERTS_PER_GROUP), bool)
    for g in range(N_GROUPS):
        pair = 0
        for lo in range(EXPERTS_PER_GROUP):
            for hi in range(lo + 1, EXPERTS_PER_GROUP):
                member[g * PAIRS_PER_GROUP + pair, [lo, hi]] = True
                pair += 1
    return member


_PAIR_MEMBER = _pair_member()


def _dot(a, b):
    return jnp.dot(a.astype(BF16), b.astype(BF16), preferred_element_type=F32)


def _dot_nt(a, b):
    return lax.dot_general(a.astype(BF16), b.astype(BF16), (((1,), (1,)), ((), ())),
                           preferred_element_type=F32)


def _dot_tn(a, b):
    return lax.dot_general(a.astype(BF16), b.astype(BF16), (((0,), (0,)), ((), ())),
                           preferred_element_type=F32)


def _head_sum(x, ones_bd):
    parts = [_dot(x[:, p * PAIR:(p + 1) * PAIR], ones_bd) for p in range(N_PAIRS)]
    return jnp.concatenate(parts, axis=1)


def _sigmoid(x):
    return 1.0 / (1.0 + jnp.exp(-x))


def _prep_kernel(x_ref, win_ref, mu_ref, w0_ref, a0_ref, lora_ref, gup_ref, kk_ref, ka_ref,
                 poolw_ref, pools_ref, ones_ref, rk_ref,
                 pool_o, r_o, w_o, k_o, v_o, a_o, b_o, g_o, bv_o,
                 proj_ref, halo_ref):
    i = pl.program_id(1)
    ts = x_ref.shape[1]

    @pl.when(i == 0)
    def _():
        halo_ref[...] = jnp.zeros_like(halo_ref)

    proj_ref[...] = jnp.dot(x_ref[0].astype(BF16), win_ref[...], preferred_element_type=F32)

    row = lax.broadcasted_iota(jnp.int32, (ts, 1), 0)

    def shifted(off, width):
        z = proj_ref[:, off:off + width]
        prev = jnp.where(row == 0, halo_ref[POOL_HALO - 1:POOL_HALO, off:off + width],
                         pltpu.roll(z, 1, 0))
        return z + (prev - z) * mu_ref[:, off - D_POOL:off - D_POOL + width]

    p = proj_ref[:, 0:D_POOL]
    ext = jnp.concatenate([halo_ref[:, 0:D_POOL], p], axis=0)
    s2 = ext + pltpu.roll(ext, 1, 0)
    s4 = s2 + pltpu.roll(s2, 2, 0)
    s8 = s4 + pltpu.roll(s4, 4, 0)
    s16 = s8 + pltpu.roll(s8, 8, 0)
    lane = lax.broadcasted_iota(jnp.int32, (ts, D_POOL), 1)
    grp = lane // POOL_GROUP
    wsum = jnp.where(grp == 0, s2[POOL_HALO:], jnp.where(grp == 1, s4[POOL_HALO:],
                     jnp.where(grp == 2, s8[POOL_HALO:], s16[POOL_HALO:])))
    win = jnp.where(grp == 0, 2.0, jnp.where(grp == 1, 4.0, jnp.where(grp == 2, 8.0, 16.0)))
    pos = (i * ts + row + 1).astype(F32)
    diff = wsum / jnp.minimum(pos, win) - p
    pool_o[0] = _dot(diff, poolw_ref[...]) * pools_ref[...]

    o = D_POOL
    r = shifted(o, D_RWKV)
    k = shifted(o + D_RWKV, D_RWKV)
    v = shifted(o + 2 * D_RWKV, D_RWKV)
    lw = shifted(o + 3 * D_RWKV, D_DECAY_LORA + D_AAA_LORA)
    gd = shifted(o + 3 * D_RWKV + D_DECAY_LORA + D_AAA_LORA, D_GATE_LORA)

    lane128 = lax.broadcasted_iota(jnp.int32, (ts, LANES), 1)
    lora_in = jnp.where(lane128 < D_DECAY_LORA, jnp.tanh(lw), lw)
    lora = _dot(lora_in, lora_ref[...])
    w_o[0] = -DECAY_SCALE * _sigmoid(w0_ref[...] + lora[:, 0:D_RWKV])
    eta = _sigmoid(a0_ref[...] + lora[:, D_RWKV:2 * D_RWKV])
    g_o[0] = _dot(_sigmoid(gd), gup_ref[...]).astype(g_o.dtype)
    kk = k * kk_ref[...]
    ss = _head_sum(kk * kk, ones_ref[...])
    kkn = kk * lax.rsqrt(jnp.maximum(ss, KK_NORM_FLOOR))
    k_mod = k * (1.0 + (eta - 1.0) * ka_ref[...])
    r_o[0] = r
    v_o[0] = v.astype(v_o.dtype)
    k_o[0] = k_mod
    bv_o[0] = (_head_sum(r * k_mod * rk_ref[...], ones_ref[...]) * v).astype(bv_o.dtype)
    a_o[0] = -kkn
    b_o[0] = kkn * eta

    halo_ref[...] = proj_ref[ts - POOL_HALO:ts, :]


def _prep_call(x, win_bf, mu, w0, a0, lora_w, gup_bf, k_k, k_a, poolw_bd, pool_scale, ones_bd, rk):
    b, s, _ = x.shape
    grid = (b, s // TS)
    full = lambda arr: pl.BlockSpec(arr.shape, lambda bi, i: (0,) * arr.ndim)
    tok = lambda width: pl.BlockSpec((1, TS, width), lambda bi, i: (bi, i, 0))
    dtypes = (F32, F32, F32, BF16, F32, F32, F32, F32)
    outs = ([jax.ShapeDtypeStruct((b, s, D_POOL), F32)]
            + [jax.ShapeDtypeStruct((b, s, D_RWKV), dt) for dt in dtypes])
    params = (win_bf, mu, w0, a0, lora_w, gup_bf, k_k, k_a, poolw_bd, pool_scale, ones_bd, rk)
    return pl.pallas_call(
        _prep_kernel,
        grid=grid,
        in_specs=[tok(D_MODEL)] + [full(a) for a in params],
        out_specs=[tok(D_POOL)] + [tok(D_RWKV)] * 8,
        out_shape=outs,
        scratch_shapes=[pltpu.VMEM((TS, D_IN), F32), pltpu.VMEM((POOL_HALO, D_IN), F32)],
        compiler_params=pltpu.CompilerParams(
            dimension_semantics=("parallel", "arbitrary"), vmem_limit_bytes=VMEM_LIMIT),
        name="prep",
    )(x, *params)


def _rwkv_kernel(r_ref, w_ref, k_ref, v_ref, a_ref, b_ref, o_ref, h_ref):
    c = pl.program_id(1)

    @pl.when(c == 0)
    def _():
        h_ref[...] = jnp.zeros_like(h_ref)

    L = CHUNK
    n_chunks = r_ref.shape[1] // L
    row = lax.broadcasted_iota(jnp.int32, (L, PAIR), 0)
    lane = lax.broadcasted_iota(jnp.int32, (L, PAIR), 1)
    head0 = lane < HEAD
    strict = row > (lane & (L - 1))
    incl = row >= (lane & (L - 1))
    eye_wide = jnp.where(row == (lane & (L - 1)), 1.0, 0.0)

    def expand(x):
        xb = x.astype(BF16)
        zero = jnp.zeros_like(xb)
        return jnp.concatenate([jnp.where(head0, xb, zero), jnp.where(head0, zero, xb)], axis=0)

    def tile(ref, j, p):
        return ref[0, j * L:(j + 1) * L, p * PAIR:(p + 1) * PAIR]

    streams = [(j, p) for j in range(n_chunks) for p in range(N_PAIRS)]
    st = []
    for j, p in streams:
        w = tile(w_ref, j, p)
        cum = w
        for sh in (1, 2, 4, 8, 16, 32):
            cum = cum + jnp.where(row >= sh, pltpu.roll(cum, sh, 0), 0.0)
        tot = cum[L - 1:L, :]
        e_neg = jnp.exp(-cum)
        e_rem = jnp.exp(tot - cum)
        b = tile(b_ref, j, p)
        k = tile(k_ref, j, p)
        a_n = tile(a_ref, j, p) * jnp.exp(cum - w)
        st.append(dict(
            ar=jnp.concatenate([a_n, tile(r_ref, j, p) * jnp.exp(cum)], axis=0).astype(BF16),
            v_e=expand(tile(v_ref, j, p)),
            bk_t=jnp.concatenate([expand(b * e_neg), expand(k * e_neg)], axis=0),
            bk_h=jnp.concatenate([expand(b * e_rem), expand(k * e_rem)], axis=0),
            w_tot=jnp.exp(tot)))
    for d in st:
        sc = _dot_nt(d['ar'], d['bk_t'])
        d['t'] = jnp.where(strict, sc[0:L, 0:PAIR], 0.0)
        d['t_ak'] = jnp.where(strict, sc[0:L, PAIR:2 * PAIR], 0.0)
        d['r_all'] = jnp.concatenate([d['ar'][L:2 * L],
                                      jnp.where(incl, sc[L:2 * L, 0:PAIR], 0.0).astype(BF16),
                                      jnp.where(incl, sc[L:2 * L, PAIR:2 * PAIR], 0.0).astype(BF16)],
                                     axis=1)
    for d in st:
        d['takv'] = _dot(d['t_ak'], d['v_e'])
        d['m'] = eye_wide + d['t']
        d['t'] = _dot(d['t'], expand(d['t']))
    for step in range(5):
        for d in st:
            if step < 4:
                both = _dot(d['t'], jnp.concatenate([expand(d['t']), expand(d['m'])], axis=1))
                d['t'] = both[:, 0:PAIR]
                d['m'] = d['m'] + both[:, PAIR:2 * PAIR]
            else:
                d['m'] = d['m'] + _dot(d['t'], expand(d['m']))
    for d in st:
        d['pq'] = _dot(d['m'], jnp.concatenate([expand(d['ar'][0:L]), expand(d['takv'])], axis=1))

    h = [h_ref[p] for p in range(N_PAIRS)]
    for j in range(n_chunks):
        ds = st[j * N_PAIRS:(j + 1) * N_PAIRS]
        hb = [hp.astype(BF16) for hp in h]
        u_e = [expand(_dot(d['pq'][:, 0:PAIR], hb[p]) + d['pq'][:, PAIR:2 * PAIR])
               for p, d in enumerate(ds)]
        for p, d in enumerate(ds):
            o_ref[0, j * L:(j + 1) * L, p * PAIR:(p + 1) * PAIR] = _dot(
                d['r_all'], jnp.concatenate([hb[p], u_e[p], d['v_e']], axis=0))
        for p, d in enumerate(ds):
            h_add = _dot_tn(d['bk_h'], jnp.concatenate([u_e[p], d['v_e']], axis=0))
            w_col = jnp.transpose(jnp.broadcast_to(d['w_tot'], (PAIR, PAIR)))
            h[p] = h[p] * w_col + h_add
    for p in range(N_PAIRS):
        h_ref[p] = h[p]


def _rwkv_call(r, w, k, v, a, b):
    bsz, s, _ = r.shape
    spec = pl.BlockSpec((1, RWKV_TILE, D_RWKV), lambda bi, c: (bi, c, 0))
    return pl.pallas_call(
        _rwkv_kernel,
        grid=(bsz, s // RWKV_TILE),
        in_specs=[spec] * 6,
        out_specs=spec,
        out_shape=jax.ShapeDtypeStruct((bsz, s, D_RWKV), F32),
        scratch_shapes=[pltpu.VMEM((N_PAIRS, PAIR, PAIR), F32)],
        compiler_params=pltpu.CompilerParams(
            dimension_semantics=("parallel", "arbitrary"), vmem_limit_bytes=VMEM_LIMIT),
        name="rwkv",
    )(r, w, k, v, a, b)


def _layer_norm(x, g, b):
    mu = jnp.mean(x, axis=-1, keepdims=True)
    xc = x - mu
    var = jnp.mean(xc * xc, axis=-1, keepdims=True)
    return xc * lax.rsqrt(var + LN_EPS) * g + b


def _post_kernel(x_ref, pool_ref, o_ref, bv_ref, g_ref,
                 lnxg_ref, lnxb_ref, ones_ref, wop_ref, wor_ref, ln1g_ref, ln1b_ref,
                 rw_hi_ref, rw_lo_ref, rb_ref, tri_ref, eg_ref, eu_ref, ed_ref,
                 h_o, meta_t_o, cnt_o, eg_o, eu_o, ed_o, base_ref):
    i = pl.program_id(0)
    ts = x_ref.shape[0]

    @pl.when(i == 0)
    def _():
        base_ref[...] = jnp.zeros_like(base_ref)

    eg_o[...] = eg_ref[...].astype(BF16)
    eu_o[...] = eu_ref[...].astype(BF16)
    ed_o[...] = ed_ref[...].astype(BF16)

    ones = ones_ref[...]
    o = o_ref[...]
    mu = _head_sum(o, ones) * (1.0 / HEAD)
    oc = o - mu
    var = _head_sum(oc * oc, ones) * (1.0 / HEAD)
    y = oc * lax.rsqrt(var + LNX_EPS) * lnxg_ref[...] + lnxb_ref[...]
    y = (y + bv_ref[...]) * g_ref[...]
    mixed = _dot(pool_ref[...], wop_ref[...]) + _dot(y, wor_ref[...])
    h = _layer_norm(ALPHA * x_ref[...] + mixed, ln1g_ref[...], ln1b_ref[...])
    h_o[:, 0:D_MODEL] = h

    h_hi = h.astype(BF16)
    h_lo = (h - h_hi.astype(F32)).astype(BF16)
    logits = (_dot_nt(rw_hi_ref[...], h_hi) + _dot_nt(rw_hi_ref[...], h_lo)
              + _dot_nt(rw_lo_ref[...], h_hi)) + rb_ref[...]
    neg = -jnp.inf
    row8 = lax.broadcasted_iota(jnp.int32, (8, ts), 0)
    gl = jnp.where(row8 < N_GROUPS, logits[0:8], neg)
    gmax = jnp.max(gl, axis=0, keepdims=True)
    g_idx = jnp.min(jnp.where(gl == gmax, row8, LANES), axis=0, keepdims=True)
    g_top_p = 1.0 / jnp.sum(jnp.exp(gl - gmax), axis=0, keepdims=True)
    el = logits[EXPERT_LANE0:EXPERT_LANE0 + EXPERTS_PER_GROUP]
    for g in range(1, N_GROUPS):
        lo_row = EXPERT_LANE0 + g * EXPERTS_PER_GROUP
        el = jnp.where(g_idx == g, logits[lo_row:lo_row + EXPERTS_PER_GROUP], el)
    m1 = jnp.max(el, axis=0, keepdims=True)
    l1 = jnp.min(jnp.where(el == m1, row8, LANES), axis=0, keepdims=True)
    el2 = jnp.where(row8 == l1, neg, el)
    m2 = jnp.max(el2, axis=0, keepdims=True)
    l2 = jnp.min(jnp.where(el2 == m2, row8, LANES), axis=0, keepdims=True)
    e21 = jnp.exp(m2 - m1)
    wgt1 = g_top_p / (1.0 + e21)
    wgt2 = g_top_p * e21 / (1.0 + e21)

    lo = jnp.minimum(l1, l2)
    hi = jnp.maximum(l1, l2)
    cat = g_idx * PAIRS_PER_GROUP + jnp.right_shift(lo * (2 * EXPERTS_PER_GROUP - 1 - lo), 1) + (hi - lo - 1)
    row = lax.broadcasted_iota(jnp.int32, (LANES, ts), 0)
    in_cat = row == cat
    onehot = jnp.where(in_cat, 1.0, 0.0)
    before = jnp.dot(onehot.astype(BF16), tri_ref[...], preferred_element_type=F32)
    posn = base_ref[...] + before
    rank = jnp.sum(jnp.where(in_cat, posn, 0.0), axis=0, keepdims=True)
    new_base = base_ref[...] + jnp.sum(onehot, axis=1, keepdims=True)
    base_ref[...] = new_base
    cnt_o[...] = jnp.broadcast_to(new_base, cnt_o.shape)

    meta_t = jnp.where(row8 == l1, wgt1, jnp.where(row8 == l2, wgt2, 0.0))
    tail_t = jnp.where(row8 == 0, cat.astype(F32), jnp.where(row8 == 1, rank, 0.0))
    meta_t_o[...] = tail_t
    rec_t = jnp.concatenate([meta_t, tail_t, jnp.zeros((LANES - 16, ts), F32)], axis=0)
    h_o[:, D_MODEL:D_EXT] = jnp.transpose(rec_t)


def _post_call(x2, pool2, o2, bv2, g2, lnx_g, lnx_b, ones_bd, wo_pool, wo_rwkv,
               ln1_g, ln1_b, rw_hi, rw_lo, rb, tri, exp_gate, exp_up, exp_down):
    n = x2.shape[0]
    n_steps = n // TS
    assert N_EXPERTS % n_steps == 0
    epb = N_EXPERTS // n_steps
    full = lambda arr: pl.BlockSpec(arr.shape, lambda i: (0,) * arr.ndim)
    tok = lambda width: pl.BlockSpec((TS, width), lambda i: (i, 0))
    wblk = lambda arr: pl.BlockSpec((epb,) + arr.shape[1:], lambda i: (i, 0, 0))
    wout = lambda arr: jax.ShapeDtypeStruct(arr.shape, BF16)
    params = (lnx_g, lnx_b, ones_bd, wo_pool, wo_rwkv, ln1_g, ln1_b, rw_hi, rw_lo, rb, tri)
    experts = (exp_gate, exp_up, exp_down)
    return pl.pallas_call(
        _post_kernel,
        grid=(n_steps,),
        in_specs=([tok(D_MODEL), tok(D_POOL)] + [tok(D_RWKV)] * 3 + [full(a) for a in params]
                  + [wblk(a) for a in experts]),
        out_specs=[tok(D_EXT), pl.BlockSpec((8, TS), lambda i: (0, i)),
                   pl.BlockSpec((LANES, LANES), lambda i: (0, 0))] + [wblk(a) for a in experts],
        out_shape=[jax.ShapeDtypeStruct((n, D_EXT), F32), jax.ShapeDtypeStruct((8, n), F32),
                   jax.ShapeDtypeStruct((LANES, LANES), F32)] + [wout(a) for a in experts],
        scratch_shapes=[pltpu.VMEM((LANES, 1), F32)],
        compiler_params=pltpu.CompilerParams(
            dimension_semantics=("arbitrary",), vmem_limit_bytes=VMEM_LIMIT),
        name="post",
    )(x2, pool2, o2, bv2, g2, *params, *experts)


SLAB = 8


def _start_rows(ts, row_copy):
    def issue(g, carry):
        for u in range(SLAB):
            row_copy(g, u).start(priority=u % 2)
        return carry

    lax.fori_loop(0, ts // SLAB, issue, 0)


def _dispatch_kernel(pad_start_ref, pad_cnt_ref, n_used_ref, dest_ref, h_ref, xs_ref, zero_ref,
                     stage_ref, sem, zsem):
    i = pl.program_id(0)
    ts = h_ref.shape[0]
    n_blocks = xs_ref.shape[0] * SLAB // BM
    slot = lax.rem(i, 2)

    def wait_tile(s):
        pltpu.make_async_copy(stage_ref.at[s], xs_ref.at[pl.ds(0, ts // SLAB)], sem.at[s]).wait()

    def zero_row_copy(row):
        return pltpu.make_async_copy(zero_ref.at[0, pl.ds(0, 1)],
                                     xs_ref.at[row // SLAB, pl.ds(lax.rem(row, SLAB), 1)], zsem)

    def zero_block_copy(blk):
        return pltpu.make_async_copy(zero_ref, xs_ref.at[pl.ds(blk * (BM // SLAB), BM // SLAB)], zsem)

    @pl.when(i == 0)
    def _():
        zero_ref[...] = jnp.zeros_like(zero_ref)

        def fill(e, carry):
            start = pad_start_ref[e]

            def one(j, c):
                zero_row_copy(start + j).start()
                return c

            return lax.fori_loop(0, pad_cnt_ref[e], one, carry)

        def fill_wait(e, carry):
            def one(j, c):
                zero_row_copy(0).wait()
                return c

            return lax.fori_loop(0, pad_cnt_ref[e], one, carry)

        def tail(blk, carry):
            zero_block_copy(blk).start()
            return carry

        def tail_wait(blk, carry):
            zero_block_copy(blk).wait()
            return carry

        lax.fori_loop(0, N_GROUPS, fill, 0)
        lax.fori_loop(n_used_ref[0], n_blocks, tail, 0)
        lax.fori_loop(0, N_GROUPS, fill_wait, 0)
        lax.fori_loop(n_used_ref[0], n_blocks, tail_wait, 0)

    for s in range(2):
        @pl.when(slot == s)
        def _(s=s):
            stage_ref[s] = h_ref[...].reshape(ts // SLAB, SLAB, D_EXT)
            _start_rows(ts, lambda g, u: pltpu.make_async_copy(
                stage_ref.at[s, g, pl.ds(u, 1)],
                xs_ref.at[dest_ref[0, 0, g * SLAB + u], pl.ds(dest_ref[0, 0, ts + g * SLAB + u], 1)],
                sem.at[s]))

    @pl.when(i > 0)
    def _():
        wait_tile(1 - slot)

    @pl.when(i == pl.num_programs(0) - 1)
    def _():
        wait_tile(slot)


def _dispatch_call(pad_start, pad_cnt, n_used, dest, h_ext, m_pad):
    n = h_ext.shape[0]
    grid_spec = pltpu.PrefetchScalarGridSpec(
        num_scalar_prefetch=3,
        grid=(n // TM,),
        in_specs=[pl.BlockSpec((1, 1, 2 * TM), lambda i, *_: (i, 0, 0), memory_space=pltpu.SMEM),
                  pl.BlockSpec((TM, D_EXT), lambda i, *_: (i, 0))],
        out_specs=pl.BlockSpec(memory_space=pl.ANY),
        scratch_shapes=[pltpu.VMEM((BM // SLAB, SLAB, D_EXT), F32),
                        pltpu.VMEM((2, TM // SLAB, SLAB, D_EXT), F32),
                        pltpu.SemaphoreType.DMA((2,)), pltpu.SemaphoreType.DMA(())],
    )
    return pl.pallas_call(
        _dispatch_kernel,
        grid_spec=grid_spec,
        out_shape=jax.ShapeDtypeStruct((m_pad // SLAB, SLAB, D_EXT), F32),
        compiler_params=pltpu.CompilerParams(
            dimension_semantics=("arbitrary",), vmem_limit_bytes=VMEM_LIMIT,
            disable_bounds_checks=True),
        name="dispatch",
    )(pad_start, pad_cnt, n_used, dest, h_ext)


def _expert_kernel(blk_g_ref, n_used_ref, act_ref, xs_ref, wg_ref, wu_ref, wd_ref, ys_ref, xb_ref):
    i = pl.program_id(0)

    @pl.when(i < n_used_ref[0])
    def _():
        xb_ref[...] = xs_ref[:, 0:D_MODEL].astype(BF16)

        def expert_out(e):
            xb = xb_ref[...]
            gate = jnp.dot(xb, wg_ref[0, e], preferred_element_type=F32)
            up = jnp.dot(xb, wu_ref[0, e], preferred_element_type=F32)
            w_e = xs_ref[:, D_MODEL + e:D_MODEL + e + 1]
            hid = jnp.where(w_e != 0.0, gate * _sigmoid(gate) * up * w_e, 0.0)
            return jnp.dot(hid.astype(BF16), wd_ref[0, e * D_EXPERT:(e + 1) * D_EXPERT],
                           preferred_element_type=F32)

        for e in range(EXPERTS_PER_GROUP):
            flag = act_ref[i * EXPERTS_PER_GROUP + e]

            @pl.when(flag == 2)
            def _(e=e):
                ys_ref[...] = expert_out(e)

            @pl.when(flag == 1)
            def _(e=e):
                ys_ref[...] += expert_out(e)

    @pl.when(i >= n_used_ref[0])
    def _():
        ys_ref[...] = jnp.zeros_like(ys_ref)


def _expert_call(blk_g, n_used, act, xs, wg, wu, wd):
    m_pad = xs.shape[0]
    n_blocks = m_pad // BM
    row_blk = lambda i, bg, nu, ac: (jnp.minimum(i, nu[0] - 1), 0)
    out_blk = lambda i, bg, nu, ac: (i, 0)
    grp = lambda i, bg, nu: bg[jnp.minimum(i, nu[0] - 1)]
    grid_spec = pltpu.PrefetchScalarGridSpec(
        num_scalar_prefetch=3,
        grid=(n_blocks,),
        in_specs=[pl.BlockSpec((BM, D_EXT), row_blk),
                  pl.BlockSpec((1, EXPERTS_PER_GROUP, D_MODEL, D_EXPERT),
                               lambda i, bg, nu, ac: (grp(i, bg, nu), 0, 0, 0)),
                  pl.BlockSpec((1, EXPERTS_PER_GROUP, D_MODEL, D_EXPERT),
                               lambda i, bg, nu, ac: (grp(i, bg, nu), 0, 0, 0)),
                  pl.BlockSpec((1, EXPERTS_PER_GROUP * D_EXPERT, D_MODEL),
                               lambda i, bg, nu, ac: (grp(i, bg, nu), 0, 0))],
        out_specs=pl.BlockSpec((BM, D_MODEL), out_blk),
        scratch_shapes=[pltpu.VMEM((BM, D_MODEL), BF16)],
    )
    return pl.pallas_call(
        _expert_kernel,
        grid_spec=grid_spec,
        out_shape=jax.ShapeDtypeStruct((m_pad, D_MODEL), F32),
        compiler_params=pltpu.CompilerParams(
            dimension_semantics=("arbitrary",), vmem_limit_bytes=VMEM_LIMIT),
        name="experts",
    )(blk_g, n_used, act, xs, wg, wu, wd)


def _combine_kernel(dest_ref, dest_next_ref, h_ref, ys_ref, g_ref, b_ref, out_ref, y_ref, sem):
    i = pl.program_id(0)
    n_steps = pl.num_programs(0)
    ts = h_ref.shape[0]

    def gather(idx_ref, buf):
        _start_rows(ts, lambda g, u: pltpu.make_async_copy(
            ys_ref.at[idx_ref[0, 0, g * SLAB + u], pl.ds(idx_ref[0, 0, ts + g * SLAB + u], 1)],
            y_ref.at[buf, g, pl.ds(u, 1)], sem.at[buf]))

    cur = lax.rem(i, 2)

    @pl.when(i == 0)
    def _():
        gather(dest_ref, 0)

    for buf in range(2):
        @pl.when((i + 1 < n_steps) & (cur != buf))
        def _(buf=buf):
            gather(dest_next_ref, buf)

    pltpu.make_async_copy(ys_ref.at[pl.ds(0, ts // SLAB)], y_ref.at[cur], sem.at[cur]).wait()
    y = y_ref[cur].reshape(ts, D_MODEL)
    out_ref[...] = _layer_norm(ALPHA * h_ref[...] + y, g_ref[...], b_ref[...])


def _combine_call(dest, h_ext, ys, ln2_g, ln2_b):
    n = h_ext.shape[0]
    full = lambda arr: pl.BlockSpec(arr.shape, lambda i: (0,) * arr.ndim)
    return pl.pallas_call(
        _combine_kernel,
        grid=(n // TM,),
        in_specs=[pl.BlockSpec((1, 1, 2 * TM), lambda i: (i, 0, 0), memory_space=pltpu.SMEM),
                  pl.BlockSpec((1, 1, 2 * TM), lambda i: (jnp.minimum(i + 1, n // TM - 1), 0, 0),
                               memory_space=pltpu.SMEM),
                  pl.BlockSpec((TM, D_MODEL), lambda i: (i, 0)),
                  pl.BlockSpec(memory_space=pl.ANY),
                  full(ln2_g), full(ln2_b)],
        out_specs=pl.BlockSpec((TM, D_MODEL), lambda i: (i, 0)),
        out_shape=jax.ShapeDtypeStruct((n, D_MODEL), F32),
        scratch_shapes=[pltpu.VMEM((2, TM // SLAB, SLAB, D_MODEL), F32),
                        pltpu.SemaphoreType.DMA((2,))],
        compiler_params=pltpu.CompilerParams(
            dimension_semantics=("arbitrary",), vmem_limit_bytes=VMEM_LIMIT,
            disable_bounds_checks=True),
        name="combine",
    )(dest, dest, h_ext, ys, ln2_g, ln2_b)


def _block_diag(blocks):
    g, c, d = blocks.shape
    eye = jnp.eye(g, dtype=blocks.dtype)
    return (eye[:, None, :, None] * blocks[:, :, None, :]).reshape(g * c, g * d)


def _layer(x, w_in, pool_w, pool_scale, mu_shift, w0, w_up, a0, a_up, g_up, k_k, k_a, r_k,
           lnx_g, lnx_b, w_out, ln1_g, ln1_b, router_group, router_group_b, router_expert,
           router_expert_b, exp_gate, exp_up, exp_down, ln2_g, ln2_b):
    b, s, d = x.shape
    n = b * s
    row2 = lambda t: t.reshape(1, -1)

    ones_bd = _block_diag(jnp.ones((2, HEAD, HEAD), BF16))
    lora_w = _block_diag(jnp.stack([w_up, a_up])).astype(BF16)
    poolw_bd = _block_diag(pool_w).astype(BF16)
    rw = jnp.zeros((LANES, D_MODEL), F32)
    rw = rw.at[0:N_GROUPS].set(router_group.T).at[EXPERT_LANE0:EXPERT_LANE0 + N_EXPERTS].set(router_expert.T)
    rw_hi = rw.astype(BF16)
    rw_lo = (rw - rw_hi.astype(F32)).astype(BF16)
    rb = jnp.zeros((LANES, 1), F32)
    rb = rb.at[0:N_GROUPS, 0].set(router_group_b).at[EXPERT_LANE0:EXPERT_LANE0 + N_EXPERTS, 0].set(router_expert_b)
    tri = (lax.broadcasted_iota(jnp.int32, (TS, TS), 0) < lax.broadcasted_iota(jnp.int32, (TS, TS), 1)).astype(BF16)

    pool_out, r, w, k, v, a_vec, b_vec, g, bv = _prep_call(
        x, w_in.astype(BF16), row2(mu_shift), row2(w0), row2(a0), lora_w, g_up.astype(BF16),
        row2(k_k), row2(k_a), poolw_bd, row2(pool_scale), ones_bd, row2(r_k))
    o = _rwkv_call(r, w, k, v, a_vec, b_vec)

    flat = lambda t: t.reshape(n, t.shape[-1])
    w_out_bf = w_out.astype(BF16)
    h_ext, meta_t, counts, eg_bf, eu_bf, ed_bf = _post_call(
        flat(x), flat(pool_out), flat(o), flat(bv), flat(g),
        row2(lnx_g), row2(lnx_b), ones_bd, w_out_bf[0:D_POOL], w_out_bf[D_POOL:],
        row2(ln1_g), row2(ln1_b), rw_hi, rw_lo, rb, tri, exp_gate, exp_up, exp_down)

    cnt_cat = counts[0:N_CATS, 0].astype(jnp.int32).reshape(N_GROUPS, PAIRS_PER_GROUP)
    cnt = jnp.sum(cnt_cat, axis=1)
    padded = (cnt + BM - 1) // BM * BM
    pend = jnp.cumsum(padded)
    pstart = pend - padded
    cat_start = (pstart[:, None] + jnp.cumsum(cnt_cat, axis=1) - cnt_cat).reshape(N_CATS)
    cnt_cat = cnt_cat.reshape(N_CATS)
    cats = meta_t[0].astype(jnp.int32)
    onehot = cats[:, None] == jnp.arange(N_CATS, dtype=jnp.int32)
    dest = jnp.sum(jnp.where(onehot, cat_start, 0), axis=-1) + meta_t[1].astype(jnp.int32)
    dest = jnp.concatenate([(dest // SLAB).reshape(n // TM, 1, TM),
                            (dest % SLAB).reshape(n // TM, 1, TM)], axis=2)
    m_pad = n + N_GROUPS * BM
    n_blocks = m_pad // BM
    blk_start = jnp.arange(n_blocks, dtype=jnp.int32) * BM
    blk_g = jnp.minimum(jnp.sum(blk_start[:, None] >= pend[None, :], axis=1), N_GROUPS - 1).astype(jnp.int32)
    n_used = (pend[-1:] // BM).astype(jnp.int32)
    overlap = ((cat_start[None, :] < blk_start[:, None] + BM)
               & (cat_start[None, :] + cnt_cat[None, :] > blk_start[:, None])
               & (cnt_cat[None, :] > 0))
    act = jnp.any(overlap[:, :, None] & _PAIR_MEMBER[None], axis=1).astype(jnp.int32)
    first = jnp.cumsum(act, axis=1) == 1
    act = (act + (first & (act == 1))).reshape(-1)

    grouped = lambda wts: wts.reshape(N_GROUPS, EXPERTS_PER_GROUP, *wts.shape[1:])
    wd = ed_bf.reshape(N_GROUPS, EXPERTS_PER_GROUP * D_EXPERT, D_MODEL)
    xs = _dispatch_call(pstart + cnt, padded - cnt, n_used, dest, h_ext, m_pad)
    xs = xs.reshape(m_pad, D_EXT)
    ys = _expert_call(blk_g, n_used, act, xs, grouped(eg_bf), grouped(eu_bf), wd)
    out = _combine_call(dest, h_ext, ys.reshape(m_pad // SLAB, SLAB, D_MODEL),
                        row2(ln2_g), row2(ln2_b))
    return out.reshape(b, s, d)


def kernel(x, w_in, pool_w, pool_scale, mu_shift, w0, w_up, a0, a_up, g_up, k_k, k_a, r_k, lnx_g, lnx_b, w_out, ln1_g, ln1_b, router_group, router_group_b, router_expert, router_expert_b, exp_gate, exp_up, exp_down, ln2_g, ln2_b):
    depth = w_in.shape[0]
    for l in range(depth):
        x = _layer(x, w_in[l], pool_w[l], pool_scale[l], mu_shift[l], w0[l], w_up[l], a0[l],
                   a_up[l], g_up[l], k_k[l], k_a[l], r_k[l], lnx_g[l], lnx_b[l], w_out[l],
                   ln1_g[l], ln1_b[l], router_group[l], router_group_b[l], router_expert[l],
                   router_expert_b[l], exp_gate[l], exp_up[l], exp_down[l], ln2_g[l], ln2_b[l])
    return x
```

```python
import numpy as np
import jax
import jax.numpy as jnp
from jax import lax
from jax.experimental import pallas as pl
from jax.experimental.pallas import tpu as pltpu

F32 = jnp.float32
BF16 = jnp.bfloat16

D_MODEL = 1024
D_POOL = 256
POOL_WINDOWS = (2, 4, 8, 16)
POOL_GROUP = 64
POOL_HALO = 16
D_RWKV = 768
HEAD = 64
D_DECAY_LORA = 64
D_AAA_LORA = 64
D_GATE_LORA = 128
D_RWKV_IN = 3 * D_RWKV + D_DECAY_LORA + D_AAA_LORA + D_GATE_LORA
D_IN = D_POOL + D_RWKV_IN
N_GROUPS = 4
EXPERTS_PER_GROUP = 8
N_EXPERTS = 32
D_EXPERT = 256
LN_EPS = 1e-5
LNX_EPS = 64e-5
ALPHA = 2.0 ** 0.25
DECAY_SCALE = 0.6065306597126334
KK_NORM_FLOOR = 1e-24

LANES = 128
VMEM_LIMIT = 56 * 1024 * 1024

PAIR = 2 * HEAD
N_PAIRS = D_RWKV // PAIR
CHUNK = 64
RWKV_TILE = 8 * CHUNK
TS = 512
TM = 1024
BM = 256
DMA_UNROLL = 8
EXPERT_LANE0 = 32
D_EXT = D_MODEL + LANES
META_CAT = 8
META_RANK = 9
PAIRS_PER_GROUP = EXPERTS_PER_GROUP * (EXPERTS_PER_GROUP - 1) // 2
N_CATS = N_GROUPS * PAIRS_PER_GROUP


def _pair_member():
    member = np.zeros((N_CATS, EXPERTS_PER_GROUP), bool)
    for g in range(N_GROUPS):
        pair = 0
        for lo in range(EXPERTS_PER_GROUP):
            for hi in range(lo + 1, EXPERTS_PER_GROUP):
                member[g * PAIRS_PER_GROUP + pair, [lo, hi]] = True
                pair += 1
    return member


_PAIR_MEMBER = _pair_member()


def _dot(a, b):
    return jnp.dot(a.astype(BF16), b.astype(BF16), preferred_element_type=F32)


def _dot_nt(a, b):
    return lax.dot_general(a.astype(BF16), b.astype(BF16), (((1,), (1,)), ((), ())),
                           preferred_element_type=F32)


def _dot_tn(a, b):
    return lax.dot_general(a.astype(BF16), b.astype(BF16), (((0,), (0,)), ((), ())),
                           preferred_element_type=F32)


def _head_sum(x, ones_bd):
    parts = [_dot(x[:, p * PAIR:(p + 1) * PAIR], ones_bd) for p in range(N_PAIRS)]
    return jnp.concatenate(parts, axis=1)


def _sigmoid(x):
    return 1.0 / (1.0 + jnp.exp(-x))


def _prep_kernel(x_ref, win_ref, mu_ref, w0_ref, a0_ref, lora_ref, gup_ref, kk_ref, ka_ref,
                 poolw_ref, pools_ref, ones_ref, rk_ref,
                 pool_o, r_o, w_o, k_o, v_o, a_o, b_o, g_o, bv_o,
                 proj_ref, halo_ref):
    i = pl.program_id(1)
    ts = x_ref.shape[1]

    @pl.when(i == 0)
    def _():
        halo_ref[...] = jnp.zeros_like(halo_ref)

    proj_ref[...] = jnp.dot(x_ref[0].astype(BF16), win_ref[...], preferred_element_type=F32)

    row = lax.broadcasted_iota(jnp.int32, (ts, 1), 0)

    def shifted(off, width):
        z = proj_ref[:, off:off + width]
        prev = jnp.where(row == 0, halo_ref[POOL_HALO - 1:POOL_HALO, off:off + width],
                         pltpu.roll(z, 1, 0))
        return z + (prev - z) * mu_ref[:, off - D_POOL:off - D_POOL + width]

    p = proj_ref[:, 0:D_POOL]
    ext = jnp.concatenate([halo_ref[:, 0:D_POOL], p], axis=0)
    s2 = ext + pltpu.roll(ext, 1, 0)
    s4 = s2 + pltpu.roll(s2, 2, 0)
    s8 = s4 + pltpu.roll(s4, 4, 0)
    s16 = s8 + pltpu.roll(s8, 8, 0)
    lane = lax.broadcasted_iota(jnp.int32, (ts, D_POOL), 1)
    grp = lane // POOL_GROUP
    wsum = jnp.where(grp == 0, s2[POOL_HALO:], jnp.where(grp == 1, s4[POOL_HALO:],
                     jnp.where(grp == 2, s8[POOL_HALO:], s16[POOL_HALO:])))
    win = jnp.where(grp == 0, 2.0, jnp.where(grp == 1, 4.0, jnp.where(grp == 2, 8.0, 16.0)))
    pos = (i * ts + row + 1).astype(F32)
    diff = wsum / jnp.minimum(pos, win) - p
    pool_o[0] = _dot(diff, poolw_ref[...]) * pools_ref[...]

    o = D_POOL
    r = shifted(o, D_RWKV)
    k = shifted(o + D_RWKV, D_RWKV)
    v = shifted(o + 2 * D_RWKV, D_RWKV)
    lw = shifted(o + 3 * D_RWKV, D_DECAY_LORA + D_AAA_LORA)
    gd = shifted(o + 3 * D_RWKV + D_DECAY_LORA + D_AAA_LORA, D_GATE_LORA)

    lane128 = lax.broadcasted_iota(jnp.int32, (ts, LANES), 1)
    lora_in = jnp.where(lane128 < D_DECAY_LORA, jnp.tanh(lw), lw)
    lora = _dot(lora_in, lora_ref[...])
    w_o[0] = -DECAY_SCALE * _sigmoid(w0_ref[...] + lora[:, 0:D_RWKV])
    eta = _sigmoid(a0_ref[...] + lora[:, D_RWKV:2 * D_RWKV])
    g_o[0] = _dot(_sigmoid(gd), gup_ref[...]).astype(g_o.dtype)
    kk = k * kk_ref[...]
    ss = _head_sum(kk * kk, ones_ref[...])
    kkn = kk * lax.rsqrt(jnp.maximum(ss, KK_NORM_FLOOR))
    k_mod = k * (1.0 + (eta - 1.0) * ka_ref[...])
    r_o[0] = r
    v_o[0] = v.astype(v_o.dtype)
    k_o[0] = k_mod
    bv_o[0] = (_head_sum(r * k_mod * rk_ref[...], ones_ref[...]) * v).astype(bv_o.dtype)
    a_o[0] = -kkn
    b_o[0] = kkn * eta

    halo_ref[...] = proj_ref[ts - POOL_HALO:ts, :]


def _prep_call(x, win_bf, mu, w0, a0, lora_w, gup_bf, k_k, k_a, poolw_bd, pool_scale, ones_bd, rk):
    b, s, _ = x.shape
    grid = (b, s // TS)
    full = lambda arr: pl.BlockSpec(arr.shape, lambda bi, i: (0,) * arr.ndim)
    tok = lambda width: pl.BlockSpec((1, TS, width), lambda bi, i: (bi, i, 0))
    dtypes = (F32, F32, F32, BF16, F32, F32, F32, F32)
    outs = ([jax.ShapeDtypeStruct((b, s, D_POOL), F32)]
            + [jax.ShapeDtypeStruct((b, s, D_RWKV), dt) for dt in dtypes])
    params = (win_bf, mu, w0, a0, lora_w, gup_bf, k_k, k_a, poolw_bd, pool_scale, ones_bd, rk)
    return pl.pallas_call(
        _prep_kernel,
        grid=grid,
        in_specs=[tok(D_MODEL)] + [full(a) for a in params],
        out_specs=[tok(D_POOL)] + [tok(D_RWKV)] * 8,
        out_shape=outs,
        scratch_shapes=[pltpu.VMEM((TS, D_IN), F32), pltpu.VMEM((POOL_HALO, D_IN), F32)],
        compiler_params=pltpu.CompilerParams(
            dimension_semantics=("parallel", "arbitrary"), vmem_limit_bytes=VMEM_LIMIT),
        name="prep",
    )(x, *params)


def _rwkv_kernel(r_ref, w_ref, k_ref, v_ref, a_ref, b_ref, eg_ref, eu_ref, ed_ref,
                 o_ref, eg_o, eu_o, ed_o, h_ref):
    c = pl.program_id(1)

    @pl.when(c == 0)
    def _():
        h_ref[...] = jnp.zeros_like(h_ref)

    eg_o[...] = eg_ref[...].astype(BF16)
    eu_o[...] = eu_ref[...].astype(BF16)
    ed_o[...] = ed_ref[...].astype(BF16)

    L = CHUNK
    n_chunks = r_ref.shape[1] // L
    row = lax.broadcasted_iota(jnp.int32, (L, PAIR), 0)
    lane = lax.broadcasted_iota(jnp.int32, (L, PAIR), 1)
    head0 = lane < HEAD
    strict = row > (lane & (L - 1))
    incl = row >= (lane & (L - 1))
    eye_wide = jnp.where(row == (lane & (L - 1)), 1.0, 0.0)

    def expand(x):
        xb = x.astype(BF16)
        zero = jnp.zeros_like(xb)
        return jnp.concatenate([jnp.where(head0, xb, zero), jnp.where(head0, zero, xb)], axis=0)

    def tile(ref, j, p):
        return ref[0, j * L:(j + 1) * L, p * PAIR:(p + 1) * PAIR]

    streams = [(j, p) for j in range(n_chunks) for p in range(N_PAIRS)]
    st = []
    for j, p in streams:
        w = tile(w_ref, j, p)
        cum = w
        for sh in (1, 2, 4, 8, 16, 32):
            cum = cum + jnp.where(row >= sh, pltpu.roll(cum, sh, 0), 0.0)
        tot = cum[L - 1:L, :]
        e_neg = jnp.exp(-cum)
        e_rem = jnp.exp(tot - cum)
        b = tile(b_ref, j, p)
        k = tile(k_ref, j, p)
        a_n = tile(a_ref, j, p) * jnp.exp(cum - w)
        st.append(dict(
            ar=jnp.concatenate([a_n, tile(r_ref, j, p) * jnp.exp(cum)], axis=0).astype(BF16),
            v_e=expand(tile(v_ref, j, p)),
            bk_t=jnp.concatenate([expand(b * e_neg), expand(k * e_neg)], axis=0),
            bk_h=jnp.concatenate([expand(b * e_rem), expand(k * e_rem)], axis=0),
            w_tot=jnp.exp(tot)))
    for d in st:
        sc = _dot_nt(d['ar'], d['bk_t'])
        d['t'] = jnp.where(strict, sc[0:L, 0:PAIR], 0.0)
        d['t_ak'] = jnp.where(strict, sc[0:L, PAIR:2 * PAIR], 0.0)
        d['r_all'] = jnp.concatenate([d['ar'][L:2 * L],
                                      jnp.where(incl, sc[L:2 * L, 0:PAIR], 0.0).astype(BF16),
                                      jnp.where(incl, sc[L:2 * L, PAIR:2 * PAIR], 0.0).astype(BF16)],
                                     axis=1)
    for d in st:
        d['takv'] = _dot(d['t_ak'], d['v_e'])
        d['m'] = eye_wide + d['t']
        d['t'] = _dot(d['t'], expand(d['t']))
    for step in range(5):
        for d in st:
            if step < 4:
                both = _dot(d['t'], jnp.concatenate([expand(d['t']), expand(d['m'])], axis=1))
                d['t'] = both[:, 0:PAIR]
                d['m'] = d['m'] + both[:, PAIR:2 * PAIR]
            else:
                d['m'] = d['m'] + _dot(d['t'], expand(d['m']))
    for d in st:
        d['pq'] = _dot(d['m'], jnp.concatenate([expand(d['ar'][0:L]), expand(d['takv'])], axis=1))

    h = [h_ref[p] for p in range(N_PAIRS)]
    for j in range(n_chunks):
        ds = st[j * N_PAIRS:(j + 1) * N_PAIRS]
        hb = [hp.astype(BF16) for hp in h]
        u_e = [expand(_dot(d['pq'][:, 0:PAIR], hb[p]) + d['pq'][:, PAIR:2 * PAIR])
               for p, d in enumerate(ds)]
        for p, d in enumerate(ds):
            o_ref[0, j * L:(j + 1) * L, p * PAIR:(p + 1) * PAIR] = _dot(
                d['r_all'], jnp.concatenate([hb[p], u_e[p], d['v_e']], axis=0))
        for p, d in enumerate(ds):
            h_add = _dot_tn(d['bk_h'], jnp.concatenate([u_e[p], d['v_e']], axis=0))
            w_col = jnp.transpose(jnp.broadcast_to(d['w_tot'], (PAIR, PAIR)))
            h[p] = h[p] * w_col + h_add
    for p in range(N_PAIRS):
        h_ref[p] = h[p]


def _rwkv_call(r, w, k, v, a, b, exp_gate, exp_up, exp_down):
    bsz, s, _ = r.shape
    n_c = s // RWKV_TILE
    assert N_EXPERTS % (bsz * n_c) == 0
    epb = N_EXPERTS // (bsz * n_c)
    spec = pl.BlockSpec((1, RWKV_TILE, D_RWKV), lambda bi, c: (bi, c, 0))
    wblk = lambda arr: pl.BlockSpec((epb,) + arr.shape[1:], lambda bi, c: (bi * n_c + c, 0, 0))
    experts = (exp_gate, exp_up, exp_down)
    return pl.pallas_call(
        _rwkv_kernel,
        grid=(bsz, n_c),
        in_specs=[spec] * 6 + [wblk(t) for t in experts],
        out_specs=[spec] + [wblk(t) for t in experts],
        out_shape=[jax.ShapeDtypeStruct((bsz, s, D_RWKV), F32)]
                  + [jax.ShapeDtypeStruct(t.shape, BF16) for t in experts],
        scratch_shapes=[pltpu.VMEM((N_PAIRS, PAIR, PAIR), F32)],
        compiler_params=pltpu.CompilerParams(
            dimension_semantics=("parallel", "arbitrary"), vmem_limit_bytes=VMEM_LIMIT),
        name="rwkv",
    )(r, w, k, v, a, b, *experts)


def _layer_norm(x, g, b):
    mu = jnp.mean(x, axis=-1, keepdims=True)
    xc = x - mu
    var = jnp.mean(xc * xc, axis=-1, keepdims=True)
    return xc * lax.rsqrt(var + LN_EPS) * g + b


def _post_kernel(x_ref, pool_ref, o_ref, bv_ref, g_ref,
                 lnxg_ref, lnxb_ref, ones_ref, wop_ref, wor_ref, ln1g_ref, ln1b_ref,
                 rw_hi_ref, rw_lo_ref, rb_ref, tri_ref,
                 h_o, meta_t_o, cnt_o, base_ref):
    i = pl.program_id(0)
    ts = x_ref.shape[0]

    @pl.when(i == 0)
    def _():
        base_ref[...] = jnp.zeros_like(base_ref)

    ones = ones_ref[...]
    o = o_ref[...]
    mu = _head_sum(o, ones) * (1.0 / HEAD)
    oc = o - mu
    var = _head_sum(oc * oc, ones) * (1.0 / HEAD)
    y = oc * lax.rsqrt(var + LNX_EPS) * lnxg_ref[...] + lnxb_ref[...]
    y = (y + bv_ref[...]) * g_ref[...]
    mixed = _dot(pool_ref[...], wop_ref[...]) + _dot(y, wor_ref[...])
    h = _layer_norm(ALPHA * x_ref[...] + mixed, ln1g_ref[...], ln1b_ref[...])
    h_o[:, 0:D_MODEL] = h

    h_hi = h.astype(BF16)
    h_lo = (h - h_hi.astype(F32)).astype(BF16)
    logits = (_dot_nt(rw_hi_ref[...], h_hi) + _dot_nt(rw_hi_ref[...], h_lo)
              + _dot_nt(rw_lo_ref[...], h_hi)) + rb_ref[...]
    neg = -jnp.inf
    row8 = lax.broadcasted_iota(jnp.int32, (8, ts), 0)
    gl = jnp.where(row8 < N_GROUPS, logits[0:8], neg)
    gmax = jnp.max(gl, axis=0, keepdims=True)
    g_idx = jnp.min(jnp.where(gl == gmax, row8, LANES), axis=0, keepdims=True)
    g_top_p = 1.0 / jnp.sum(jnp.exp(gl - gmax), axis=0, keepdims=True)
    el = logits[EXPERT_LANE0:EXPERT_LANE0 + EXPERTS_PER_GROUP]
    for g in range(1, N_GROUPS):
        lo_row = EXPERT_LANE0 + g * EXPERTS_PER_GROUP
        el = jnp.where(g_idx == g, logits[lo_row:lo_row + EXPERTS_PER_GROUP], el)
    m1 = jnp.max(el, axis=0, keepdims=True)
    l1 = jnp.min(jnp.where(el == m1, row8, LANES), axis=0, keepdims=True)
    el2 = jnp.where(row8 == l1, neg, el)
    m2 = jnp.max(el2, axis=0, keepdims=True)
    l2 = jnp.min(jnp.where(el2 == m2, row8, LANES), axis=0, keepdims=True)
    e21 = jnp.exp(m2 - m1)
    wgt1 = g_top_p / (1.0 + e21)
    wgt2 = g_top_p * e21 / (1.0 + e21)

    lo = jnp.minimum(l1, l2)
    hi = jnp.maximum(l1, l2)
    cat = g_idx * PAIRS_PER_GROUP + jnp.right_shift(lo * (2 * EXPERTS_PER_GROUP - 1 - lo), 1) + (hi - lo - 1)
    row = lax.broadcasted_iota(jnp.int32, (LANES, ts), 0)
    in_cat = row == cat
    onehot = jnp.where(in_cat, 1.0, 0.0)
    before = jnp.dot(onehot.astype(BF16), tri_ref[...], preferred_element_type=F32)
    posn = base_ref[...] + before
    rank = jnp.sum(jnp.where(in_cat, posn, 0.0), axis=0, keepdims=True)
    new_base = base_ref[...] + jnp.sum(onehot, axis=1, keepdims=True)
    base_ref[...] = new_base
    cnt_o[...] = jnp.broadcast_to(new_base, cnt_o.shape)

    meta_t = jnp.where(row8 == l1, wgt1, jnp.where(row8 == l2, wgt2, 0.0))
    tail_t = jnp.where(row8 == 0, cat.astype(F32), jnp.where(row8 == 1, rank, 0.0))
    meta_t_o[...] = tail_t
    rec_t = jnp.concatenate([meta_t, tail_t, jnp.zeros((LANES - 16, ts), F32)], axis=0)
    h_o[:, D_MODEL:D_EXT] = jnp.transpose(rec_t)


def _post_call(x2, pool2, o2, bv2, g2, lnx_g, lnx_b, ones_bd, wo_pool, wo_rwkv,
               ln1_g, ln1_b, rw_hi, rw_lo, rb, tri):
    n = x2.shape[0]
    full = lambda arr: pl.BlockSpec(arr.shape, lambda i: (0,) * arr.ndim)
    tok = lambda width: pl.BlockSpec((TS, width), lambda i: (i, 0))
    params = (lnx_g, lnx_b, ones_bd, wo_pool, wo_rwkv, ln1_g, ln1_b, rw_hi, rw_lo, rb, tri)
    return pl.pallas_call(
        _post_kernel,
        grid=(n // TS,),
        in_specs=[tok(D_MODEL), tok(D_POOL)] + [tok(D_RWKV)] * 3 + [full(a) for a in params],
        out_specs=[tok(D_EXT), pl.BlockSpec((8, TS), lambda i: (0, i)),
                   pl.BlockSpec((LANES, LANES), lambda i: (0, 0))],
        out_shape=[jax.ShapeDtypeStruct((n, D_EXT), F32), jax.ShapeDtypeStruct((8, n), F32),
                   jax.ShapeDtypeStruct((LANES, LANES), F32)],
        scratch_shapes=[pltpu.VMEM((LANES, 1), F32)],
        compiler_params=pltpu.CompilerParams(
            dimension_semantics=("arbitrary",), vmem_limit_bytes=VMEM_LIMIT),
        name="post",
    )(x2, pool2, o2, bv2, g2, *params)


SLAB = 8


def _start_rows(ts, row_copy):
    def issue(g, carry):
        for u in range(SLAB):
            row_copy(g, u).start(priority=u % 2)
        return carry

    lax.fori_loop(0, ts // SLAB, issue, 0)


def _dispatch_kernel(pad_start_ref, pad_cnt_ref, n_used_ref, dest_ref, h_ref, xs_ref, zero_ref,
                     stage_ref, sem, zsem):
    i = pl.program_id(0)
    ts = h_ref.shape[0]
    n_blocks = xs_ref.shape[0] * SLAB // BM
    slot = lax.rem(i, 2)

    def wait_tile(s):
        pltpu.make_async_copy(stage_ref.at[s], xs_ref.at[pl.ds(0, ts // SLAB)], sem.at[s]).wait()

    def zero_row_copy(row):
        return pltpu.make_async_copy(zero_ref.at[0, pl.ds(0, 1)],
                                     xs_ref.at[row // SLAB, pl.ds(lax.rem(row, SLAB), 1)], zsem)

    def zero_block_copy(blk):
        return pltpu.make_async_copy(zero_ref, xs_ref.at[pl.ds(blk * (BM // SLAB), BM // SLAB)], zsem)

    @pl.when(i == 0)
    def _():
        zero_ref[...] = jnp.zeros_like(zero_ref)

        def fill(e, carry):
            start = pad_start_ref[e]

            def one(j, c):
                zero_row_copy(start + j).start()
                return c

            return lax.fori_loop(0, pad_cnt_ref[e], one, carry)

        def fill_wait(e, carry):
            def one(j, c):
                zero_row_copy(0).wait()
                return c

            return lax.fori_loop(0, pad_cnt_ref[e], one, carry)

        def tail(blk, carry):
            zero_block_copy(blk).start()
            return carry

        def tail_wait(blk, carry):
            zero_block_copy(blk).wait()
            return carry

        lax.fori_loop(0, N_GROUPS, fill, 0)
        lax.fori_loop(n_used_ref[0], n_blocks, tail, 0)
        lax.fori_loop(0, N_GROUPS, fill_wait, 0)
        lax.fori_loop(n_used_ref[0], n_blocks, tail_wait, 0)

    for s in range(2):
        @pl.when(slot == s)
        def _(s=s):
            stage_ref[s] = h_ref[...].reshape(ts // SLAB, SLAB, D_EXT)
            _start_rows(ts, lambda g, u: pltpu.make_async_copy(
                stage_ref.at[s, g, pl.ds(u, 1)],
                xs_ref.at[dest_ref[0, 0, g * SLAB + u], pl.ds(dest_ref[0, 0, ts + g * SLAB + u], 1)],
                sem.at[s]))

    @pl.when(i > 0)
    def _():
        wait_tile(1 - slot)

    @pl.when(i == pl.num_programs(0) - 1)
    def _():
        wait_tile(slot)


def _dispatch_call(pad_start, pad_cnt, n_used, dest, h_ext, m_pad):
    n = h_ext.shape[0]
    grid_spec = pltpu.PrefetchScalarGridSpec(
        num_scalar_prefetch=3,
        grid=(n // TM,),
        in_specs=[pl.BlockSpec((1, 1, 2 * TM), lambda i, *_: (i, 0, 0), memory_space=pltpu.SMEM),
                  pl.BlockSpec((TM, D_EXT), lambda i, *_: (i, 0))],
        out_specs=pl.BlockSpec(memory_space=pl.ANY),
        scratch_shapes=[pltpu.VMEM((BM // SLAB, SLAB, D_EXT), F32),
                        pltpu.VMEM((2, TM // SLAB, SLAB, D_EXT), F32),
                        pltpu.SemaphoreType.DMA((2,)), pltpu.SemaphoreType.DMA(())],
    )
    return pl.pallas_call(
        _dispatch_kernel,
        grid_spec=grid_spec,
        out_shape=jax.ShapeDtypeStruct((m_pad // SLAB, SLAB, D_EXT), F32),
        compiler_params=pltpu.CompilerParams(
            dimension_semantics=("arbitrary",), vmem_limit_bytes=VMEM_LIMIT,
            disable_bounds_checks=True),
        name="dispatch",
    )(pad_start, pad_cnt, n_used, dest, h_ext)


def _expert_kernel(blk_g_ref, n_used_ref, act_ref, xs_ref, wg_ref, wu_ref, wd_ref, ys_ref, xb_ref):
    i = pl.program_id(0)

    @pl.when(i < n_used_ref[0])
    def _():
        xb_ref[...] = xs_ref[:, 0:D_MODEL].astype(BF16)

        def expert_out(e):
            xb = xb_ref[...]
            gate = jnp.dot(xb, wg_ref[0, e], preferred_element_type=F32)
            up = jnp.dot(xb, wu_ref[0, e], preferred_element_type=F32)
            w_e = xs_ref[:, D_MODEL + e:D_MODEL + e + 1]
            hid = jnp.where(w_e != 0.0, gate * _sigmoid(gate) * up * w_e, 0.0)
            return jnp.dot(hid.astype(BF16), wd_ref[0, e * D_EXPERT:(e + 1) * D_EXPERT],
                           preferred_element_type=F32)

        for e in range(EXPERTS_PER_GROUP):
            flag = act_ref[i * EXPERTS_PER_GROUP + e]

            @pl.when(flag == 2)
            def _(e=e):
                ys_ref[...] = expert_out(e)

            @pl.when(flag == 1)
            def _(e=e):
                ys_ref[...] += expert_out(e)

    @pl.when(i >= n_used_ref[0])
    def _():
        ys_ref[...] = jnp.zeros_like(ys_ref)


def _expert_call(blk_g, n_used, act, xs, wg, wu, wd):
    m_pad = xs.shape[0]
    n_blocks = m_pad // BM
    row_blk = lambda i, bg, nu, ac: (jnp.minimum(i, nu[0] - 1), 0)
    out_blk = lambda i, bg, nu, ac: (i, 0)
    grp = lambda i, bg, nu: bg[jnp.minimum(i, nu[0] - 1)]
    grid_spec = pltpu.PrefetchScalarGridSpec(
        num_scalar_prefetch=3,
        grid=(n_blocks,),
        in_specs=[pl.BlockSpec((BM, D_EXT), row_blk),
                  pl.BlockSpec((1, EXPERTS_PER_GROUP, D_MODEL, D_EXPERT),
                               lambda i, bg, nu, ac: (grp(i, bg, nu), 0, 0, 0)),
                  pl.BlockSpec((1, EXPERTS_PER_GROUP, D_MODEL, D_EXPERT),
                               lambda i, bg, nu, ac: (grp(i, bg, nu), 0, 0, 0)),
                  pl.BlockSpec((1, EXPERTS_PER_GROUP * D_EXPERT, D_MODEL),
                               lambda i, bg, nu, ac: (grp(i, bg, nu), 0, 0))],
        out_specs=pl.BlockSpec((BM, D_MODEL), out_blk),
        scratch_shapes=[pltpu.VMEM((BM, D_MODEL), BF16)],
    )
    return pl.pallas_call(
        _expert_kernel,
        grid_spec=grid_spec,
        out_shape=jax.ShapeDtypeStruct((m_pad, D_MODEL), F32),
        compiler_params=pltpu.CompilerParams(
            dimension_semantics=("arbitrary",), vmem_limit_bytes=VMEM_LIMIT),
        name="experts",
    )(blk_g, n_used, act, xs, wg, wu, wd)


def _combine_kernel(dest_ref, dest_next_ref, h_ref, ys_ref, g_ref, b_ref, out_ref, y_ref, sem):
    i = pl.program_id(0)
    n_steps = pl.num_programs(0)
    ts = h_ref.shape[0]

    def gather(idx_ref, buf):
        _start_rows(ts, lambda g, u: pltpu.make_async_copy(
            ys_ref.at[idx_ref[0, 0, g * SLAB + u], pl.ds(idx_ref[0, 0, ts + g * SLAB + u], 1)],
            y_ref.at[buf, g, pl.ds(u, 1)], sem.at[buf]))

    cur = lax.rem(i, 2)

    @pl.when(i == 0)
    def _():
        gather(dest_ref, 0)

    for buf in range(2):
        @pl.when((i + 1 < n_steps) & (cur != buf))
        def _(buf=buf):
            gather(dest_next_ref, buf)

    pltpu.make_async_copy(ys_ref.at[pl.ds(0, ts // SLAB)], y_ref.at[cur], sem.at[cur]).wait()
    y = y_ref[cur].reshape(ts, D_MODEL)
    out_ref[...] = _layer_norm(ALPHA * h_ref[...] + y, g_ref[...], b_ref[...])


def _combine_call(dest, h_ext, ys, ln2_g, ln2_b):
    n = h_ext.shape[0]
    full = lambda arr: pl.BlockSpec(arr.shape, lambda i: (0,) * arr.ndim)
    return pl.pallas_call(
        _combine_kernel,
        grid=(n // TM,),
        in_specs=[pl.BlockSpec((1, 1, 2 * TM), lambda i: (i, 0, 0), memory_space=pltpu.SMEM),
                  pl.BlockSpec((1, 1, 2 * TM), lambda i: (jnp.minimum(i + 1, n // TM - 1), 0, 0),
                               memory_space=pltpu.SMEM),
                  pl.BlockSpec((TM, D_MODEL), lambda i: (i, 0)),
                  pl.BlockSpec(memory_space=pl.ANY),
                  full(ln2_g), full(ln2_b)],
        out_specs=pl.BlockSpec((TM, D_MODEL), lambda i: (i, 0)),
        out_shape=jax.ShapeDtypeStruct((n, D_MODEL), F32),
        scratch_shapes=[pltpu.VMEM((2, TM // SLAB, SLAB, D_MODEL), F32),
                        pltpu.SemaphoreType.DMA((2,))],
        compiler_params=pltpu.CompilerParams(
            dimension_semantics=("arbitrary",), vmem_limit_bytes=VMEM_LIMIT,
            disable_bounds_checks=True),
        name="combine",
    )(dest, dest, h_ext, ys, ln2_g, ln2_b)


def _block_diag(blocks):
    g, c, d = blocks.shape
    eye = jnp.eye(g, dtype=blocks.dtype)
    return (eye[:, None, :, None] * blocks[:, :, None, :]).reshape(g * c, g * d)


def _layer(x, w_in, pool_w, pool_scale, mu_shift, w0, w_up, a0, a_up, g_up, k_k, k_a, r_k,
           lnx_g, lnx_b, w_out, ln1_g, ln1_b, router_group, router_group_b, router_expert,
           router_expert_b, exp_gate, exp_up, exp_down, ln2_g, ln2_b):
    b, s, d = x.shape
    n = b * s
    row2 = lambda t: t.reshape(1, -1)

    ones_bd = _block_diag(jnp.ones((2, HEAD, HEAD), BF16))
    lora_w = _block_diag(jnp.stack([w_up, a_up])).astype(BF16)
    poolw_bd = _block_diag(pool_w).astype(BF16)
    rw = jnp.zeros((LANES, D_MODEL), F32)
    rw = rw.at[0:N_GROUPS].set(router_group.T).at[EXPERT_LANE0:EXPERT_LANE0 + N_EXPERTS].set(router_expert.T)
    rw_hi = rw.astype(BF16)
    rw_lo = (rw - rw_hi.astype(F32)).astype(BF16)
    rb = jnp.zeros((LANES, 1), F32)
    rb = rb.at[0:N_GROUPS, 0].set(router_group_b).at[EXPERT_LANE0:EXPERT_LANE0 + N_EXPERTS, 0].set(router_expert_b)
    tri = (lax.broadcasted_iota(jnp.int32, (TS, TS), 0) < lax.broadcasted_iota(jnp.int32, (TS, TS), 1)).astype(BF16)

    pool_out, r, w, k, v, a_vec, b_vec, g, bv = _prep_call(
        x, w_in.astype(BF16), row2(mu_shift), row2(w0), row2(a0), lora_w, g_up.astype(BF16),
        row2(k_k), row2(k_a), poolw_bd, row2(pool_scale), ones_bd, row2(r_k))
    o, eg_bf, eu_bf, ed_bf = _rwkv_call(r, w, k, v, a_vec, b_vec, exp_gate, exp_up, exp_down)

    flat = lambda t: t.reshape(n, t.shape[-1])
    w_out_bf = w_out.astype(BF16)
    h_ext, meta_t, counts = _post_call(
        flat(x), flat(pool_out), flat(o), flat(bv), flat(g),
        row2(lnx_g), row2(lnx_b), ones_bd, w_out_bf[0:D_POOL], w_out_bf[D_POOL:],
        row2(ln1_g), row2(ln1_b), rw_hi, rw_lo, rb, tri)

    cnt_cat = counts[0:N_CATS, 0].astype(jnp.int32).reshape(N_GROUPS, PAIRS_PER_GROUP)
    cnt = jnp.sum(cnt_cat, axis=1)
    padded = (cnt + BM - 1) // BM * BM
    pend = jnp.cumsum(padded)
    pstart = pend - padded
    cat_start = (pstart[:, None] + jnp.cumsum(cnt_cat, axis=1) - cnt_cat).reshape(N_CATS)
    cnt_cat = cnt_cat.reshape(N_CATS)
    cats = meta_t[0].astype(jnp.int32)
    onehot = cats[:, None] == jnp.arange(N_CATS, dtype=jnp.int32)
    dest = jnp.sum(jnp.where(onehot, cat_start, 0), axis=-1) + meta_t[1].astype(jnp.int32)
    dest = jnp.concatenate([(dest // SLAB).reshape(n // TM, 1, TM),
                            (dest % SLAB).reshape(n // TM, 1, TM)], axis=2)
    m_pad = n + N_GROUPS * BM
    n_blocks = m_pad // BM
    blk_start = jnp.arange(n_blocks, dtype=jnp.int32) * BM
    blk_g = jnp.minimum(jnp.sum(blk_start[:, None] >= pend[None, :], axis=1), N_GROUPS - 1).astype(jnp.int32)
    n_used = (pend[-1:] // BM).astype(jnp.int32)
    overlap = ((cat_start[None, :] < blk_start[:, None] + BM)
               & (cat_start[None, :] + cnt_cat[None, :] > blk_start[:, None])
               & (cnt_cat[None, :] > 0))
    act = jnp.any(overlap[:, :, None] & _PAIR_MEMBER[None], axis=1).astype(jnp.int32)
    first = jnp.cumsum(act, axis=1) == 1
    act = (act + (first & (act == 1))).reshape(-1)

    grouped = lambda wts: wts.reshape(N_GROUPS, EXPERTS_PER_GROUP, *wts.shape[1:])
    wd = ed_bf.reshape(N_GROUPS, EXPERTS_PER_GROUP * D_EXPERT, D_MODEL)
    xs = _dispatch_call(pstart + cnt, padded - cnt, n_used, dest, h_ext, m_pad)
    xs = xs.reshape(m_pad, D_EXT)
    ys = _expert_call(blk_g, n_used, act, xs, grouped(eg_bf), grouped(eu_bf), wd)
    out = _combine_call(dest, h_ext, ys.reshape(m_pad // SLAB, SLAB, D_MODEL),
                        row2(ln2_g), row2(ln2_b))
    return out.reshape(b, s, d)


def kernel(x, w_in, pool_w, pool_scale, mu_shift, w0, w_up, a0, a_up, g_up, k_k, k_a, r_k, lnx_g, lnx_b, w_out, ln1_g, ln1_b, router_group, router_group_b, router_expert, router_expert_b, exp_gate, exp_up, exp_down, ln2_g, ln2_b):
    depth = w_in.shape[0]
    for l in range(depth):
        x = _layer(x, w_in[l], pool_w[l], pool_scale[l], mu_shift[l], w0[l], w_up[l], a0[l],
                   a_up[l], g_up[l], k_k[l], k_a[l], r_k[l], lnx_g[l], lnx_b[l], w_out[l],
                   ln1_g[l], ln1_b[l], router_group[l], router_group_b[l], router_expert[l],
                   router_expert_b[l], exp_gate[l], exp_up[l], exp_down[l], ln2_g[l], ln2_b[l])
    return x
```

```python
import numpy as np
import jax
import jax.numpy as jnp
from jax import lax
from jax.experimental import pallas as pl
from jax.experimental.pallas import tpu as pltpu

F32 = jnp.float32
BF16 = jnp.bfloat16

D_MODEL = 1024
D_POOL = 256
POOL_WINDOWS = (2, 4, 8, 16)
POOL_GROUP = 64
POOL_HALO = 16
D_RWKV = 768
HEAD = 64
D_DECAY_LORA = 64
D_AAA_LORA = 64
D_GATE_LORA = 128
D_RWKV_IN = 3 * D_RWKV + D_DECAY_LORA + D_AAA_LORA + D_GATE_LORA
D_IN = D_POOL + D_RWKV_IN
N_GROUPS = 4
EXPERTS_PER_GROUP = 8
N_EXPERTS = 32
D_EXPERT = 256
LN_EPS = 1e-5
LNX_EPS = 64e-5
ALPHA = 2.0 ** 0.25
DECAY_SCALE = 0.6065306597126334
KK_NORM_FLOOR = 1e-24

LANES = 128
VMEM_LIMIT = 56 * 1024 * 1024

PAIR = 2 * HEAD
N_PAIRS = D_RWKV // PAIR
CHUNK = 64
RWKV_TILE = 8 * CHUNK
TS = 512
TM = 1024
BM = 256
SLAB = 8
EXPERT_LANE0 = 32
D_EXT = D_MODEL + LANES
META_CAT = 8
META_RANK = 9
PAIRS_PER_GROUP = EXPERTS_PER_GROUP * (EXPERTS_PER_GROUP - 1) // 2
N_CATS = N_GROUPS * PAIRS_PER_GROUP


def _pair_member():
    member = np.zeros((N_CATS, EXPERTS_PER_GROUP), bool)
    for g in range(N_GROUPS):
        pair = 0
        for lo in range(EXPERTS_PER_GROUP):
            for hi in range(lo + 1, EXPERTS_PER_GROUP):
                member[g * PAIRS_PER_GROUP + pair, [lo, hi]] = True
                pair += 1
    return member


_PAIR_MEMBER = _pair_member()


def _dot(a, b):
    return jnp.dot(a.astype(BF16), b.astype(BF16), preferred_element_type=F32)


def _dot_nt(a, b):
    return lax.dot_general(a.astype(BF16), b.astype(BF16), (((1,), (1,)), ((), ())),
                           preferred_element_type=F32)


def _dot_tn(a, b):
    return lax.dot_general(a.astype(BF16), b.astype(BF16), (((0,), (0,)), ((), ())),
                           preferred_element_type=F32)


def _head_sum(x, ones_bd):
    parts = [_dot(x[:, p * PAIR:(p + 1) * PAIR], ones_bd) for p in range(N_PAIRS)]
    return jnp.concatenate(parts, axis=1)


def _sigmoid(x):
    return 1.0 / (1.0 + jnp.exp(-x))


def _prep_kernel(x_ref, win_ref, mu_ref, w0_ref, a0_ref, lora_ref, gup_ref, kk_ref, ka_ref,
                 poolw_ref, pools_ref, ones_ref, rk_ref,
                 pool_o, r_o, w_o, k_o, v_o, a_o, b_o, g_o, bv_o,
                 proj_ref, halo_ref):
    i = pl.program_id(1)
    ts = x_ref.shape[1]

    @pl.when(i == 0)
    def _():
        halo_ref[...] = jnp.zeros_like(halo_ref)

    proj_ref[...] = jnp.dot(x_ref[0].astype(BF16), win_ref[...], preferred_element_type=F32)

    row = lax.broadcasted_iota(jnp.int32, (ts, 1), 0)

    def shifted(off, width):
        z = proj_ref[:, off:off + width]
        prev = jnp.where(row == 0, halo_ref[POOL_HALO - 1:POOL_HALO, off:off + width],
                         pltpu.roll(z, 1, 0))
        return z + (prev - z) * mu_ref[:, off - D_POOL:off - D_POOL + width]

    p = proj_ref[:, 0:D_POOL]
    ext = jnp.concatenate([halo_ref[:, 0:D_POOL], p], axis=0)
    s2 = ext + pltpu.roll(ext, 1, 0)
    s4 = s2 + pltpu.roll(s2, 2, 0)
    s8 = s4 + pltpu.roll(s4, 4, 0)
    s16 = s8 + pltpu.roll(s8, 8, 0)
    lane = lax.broadcasted_iota(jnp.int32, (ts, D_POOL), 1)
    grp = lane // POOL_GROUP
    wsum = jnp.where(grp == 0, s2[POOL_HALO:], jnp.where(grp == 1, s4[POOL_HALO:],
                     jnp.where(grp == 2, s8[POOL_HALO:], s16[POOL_HALO:])))
    w2, w4, w8, w16 = (float(wn) for wn in POOL_WINDOWS)
    win = jnp.where(grp == 0, w2, jnp.where(grp == 1, w4, jnp.where(grp == 2, w8, w16)))
    pos = (i * ts + row + 1).astype(F32)
    diff = wsum / jnp.minimum(pos, win) - p
    pool_o[0] = _dot(diff, poolw_ref[...]) * pools_ref[...]

    o = D_POOL
    r = shifted(o, D_RWKV)
    k = shifted(o + D_RWKV, D_RWKV)
    v = shifted(o + 2 * D_RWKV, D_RWKV)
    lw = shifted(o + 3 * D_RWKV, D_DECAY_LORA + D_AAA_LORA)
    gd = shifted(o + 3 * D_RWKV + D_DECAY_LORA + D_AAA_LORA, D_GATE_LORA)

    lane128 = lax.broadcasted_iota(jnp.int32, (ts, LANES), 1)
    lora_in = jnp.where(lane128 < D_DECAY_LORA, jnp.tanh(lw), lw)
    lora = _dot(lora_in, lora_ref[...])
    w_o[0] = -DECAY_SCALE * _sigmoid(w0_ref[...] + lora[:, 0:D_RWKV])
    eta = _sigmoid(a0_ref[...] + lora[:, D_RWKV:2 * D_RWKV])
    g_o[0] = _dot(_sigmoid(gd), gup_ref[...]).astype(g_o.dtype)
    kk = k * kk_ref[...]
    ss = _head_sum(kk * kk, ones_ref[...])
    kkn = kk * lax.rsqrt(jnp.maximum(ss, KK_NORM_FLOOR))
    k_mod = k * (1.0 + (eta - 1.0) * ka_ref[...])
    r_o[0] = r
    v_o[0] = v.astype(v_o.dtype)
    k_o[0] = k_mod
    bv_o[0] = (_head_sum(r * k_mod * rk_ref[...], ones_ref[...]) * v).astype(bv_o.dtype)
    a_o[0] = -kkn
    b_o[0] = kkn * eta

    halo_ref[...] = proj_ref[ts - POOL_HALO:ts, :]


def _prep_call(x, win_bf, mu, w0, a0, lora_w, gup_bf, k_k, k_a, poolw_bd, pool_scale, ones_bd, rk):
    b, s, _ = x.shape
    grid = (b, s // TS)
    full = lambda arr: pl.BlockSpec(arr.shape, lambda bi, i: (0,) * arr.ndim)
    tok = lambda width: pl.BlockSpec((1, TS, width), lambda bi, i: (bi, i, 0))
    dtypes = (F32, F32, F32, BF16, F32, F32, F32, F32)
    outs = ([jax.ShapeDtypeStruct((b, s, D_POOL), F32)]
            + [jax.ShapeDtypeStruct((b, s, D_RWKV), dt) for dt in dtypes])
    params = (win_bf, mu, w0, a0, lora_w, gup_bf, k_k, k_a, poolw_bd, pool_scale, ones_bd, rk)
    return pl.pallas_call(
        _prep_kernel,
        grid=grid,
        in_specs=[tok(D_MODEL)] + [full(a) for a in params],
        out_specs=[tok(D_POOL)] + [tok(D_RWKV)] * 8,
        out_shape=outs,
        scratch_shapes=[pltpu.VMEM((TS, D_IN), F32), pltpu.VMEM((POOL_HALO, D_IN), F32)],
        compiler_params=pltpu.CompilerParams(
            dimension_semantics=("parallel", "arbitrary"), vmem_limit_bytes=VMEM_LIMIT),
        name="prep",
    )(x, *params)


def _rwkv_kernel(r_ref, w_ref, k_ref, v_ref, a_ref, b_ref, eg_ref, eu_ref, ed_ref,
                 o_ref, eg_o, eu_o, ed_o, h_ref):
    c = pl.program_id(1)

    @pl.when(c == 0)
    def _():
        h_ref[...] = jnp.zeros_like(h_ref)

    eg_o[...] = eg_ref[...].astype(BF16)
    eu_o[...] = eu_ref[...].astype(BF16)
    ed_o[...] = ed_ref[...].astype(BF16)

    L = CHUNK
    n_chunks = r_ref.shape[1] // L
    row = lax.broadcasted_iota(jnp.int32, (L, PAIR), 0)
    lane = lax.broadcasted_iota(jnp.int32, (L, PAIR), 1)
    head0 = lane < HEAD
    strict = row > (lane & (L - 1))
    incl = row >= (lane & (L - 1))
    eye_wide = jnp.where(row == (lane & (L - 1)), 1.0, 0.0)

    def expand(x):
        xb = x.astype(BF16)
        zero = jnp.zeros_like(xb)
        return jnp.concatenate([jnp.where(head0, xb, zero), jnp.where(head0, zero, xb)], axis=0)

    def tile(ref, j, p):
        return ref[0, j * L:(j + 1) * L, p * PAIR:(p + 1) * PAIR]

    streams = [(j, p) for j in range(n_chunks) for p in range(N_PAIRS)]
    st = []
    for j, p in streams:
        w = tile(w_ref, j, p)
        cum = w
        for sh in (1, 2, 4, 8, 16, 32):
            cum = cum + jnp.where(row >= sh, pltpu.roll(cum, sh, 0), 0.0)
        tot = cum[L - 1:L, :]
        e_neg = jnp.exp(-cum)
        e_rem = jnp.exp(tot - cum)
        b = tile(b_ref, j, p)
        k = tile(k_ref, j, p)
        a_n = tile(a_ref, j, p) * jnp.exp(cum - w)
        st.append(dict(
            ar=jnp.concatenate([a_n, tile(r_ref, j, p) * jnp.exp(cum)], axis=0).astype(BF16),
            v_e=expand(tile(v_ref, j, p)),
            bk_t=jnp.concatenate([expand(b * e_neg), expand(k * e_neg)], axis=0),
            bk_h=jnp.concatenate([expand(b * e_rem), expand(k * e_rem)], axis=0),
            w_tot=jnp.exp(tot)))
    for d in st:
        sc = _dot_nt(d['ar'], d['bk_t'])
        d['t'] = jnp.where(strict, sc[0:L, 0:PAIR], 0.0)
        d['t_ak'] = jnp.where(strict, sc[0:L, PAIR:2 * PAIR], 0.0)
        d['r_all'] = jnp.concatenate([d['ar'][L:2 * L],
                                      jnp.where(incl, sc[L:2 * L, 0:PAIR], 0.0).astype(BF16),
                                      jnp.where(incl, sc[L:2 * L, PAIR:2 * PAIR], 0.0).astype(BF16)],
                                     axis=1)
    for d in st:
        d['takv'] = _dot(d['t_ak'], d['v_e'])
        d['m'] = eye_wide + d['t']
        d['t'] = _dot(d['t'], expand(d['t']))
    for step in range(5):
        for d in st:
            if step < 4:
                both = _dot(d['t'], jnp.concatenate([expand(d['t']), expand(d['m'])], axis=1))
                d['t'] = both[:, 0:PAIR]
                d['m'] = d['m'] + both[:, PAIR:2 * PAIR]
            else:
                d['m'] = d['m'] + _dot(d['t'], expand(d['m']))
    for d in st:
        d['pq'] = _dot(d['m'], jnp.concatenate([expand(d['ar'][0:L]), expand(d['takv'])], axis=1))

    h = [h_ref[p] for p in range(N_PAIRS)]
    for j in range(n_chunks):
        ds = st[j * N_PAIRS:(j + 1) * N_PAIRS]
        hb = [hp.astype(BF16) for hp in h]
        u_e = [expand(_dot(d['pq'][:, 0:PAIR], hb[p]) + d['pq'][:, PAIR:2 * PAIR])
               for p, d in enumerate(ds)]
        for p, d in enumerate(ds):
            o_ref[0, j * L:(j + 1) * L, p * PAIR:(p + 1) * PAIR] = _dot(
                d['r_all'], jnp.concatenate([hb[p], u_e[p], d['v_e']], axis=0))
        for p, d in enumerate(ds):
            h_add = _dot_tn(d['bk_h'], jnp.concatenate([u_e[p], d['v_e']], axis=0))
            w_col = jnp.transpose(jnp.broadcast_to(d['w_tot'], (PAIR, PAIR)))
            h[p] = h[p] * w_col + h_add
    for p in range(N_PAIRS):
        h_ref[p] = h[p]


def _rwkv_call(r, w, k, v, a, b, exp_gate, exp_up, exp_down):
    bsz, s, _ = r.shape
    n_c = s // RWKV_TILE
    assert N_EXPERTS % (bsz * n_c) == 0
    epb = N_EXPERTS // (bsz * n_c)
    spec = pl.BlockSpec((1, RWKV_TILE, D_RWKV), lambda bi, c: (bi, c, 0))
    wblk = lambda arr: pl.BlockSpec((epb,) + arr.shape[1:], lambda bi, c: (bi * n_c + c, 0, 0))
    experts = (exp_gate, exp_up, exp_down)
    return pl.pallas_call(
        _rwkv_kernel,
        grid=(bsz, n_c),
        in_specs=[spec] * 6 + [wblk(t) for t in experts],
        out_specs=[spec] + [wblk(t) for t in experts],
        out_shape=[jax.ShapeDtypeStruct((bsz, s, D_RWKV), F32)]
                  + [jax.ShapeDtypeStruct(t.shape, BF16) for t in experts],
        scratch_shapes=[pltpu.VMEM((N_PAIRS, PAIR, PAIR), F32)],
        compiler_params=pltpu.CompilerParams(
            dimension_semantics=("parallel", "arbitrary"), vmem_limit_bytes=VMEM_LIMIT),
        name="rwkv",
    )(r, w, k, v, a, b, *experts)


def _layer_norm(x, g, b):
    mu = jnp.mean(x, axis=-1, keepdims=True)
    xc = x - mu
    var = jnp.mean(xc * xc, axis=-1, keepdims=True)
    return xc * lax.rsqrt(var + LN_EPS) * g + b


def _post_kernel(x_ref, pool_ref, o_ref, bv_ref, g_ref,
                 lnxg_ref, lnxb_ref, ones_ref, wop_ref, wor_ref, ln1g_ref, ln1b_ref,
                 rw_hi_ref, rw_lo_ref, rb_ref, tri_ref,
                 h_o, meta_t_o, cnt_o, base_ref):
    i = pl.program_id(0)
    ts = x_ref.shape[0]

    @pl.when(i == 0)
    def _():
        base_ref[...] = jnp.zeros_like(base_ref)

    ones = ones_ref[...]
    o = o_ref[...]
    mu = _head_sum(o, ones) * (1.0 / HEAD)
    oc = o - mu
    var = _head_sum(oc * oc, ones) * (1.0 / HEAD)
    y = oc * lax.rsqrt(var + LNX_EPS) * lnxg_ref[...] + lnxb_ref[...]
    y = (y + bv_ref[...]) * g_ref[...]
    mixed = _dot(pool_ref[...], wop_ref[...]) + _dot(y, wor_ref[...])
    h = _layer_norm(ALPHA * x_ref[...] + mixed, ln1g_ref[...], ln1b_ref[...])
    h_o[:, 0:D_MODEL] = h

    h_hi = h.astype(BF16)
    h_lo = (h - h_hi.astype(F32)).astype(BF16)
    logits = (_dot_nt(rw_hi_ref[...], h_hi) + _dot_nt(rw_hi_ref[...], h_lo)
              + _dot_nt(rw_lo_ref[...], h_hi)) + rb_ref[...]
    neg = -jnp.inf
    row8 = lax.broadcasted_iota(jnp.int32, (8, ts), 0)
    gl = jnp.where(row8 < N_GROUPS, logits[0:8], neg)
    gmax = jnp.max(gl, axis=0, keepdims=True)
    g_idx = jnp.min(jnp.where(gl == gmax, row8, LANES), axis=0, keepdims=True)
    g_top_p = 1.0 / jnp.sum(jnp.exp(gl - gmax), axis=0, keepdims=True)
    el = logits[EXPERT_LANE0:EXPERT_LANE0 + EXPERTS_PER_GROUP]
    for g in range(1, N_GROUPS):
        lo_row = EXPERT_LANE0 + g * EXPERTS_PER_GROUP
        el = jnp.where(g_idx == g, logits[lo_row:lo_row + EXPERTS_PER_GROUP], el)
    m1 = jnp.max(el, axis=0, keepdims=True)
    l1 = jnp.min(jnp.where(el == m1, row8, LANES), axis=0, keepdims=True)
    el2 = jnp.where(row8 == l1, neg, el)
    m2 = jnp.max(el2, axis=0, keepdims=True)
    l2 = jnp.min(jnp.where(el2 == m2, row8, LANES), axis=0, keepdims=True)
    e21 = jnp.exp(m2 - m1)
    wgt1 = g_top_p / (1.0 + e21)
    wgt2 = g_top_p * e21 / (1.0 + e21)

    lo = jnp.minimum(l1, l2)
    hi = jnp.maximum(l1, l2)
    cat = g_idx * PAIRS_PER_GROUP + jnp.right_shift(lo * (2 * EXPERTS_PER_GROUP - 1 - lo), 1) + (hi - lo - 1)
    row = lax.broadcasted_iota(jnp.int32, (LANES, ts), 0)
    in_cat = row == cat
    onehot = jnp.where(in_cat, 1.0, 0.0)
    before = jnp.dot(onehot.astype(BF16), tri_ref[...], preferred_element_type=F32)
    posn = base_ref[...] + before
    rank = jnp.sum(jnp.where(in_cat, posn, 0.0), axis=0, keepdims=True)
    new_base = base_ref[...] + jnp.sum(onehot, axis=1, keepdims=True)
    base_ref[...] = new_base
    cnt_o[...] = jnp.broadcast_to(new_base, cnt_o.shape)

    meta_t = jnp.where(row8 == l1, wgt1, jnp.where(row8 == l2, wgt2, 0.0))
    tail_t = jnp.where(row8 == 0, cat.astype(F32), jnp.where(row8 == 1, rank, 0.0))
    meta_t_o[...] = tail_t
    rec_t = jnp.concatenate([meta_t, tail_t, jnp.zeros((LANES - 16, ts), F32)], axis=0)
    h_o[:, D_MODEL:D_EXT] = jnp.transpose(rec_t)


def _post_call(x2, pool2, o2, bv2, g2, lnx_g, lnx_b, ones_bd, wo_pool, wo_rwkv,
               ln1_g, ln1_b, rw_hi, rw_lo, rb, tri):
    n = x2.shape[0]
    full = lambda arr: pl.BlockSpec(arr.shape, lambda i: (0,) * arr.ndim)
    tok = lambda width: pl.BlockSpec((TS, width), lambda i: (i, 0))
    params = (lnx_g, lnx_b, ones_bd, wo_pool, wo_rwkv, ln1_g, ln1_b, rw_hi, rw_lo, rb, tri)
    return pl.pallas_call(
        _post_kernel,
        grid=(n // TS,),
        in_specs=[tok(D_MODEL), tok(D_POOL)] + [tok(D_RWKV)] * 3 + [full(a) for a in params],
        out_specs=[tok(D_EXT), pl.BlockSpec((8, TS), lambda i: (0, i)),
                   pl.BlockSpec((LANES, LANES), lambda i: (0, 0))],
        out_shape=[jax.ShapeDtypeStruct((n, D_EXT), F32), jax.ShapeDtypeStruct((8, n), F32),
                   jax.ShapeDtypeStruct((LANES, LANES), F32)],
        scratch_shapes=[pltpu.VMEM((LANES, 1), F32)],
        compiler_params=pltpu.CompilerParams(
            dimension_semantics=("arbitrary",), vmem_limit_bytes=VMEM_LIMIT),
        name="post",
    )(x2, pool2, o2, bv2, g2, *params)


def _start_rows(ts, row_copy):
    def issue(g, carry):
        for u in range(SLAB):
            row_copy(g, u).start(priority=u % 2)
        return carry

    lax.fori_loop(0, ts // SLAB, issue, 0)


def _dispatch_kernel(pad_start_ref, pad_cnt_ref, n_used_ref, dest_ref, h_ref, xs_ref, zero_ref,
                     stage_ref, sem, zsem):
    i = pl.program_id(0)
    ts = h_ref.shape[0]
    n_blocks = xs_ref.shape[0] * SLAB // BM
    slot = lax.rem(i, 2)

    def wait_tile(s):
        pltpu.make_async_copy(stage_ref.at[s], xs_ref.at[pl.ds(0, ts // SLAB)], sem.at[s]).wait()

    def zero_row_copy(row):
        return pltpu.make_async_copy(zero_ref.at[0, pl.ds(0, 1)],
                                     xs_ref.at[row // SLAB, pl.ds(lax.rem(row, SLAB), 1)], zsem)

    def zero_block_copy(blk):
        return pltpu.make_async_copy(zero_ref, xs_ref.at[pl.ds(blk * (BM // SLAB), BM // SLAB)], zsem)

    @pl.when(i == 0)
    def _():
        zero_ref[...] = jnp.zeros_like(zero_ref)

        def fill(e, carry):
            start = pad_start_ref[e]

            def one(j, c):
                zero_row_copy(start + j).start()
                return c

            return lax.fori_loop(0, pad_cnt_ref[e], one, carry)

        def fill_wait(e, carry):
            def one(j, c):
                zero_row_copy(0).wait()
                return c

            return lax.fori_loop(0, pad_cnt_ref[e], one, carry)

        def tail(blk, carry):
            zero_block_copy(blk).start()
            return carry

        def tail_wait(blk, carry):
            zero_block_copy(blk).wait()
            return carry

        lax.fori_loop(0, N_GROUPS, fill, 0)
        lax.fori_loop(n_used_ref[0], n_blocks, tail, 0)
        lax.fori_loop(0, N_GROUPS, fill_wait, 0)
        lax.fori_loop(n_used_ref[0], n_blocks, tail_wait, 0)

    for s in range(2):
        @pl.when(slot == s)
        def _(s=s):
            stage_ref[s] = h_ref[...].reshape(ts // SLAB, SLAB, D_EXT)
            _start_rows(ts, lambda g, u: pltpu.make_async_copy(
                stage_ref.at[s, g, pl.ds(u, 1)],
                xs_ref.at[dest_ref[0, 0, g * SLAB + u], pl.ds(dest_ref[0, 0, ts + g * SLAB + u], 1)],
                sem.at[s]))

    @pl.when(i > 0)
    def _():
        wait_tile(1 - slot)

    @pl.when(i == pl.num_programs(0) - 1)
    def _():
        wait_tile(slot)


def _dispatch_call(pad_start, pad_cnt, n_used, dest, h_ext, m_pad):
    n = h_ext.shape[0]
    grid_spec = pltpu.PrefetchScalarGridSpec(
        num_scalar_prefetch=3,
        grid=(n // TM,),
        in_specs=[pl.BlockSpec((1, 1, 2 * TM), lambda i, *_: (i, 0, 0), memory_space=pltpu.SMEM),
                  pl.BlockSpec((TM, D_EXT), lambda i, *_: (i, 0))],
        out_specs=pl.BlockSpec(memory_space=pl.ANY),
        scratch_shapes=[pltpu.VMEM((BM // SLAB, SLAB, D_EXT), F32),
                        pltpu.VMEM((2, TM // SLAB, SLAB, D_EXT), F32),
                        pltpu.SemaphoreType.DMA((2,)), pltpu.SemaphoreType.DMA(())],
    )
    return pl.pallas_call(
        _dispatch_kernel,
        grid_spec=grid_spec,
        out_shape=jax.ShapeDtypeStruct((m_pad // SLAB, SLAB, D_EXT), F32),
        compiler_params=pltpu.CompilerParams(
            dimension_semantics=("arbitrary",), vmem_limit_bytes=VMEM_LIMIT,
            disable_bounds_checks=True),
        name="dispatch",
    )(pad_start, pad_cnt, n_used, dest, h_ext)


def _expert_kernel(blk_g_ref, n_used_ref, act_ref, xs_ref, wg_ref, wu_ref, wd_ref, ys_ref, xb_ref):
    i = pl.program_id(0)

    @pl.when(i < n_used_ref[0])
    def _():
        xb_ref[...] = xs_ref[:, 0:D_MODEL].astype(BF16)

        def expert_out(e):
            xb = xb_ref[...]
            gate = jnp.dot(xb, wg_ref[0, e], preferred_element_type=F32)
            up = jnp.dot(xb, wu_ref[0, e], preferred_element_type=F32)
            w_e = xs_ref[:, D_MODEL + e:D_MODEL + e + 1]
            hid = jnp.where(w_e != 0.0, gate * _sigmoid(gate) * up * w_e, 0.0)
            return jnp.dot(hid.astype(BF16), wd_ref[0, e * D_EXPERT:(e + 1) * D_EXPERT],
                           preferred_element_type=F32)

        for e in range(EXPERTS_PER_GROUP):
            flag = act_ref[i * EXPERTS_PER_GROUP + e]

            @pl.when(flag == 2)
            def _(e=e):
                ys_ref[...] = expert_out(e)

            @pl.when(flag == 1)
            def _(e=e):
                ys_ref[...] += expert_out(e)

    @pl.when(i >= n_used_ref[0])
    def _():
        ys_ref[...] = jnp.zeros_like(ys_ref)


def _expert_call(blk_g, n_used, act, xs, wg, wu, wd):
    m_pad = xs.shape[0]
    n_blocks = m_pad // BM
    row_blk = lambda i, bg, nu, ac: (jnp.minimum(i, nu[0] - 1), 0)
    out_blk = lambda i, bg, nu, ac: (i, 0)
    grp = lambda i, bg, nu: bg[jnp.minimum(i, nu[0] - 1)]
    grid_spec = pltpu.PrefetchScalarGridSpec(
        num_scalar_prefetch=3,
        grid=(n_blocks,),
        in_specs=[pl.BlockSpec((BM, D_EXT), row_blk),
                  pl.BlockSpec((1, EXPERTS_PER_GROUP, D_MODEL, D_EXPERT),
                               lambda i, bg, nu, ac: (grp(i, bg, nu), 0, 0, 0)),
                  pl.BlockSpec((1, EXPERTS_PER_GROUP, D_MODEL, D_EXPERT),
                               lambda i, bg, nu, ac: (grp(i, bg, nu), 0, 0, 0)),
                  pl.BlockSpec((1, EXPERTS_PER_GROUP * D_EXPERT, D_MODEL),
                               lambda i, bg, nu, ac: (grp(i, bg, nu), 0, 0))],
        out_specs=pl.BlockSpec((BM, D_MODEL), out_blk),
        scratch_shapes=[pltpu.VMEM((BM, D_MODEL), BF16)],
    )
    return pl.pallas_call(
        _expert_kernel,
        grid_spec=grid_spec,
        out_shape=jax.ShapeDtypeStruct((m_pad, D_MODEL), F32),
        compiler_params=pltpu.CompilerParams(
            dimension_semantics=("arbitrary",), vmem_limit_bytes=VMEM_LIMIT),
        name="experts",
    )(blk_g, n_used, act, xs, wg, wu, wd)


def _combine_kernel(dest_ref, dest_next_ref, h_ref, ys_ref, g_ref, b_ref, out_ref, y_ref, sem):
    i = pl.program_id(0)
    n_steps = pl.num_programs(0)
    ts = h_ref.shape[0]

    def gather(idx_ref, buf):
        _start_rows(ts, lambda g, u: pltpu.make_async_copy(
            ys_ref.at[idx_ref[0, 0, g * SLAB + u], pl.ds(idx_ref[0, 0, ts + g * SLAB + u], 1)],
            y_ref.at[buf, g, pl.ds(u, 1)], sem.at[buf]))

    cur = lax.rem(i, 2)

    @pl.when(i == 0)
    def _():
        gather(dest_ref, 0)

    for buf in range(2):
        @pl.when((i + 1 < n_steps) & (cur != buf))
        def _(buf=buf):
            gather(dest_next_ref, buf)

    pltpu.make_async_copy(ys_ref.at[pl.ds(0, ts // SLAB)], y_ref.at[cur], sem.at[cur]).wait()
    y = y_ref[cur].reshape(ts, D_MODEL)
    out_ref[...] = _layer_norm(ALPHA * h_ref[...] + y, g_ref[...], b_ref[...])


def _combine_call(dest, h_ext, ys, ln2_g, ln2_b):
    n = h_ext.shape[0]
    full = lambda arr: pl.BlockSpec(arr.shape, lambda i: (0,) * arr.ndim)
    return pl.pallas_call(
        _combine_kernel,
        grid=(n // TM,),
        in_specs=[pl.BlockSpec((1, 1, 2 * TM), lambda i: (i, 0, 0), memory_space=pltpu.SMEM),
                  pl.BlockSpec((1, 1, 2 * TM), lambda i: (jnp.minimum(i + 1, n // TM - 1), 0, 0),
                               memory_space=pltpu.SMEM),
                  pl.BlockSpec((TM, D_MODEL), lambda i: (i, 0)),
                  pl.BlockSpec(memory_space=pl.ANY),
                  full(ln2_g), full(ln2_b)],
        out_specs=pl.BlockSpec((TM, D_MODEL), lambda i: (i, 0)),
        out_shape=jax.ShapeDtypeStruct((n, D_MODEL), F32),
        scratch_shapes=[pltpu.VMEM((2, TM // SLAB, SLAB, D_MODEL), F32),
                        pltpu.SemaphoreType.DMA((2,))],
        compiler_params=pltpu.CompilerParams(
            dimension_semantics=("arbitrary",), vmem_limit_bytes=VMEM_LIMIT,
            disable_bounds_checks=True),
        name="combine",
    )(dest, dest, h_ext, ys, ln2_g, ln2_b)


def _block_diag(blocks):
    g, c, d = blocks.shape
    eye = jnp.eye(g, dtype=blocks.dtype)
    return (eye[:, None, :, None] * blocks[:, :, None, :]).reshape(g * c, g * d)


def _layer(x, w_in, pool_w, pool_scale, mu_shift, w0, w_up, a0, a_up, g_up, k_k, k_a, r_k,
           lnx_g, lnx_b, w_out, ln1_g, ln1_b, router_group, router_group_b, router_expert,
           router_expert_b, exp_gate, exp_up, exp_down, ln2_g, ln2_b):
    b, s, d = x.shape
    n = b * s
    row2 = lambda t: t.reshape(1, -1)

    ones_bd = _block_diag(jnp.ones((2, HEAD, HEAD), BF16))
    lora_w = _block_diag(jnp.stack([w_up, a_up])).astype(BF16)
    poolw_bd = _block_diag(pool_w).astype(BF16)
    rw = jnp.zeros((LANES, D_MODEL), F32)
    rw = rw.at[0:N_GROUPS].set(router_group.T).at[EXPERT_LANE0:EXPERT_LANE0 + N_EXPERTS].set(router_expert.T)
    rw_hi = rw.astype(BF16)
    rw_lo = (rw - rw_hi.astype(F32)).astype(BF16)
    rb = jnp.zeros((LANES, 1), F32)
    rb = rb.at[0:N_GROUPS, 0].set(router_group_b).at[EXPERT_LANE0:EXPERT_LANE0 + N_EXPERTS, 0].set(router_expert_b)
    tri = (lax.broadcasted_iota(jnp.int32, (TS, TS), 0) < lax.broadcasted_iota(jnp.int32, (TS, TS), 1)).astype(BF16)

    pool_out, r, w, k, v, a_vec, b_vec, g, bv = _prep_call(
        x, w_in.astype(BF16), row2(mu_shift), row2(w0), row2(a0), lora_w, g_up.astype(BF16),
        row2(k_k), row2(k_a), poolw_bd, row2(pool_scale), ones_bd, row2(r_k))
    o, eg_bf, eu_bf, ed_bf = _rwkv_call(r, w, k, v, a_vec, b_vec, exp_gate, exp_up, exp_down)

    flat = lambda t: t.reshape(n, t.shape[-1])
    w_out_bf = w_out.astype(BF16)
    h_ext, meta_t, counts = _post_call(
        flat(x), flat(pool_out), flat(o), flat(bv), flat(g),
        row2(lnx_g), row2(lnx_b), ones_bd, w_out_bf[0:D_POOL], w_out_bf[D_POOL:],
        row2(ln1_g), row2(ln1_b), rw_hi, rw_lo, rb, tri)

    cnt_cat = counts[0:N_CATS, 0].astype(jnp.int32).reshape(N_GROUPS, PAIRS_PER_GROUP)
    cnt = jnp.sum(cnt_cat, axis=1)
    padded = (cnt + BM - 1) // BM * BM
    pend = jnp.cumsum(padded)
    pstart = pend - padded
    cat_start = (pstart[:, None] + jnp.cumsum(cnt_cat, axis=1) - cnt_cat).reshape(N_CATS)
    cnt_cat = cnt_cat.reshape(N_CATS)
    cats = meta_t[0].astype(jnp.int32)
    onehot = cats[:, None] == jnp.arange(N_CATS, dtype=jnp.int32)
    dest = jnp.sum(jnp.where(onehot, cat_start, 0), axis=-1) + meta_t[1].astype(jnp.int32)
    dest = jnp.concatenate([(dest // SLAB).reshape(n // TM, 1, TM),
                            (dest % SLAB).reshape(n // TM, 1, TM)], axis=2)
    m_pad = n + N_GROUPS * BM
    n_blocks = m_pad // BM
    blk_start = jnp.arange(n_blocks, dtype=jnp.int32) * BM
    blk_g = jnp.minimum(jnp.sum(blk_start[:, None] >= pend[None, :], axis=1), N_GROUPS - 1).astype(jnp.int32)
    n_used = (pend[-1:] // BM).astype(jnp.int32)
    overlap = ((cat_start[None, :] < blk_start[:, None] + BM)
               & (cat_start[None, :] + cnt_cat[None, :] > blk_start[:, None])
               & (cnt_cat[None, :] > 0))
    act = jnp.any(overlap[:, :, None] & _PAIR_MEMBER[None], axis=1).astype(jnp.int32)
    first = jnp.cumsum(act, axis=1) == 1
    act = (act + (first & (act == 1))).reshape(-1)

    grouped = lambda wts: wts.reshape(N_GROUPS, EXPERTS_PER_GROUP, *wts.shape[1:])
    wd = ed_bf.reshape(N_GROUPS, EXPERTS_PER_GROUP * D_EXPERT, D_MODEL)
    xs = _dispatch_call(pstart + cnt, padded - cnt, n_used, dest, h_ext, m_pad)
    xs = xs.reshape(m_pad, D_EXT)
    ys = _expert_call(blk_g, n_used, act, xs, grouped(eg_bf), grouped(eu_bf), wd)
    out = _combine_call(dest, h_ext, ys.reshape(m_pad // SLAB, SLAB, D_MODEL),
                        row2(ln2_g), row2(ln2_b))
    return out.reshape(b, s, d)


def kernel(x, w_in, pool_w, pool_scale, mu_shift, w0, w_up, a0, a_up, g_up, k_k, k_a, r_k, lnx_g, lnx_b, w_out, ln1_g, ln1_b, router_group, router_group_b, router_expert, router_expert_b, exp_gate, exp_up, exp_down, ln2_g, ln2_b):
    depth = w_in.shape[0]
    for l in range(depth):
        x = _layer(x, w_in[l], pool_w[l], pool_scale[l], mu_shift[l], w0[l], w_up[l], a0[l],
                   a_up[l], g_up[l], k_k[l], k_a[l], r_k[l], lnx_g[l], lnx_b[l], w_out[l],
                   ln1_g[l], ln1_b[l], router_group[l], router_group_b[l], router_expert[l],
                   router_expert_b[l], exp_gate[l], exp_up[l], exp_down[l], ln2_g[l], ln2_b[l])
    return x
```
